```python
import jax, jax.numpy as jnp
from jax import lax
import numpy as np

D_MODEL = 2048
BATCH = 1
SEQ = 8192
DEPTH = 4
DEC_BATCH = 16
DEC_SEQ = 16
PAST_LEN = 2048

CHUNK = 64
N_A_LAYERS = DEPTH // 2
N_B_LAYERS = DEPTH - N_A_LAYERS
A_HEADS = 6
A_DQK = 128
A_DV = 256
SB_HEADS = 12
SB_HD = 128
MEM_HEADS = 4
MEM_HD = 128
N_MEM = 256
D_FF = 5632
SB_Q_BLOCK = 128
RMS_EPS = 1e-6
F_BIAS_INIT = 3.0

A_Q = A_HEADS * A_DQK
A_V = A_HEADS * A_DV
MEM_W = MEM_HEADS * MEM_HD
SB_W = SB_HEADS * SB_HD
A_IN_DIM = 2 * A_Q + 2 * A_V + 2 * A_HEADS + MEM_W
B_IN_DIM = SB_W + MEM_W
MIX_OUT = A_V + MEM_W

kernel_name = "yoco_mlstm_stickbreaking_stream_step"


def rms_norm(x, g):
    xf = x.astype(jnp.float32)
    y = xf * lax.rsqrt(jnp.mean(xf * xf, axis=-1, keepdims=True) + RMS_EPS)
    return (y * g.astype(jnp.float32)).astype(x.dtype)


def swiglu(h, w_gate, w_up, w_down):
    return (jax.nn.silu(h @ w_gate) * (h @ w_up)) @ w_down


def mlstm_chunkwise(q, k, v, ig, lf, C0, n0, m0, chunk):
    B, T, H, DQK = q.shape
    DV = v.shape[-1]
    nc = T // chunk
    f32 = jnp.float32

    def blocks(a):
        a = a.astype(f32).reshape((B, nc, chunk, H) + a.shape[3:])
        return jnp.moveaxis(a, 3, 2).swapaxes(0, 1)

    qc = blocks(q) * (DQK ** -0.5)
    kc, vc, igc, lfc = blocks(k), blocks(v), blocks(ig), blocks(lf)
    causal = jnp.tril(jnp.ones((chunk, chunk), dtype=bool))

    def step(carry, inp):
        C, n, m = carry
        qb, kb, vb, ib, fb = inp
        b = jnp.cumsum(fb, axis=-1)
        d = jnp.where(causal, b[..., :, None] - b[..., None, :] + ib[..., None, :], -jnp.inf)
        inter = b + m[..., None]
        m_t = jnp.maximum(inter, jnp.max(d, axis=-1))
        w_in = jnp.exp(d - m_t[..., None])
        w_st = jnp.exp(inter - m_t)
        s = jnp.einsum("bhtd,bhsd->bhts", qb, kb) * w_in
        num = jnp.einsum("bhts,bhsv->bhtv", s, vb) + w_st[..., None] * jnp.einsum("bhvd,bhtd->bhtv", C, qb)
        den = jnp.sum(s, axis=-1) + w_st * jnp.einsum("bhd,bhtd->bht", n, qb)
        h = num / jnp.maximum(jnp.abs(den), jnp.exp(-m_t))[..., None]
        b_end = b[..., -1]
        g = b_end[..., None] - b + ib
        m_new = jnp.maximum(b_end + m, jnp.max(g, axis=-1))
        w_k = jnp.exp(g - m_new[..., None])
        decay = jnp.exp(b_end + m - m_new)
        C_new = decay[..., None, None] * C + jnp.einsum("bhsv,bhsd->bhvd", vb * w_k[..., None], kb)
        n_new = decay[..., None] * n + jnp.einsum("bhs,bhsd->bhd", w_k, kb)
        return (C_new, n_new, m_new), h

    (C, n, m), hs = lax.scan(step, (C0.astype(f32), n0.astype(f32), m0.astype(f32)),
                             (qc, kc, vc, igc, lfc))
    h = jnp.moveaxis(hs.swapaxes(0, 1), 2, 3).reshape(B, T, H, DV)
    return h, C, n, m


def sb_block(qb, qpos, k, v, kpos):
    z = jnp.einsum("bqhd,bshd->bhqs", qb, k).astype(jnp.float32) * (SB_HD ** -0.5)
    mask = kpos[None, :] < qpos[:, None]
    u = jnp.where(mask, jax.nn.log_sigmoid(-z), 0.0)
    rest = lax.cumsum(u, axis=3, reverse=True) - u
    a = jnp.where(mask, jnp.exp(jax.nn.log_sigmoid(z) + rest), 0.0)
    return jnp.einsum("bhqs,bshd->bqhd", a.astype(v.dtype), v)


def stick_breaking(q, q_pos, k, v, k_pos):
    B, T, H, D = q.shape
    blk = min(SB_Q_BLOCK, T)
    nb = T // blk
    qs = q.reshape(B, nb, blk, H, D).swapaxes(0, 1)
    ps = q_pos.reshape(nb, blk)
    out = lax.map(lambda a: sb_block(a[0], a[1], k, v, k_pos), (qs, ps))
    return out.swapaxes(0, 1).reshape(B, T, H, D)


def mem_kv(mem, g, w_k, w_v, k_gain):
    B, N, _ = mem.shape
    h = rms_norm(mem, g)
    mk = rms_norm((h @ w_k).reshape(B, N, MEM_HEADS, MEM_HD), k_gain)
    mv = (h @ w_v).reshape(B, N, MEM_HEADS, MEM_HD)
    return mk, mv


def mem_attention(mq, mk, mv, q_gain):
    qn = rms_norm(mq, q_gain)
    s = jnp.einsum("bthd,bnhd->bhtn", qn, mk).astype(jnp.float32) * (MEM_HD ** -0.5)
    p = jax.nn.softmax(s, axis=-1)
    return jnp.einsum("bhtn,bnhd->bthd", p.astype(mv.dtype), mv)


def trunk(x, C0, n0, m0, past_k, past_v, mem_k, mem_v, p):
    B, T, _ = x.shape
    pos0 = 0 if past_k is None else past_k.shape[1]
    q_pos = pos0 + jnp.arange(T, dtype=jnp.int32)
    chunk = min(CHUNK, T)
    Cs, ns, ms = [], [], []
    k_all = v_all = k_pos = k_new = v_new = None
    for l in range(DEPTH):
        x = x + 0.5 * swiglu(rms_norm(x, p["ffn1_norm"][l]), p["ffn1_w_gate"][l], p["ffn1_w_up"][l], p["ffn1_w_down"][l])
        h = rms_norm(x, p["mix_norm"][l])
        if l < N_A_LAYERS:
            proj = h @ p["a_w_in"][l]
            q, k, v, o, ig, fg, mq = jnp.split(
                proj, [A_Q, 2 * A_Q, 2 * A_Q + A_V, 2 * A_Q + 2 * A_V,
                       2 * A_Q + 2 * A_V + A_HEADS, 2 * A_Q + 2 * A_V + 2 * A_HEADS], axis=-1)
            log_i = (ig + p["a_b_i"][l]).astype(jnp.float32)
            log_f = jax.nn.log_sigmoid((fg + p["a_b_f"][l]).astype(jnp.float32))
            hA, C, n, m = mlstm_chunkwise(q.reshape(B, T, A_HEADS, A_DQK), k.reshape(B, T, A_HEADS, A_DQK),
                                          v.reshape(B, T, A_HEADS, A_DV), log_i, log_f, C0[l], n0[l], m0[l], chunk)
            Cs.append(C); ns.append(n); ms.append(m)
            hA = rms_norm(hA, p["a_head_norm"][l]) * jax.nn.sigmoid(o.reshape(B, T, A_HEADS, A_DV).astype(jnp.float32))
            tok = hA.reshape(B, T, A_V).astype(x.dtype)
        else:
            proj = h @ p["b_w_in"][l - N_A_LAYERS]
            qsb, mq = jnp.split(proj, [SB_W], axis=-1)
            tok = stick_breaking(qsb.reshape(B, T, SB_HEADS, SB_HD), q_pos, k_all, v_all, k_pos).reshape(B, T, SB_W)
        mo = mem_attention(mq.reshape(B, T, MEM_HEADS, MEM_HD), mem_k[l], mem_v[l], p["mem_q_norm"][l]).reshape(B, T, MEM_W)
        x = x + jnp.concatenate([tok, mo.astype(tok.dtype)], axis=-1) @ p["w_out"][l]
        x = x + 0.5 * swiglu(rms_norm(x, p["ffn2_norm"][l]), p["ffn2_w_gate"][l], p["ffn2_w_up"][l], p["ffn2_w_down"][l])
        if l == N_A_LAYERS - 1:
            hk = rms_norm(x, p["kv_norm"])
            k_new = (hk @ p["sb_w_k"]).reshape(B, T, SB_HEADS, SB_HD)
            v_new = (hk @ p["sb_w_v"]).reshape(B, T, SB_HEADS, SB_HD)
            if past_k is None:
                k_all, v_all = k_new, v_new
            else:
                k_all = jnp.concatenate([past_k, k_new.astype(past_k.dtype)], axis=1)
                v_all = jnp.concatenate([past_v, v_new.astype(past_v.dtype)], axis=1)
            k_pos = jnp.arange(pos0 + T, dtype=jnp.int32)
    return x, jnp.stack(Cs), jnp.stack(ns), jnp.stack(ms), k_new, v_new


def setup_inputs(seed: int = 0) -> dict:
    key = jax.random.key(seed)
    ks = jax.random.split(key, 40)
    f32 = jnp.float32

    def nrm(i, shape, scale):
        return jax.random.normal(ks[i], shape, f32) * scale

    def gain(i, shape):
        return 1.0 + nrm(i, shape, 0.02)

    return {
        "x_prompt": nrm(0, (BATCH, SEQ, D_MODEL), 1.0),
        "x_sample": nrm(1, (DEC_BATCH, DEC_SEQ, D_MODEL), 1.0),
        "state_mlstm_C": nrm(2, (N_A_LAYERS, DEC_BATCH, A_HEADS, A_DV, A_DQK), 0.5),
        "state_mlstm_n": nrm(3, (N_A_LAYERS, DEC_BATCH, A_HEADS, A_DQK), 0.5),
        "state_mlstm_m": nrm(4, (N_A_LAYERS, DEC_BATCH, A_HEADS), 1.0),
        "cache_sb_k": nrm(5, (DEC_BATCH, PAST_LEN, SB_HEADS, SB_HD), 1.0),
        "cache_sb_v": nrm(6, (DEC_BATCH, PAST_LEN, SB_HEADS, SB_HD), 1.0),
        "cache_mem_k": nrm(7, (DEPTH, DEC_BATCH, N_MEM, MEM_HEADS, MEM_HD), 1.0),
        "cache_mem_v": nrm(8, (DEPTH, DEC_BATCH, N_MEM, MEM_HEADS, MEM_HD), 1.0),
        "mem_prompt": nrm(9, (BATCH, N_MEM, D_MODEL), 1.0),
        "ffn1_norm": gain(10, (DEPTH, D_MODEL)),
        "ffn1_w_gate": nrm(11, (DEPTH, D_MODEL, D_FF), D_MODEL ** -0.5),
        "ffn1_w_up": nrm(12, (DEPTH, D_MODEL, D_FF), D_MODEL ** -0.5),
        "ffn1_w_down": nrm(13, (DEPTH, D_FF, D_MODEL), D_FF ** -0.5),
        "ffn2_norm": gain(14, (DEPTH, D_MODEL)),
        "ffn2_w_gate": nrm(15, (DEPTH, D_MODEL, D_FF), D_MODEL ** -0.5),
        "ffn2_w_up": nrm(16, (DEPTH, D_MODEL, D_FF), D_MODEL ** -0.5),
        "ffn2_w_down": nrm(17, (DEPTH, D_FF, D_MODEL), D_FF ** -0.5),
        "mix_norm": gain(18, (DEPTH, D_MODEL)),
        "a_w_in": nrm(19, (N_A_LAYERS, D_MODEL, A_IN_DIM), D_MODEL ** -0.5),
        "a_b_i": nrm(20, (N_A_LAYERS, A_HEADS), 0.1),
        "a_b_f": F_BIAS_INIT + nrm(21, (N_A_LAYERS, A_HEADS), 0.1),
        "a_head_norm": gain(22, (N_A_LAYERS, A_HEADS, A_DV)),
        "b_w_in": nrm(23, (N_B_LAYERS, D_MODEL, B_IN_DIM), D_MODEL ** -0.5),
        "w_out": nrm(24, (DEPTH, MIX_OUT, D_MODEL), MIX_OUT ** -0.5),
        "mem_norm": gain(25, (DEPTH, D_MODEL)),
        "mem_w_k": nrm(26, (DEPTH, D_MODEL, MEM_W), D_MODEL ** -0.5),
        "mem_w_v": nrm(27, (DEPTH, D_MODEL, MEM_W), D_MODEL ** -0.5),
        "mem_q_norm": gain(28, (DEPTH, MEM_HD)),
        "mem_k_norm": gain(29, (DEPTH, MEM_HD)),
        "kv_norm": gain(30, (D_MODEL,)),
        "sb_w_k": nrm(31, (D_MODEL, SB_W), D_MODEL ** -0.5),
        "sb_w_v": nrm(32, (D_MODEL, SB_W), D_MODEL ** -0.5),
    }


def reference(x_prompt, x_sample, state_mlstm_C, state_mlstm_n, state_mlstm_m, cache_sb_k, cache_sb_v,
              cache_mem_k, cache_mem_v, mem_prompt, ffn1_norm, ffn1_w_gate, ffn1_w_up, ffn1_w_down,
              ffn2_norm, ffn2_w_gate, ffn2_w_up, ffn2_w_down, mix_norm, a_w_in, a_b_i, a_b_f, a_head_norm,
              b_w_in, w_out, mem_norm, mem_w_k, mem_w_v, mem_q_norm, mem_k_norm, kv_norm, sb_w_k, sb_w_v):
    p = {
        "ffn1_norm": ffn1_norm, "ffn1_w_gate": ffn1_w_gate, "ffn1_w_up": ffn1_w_up, "ffn1_w_down": ffn1_w_down,
        "ffn2_norm": ffn2_norm, "ffn2_w_gate": ffn2_w_gate, "ffn2_w_up": ffn2_w_up, "ffn2_w_down": ffn2_w_down,
        "mix_norm": mix_norm, "a_w_in": a_w_in, "a_b_i": a_b_i, "a_b_f": a_b_f, "a_head_norm": a_head_norm,
        "b_w_in": b_w_in, "w_out": w_out, "mem_q_norm": mem_q_norm,
        "kv_norm": kv_norm, "sb_w_k": sb_w_k, "sb_w_v": sb_w_v,
    }
    mks, mvs = [], []
    for l in range(DEPTH):
        mk, mv = mem_kv(mem_prompt, mem_norm[l], mem_w_k[l], mem_w_v[l], mem_k_norm[l])
        mks.append(mk); mvs.append(mv)
    mem_k_prompt = jnp.stack(mks)
    mem_v_prompt = jnp.stack(mvs)
    Bp = x_prompt.shape[0]
    C0 = jnp.zeros((N_A_LAYERS, Bp, A_HEADS, A_DV, A_DQK), jnp.float32)
    n0 = jnp.zeros((N_A_LAYERS, Bp, A_HEADS, A_DQK), jnp.float32)
    m0 = jnp.zeros((N_A_LAYERS, Bp, A_HEADS), jnp.float32)
    y_prompt, C_p, n_p, m_p, k_p, v_p = trunk(x_prompt, C0, n0, m0, None, None, mem_k_prompt, mem_v_prompt, p)
    y_sample, C_s, n_s, m_s, k_s, v_s = trunk(x_sample, state_mlstm_C, state_mlstm_n, state_mlstm_m,
                                              cache_sb_k, cache_sb_v, cache_mem_k, cache_mem_v, p)
    return (y_prompt, y_sample, C_p, n_p, m_p, k_p, v_p, mem_k_prompt, mem_v_prompt, C_s, n_s, m_s, k_s, v_s)
```

```python
import functools

import jax
import jax.numpy as jnp
from jax import lax
from jax.experimental import pallas as pl
from jax.experimental.pallas import tpu as pltpu

F32 = jnp.float32
BF16 = jnp.bfloat16

RMS_EPS = 1e-6
A_HEADS = 6
A_DQK = 128
A_DV = 256
SB_HEADS = 12
SB_HD = 128
MEM_HEADS = 4
MEM_HD = 128
LANES = 128
GATE_ROWS = 16

ROW_TILE = 768
FF_TILE = 512
VMEM_LIMIT = 56 * 1024 * 1024

NT_DIMS = (((1,), (1,)), ((), ()))
TN_DIMS = (((0,), (0,)), ((), ()))
EXP_ZERO_BELOW = -104.0


def _params(n_axes):
    return pltpu.CompilerParams(
        dimension_semantics=("arbitrary",) * n_axes, vmem_limit_bytes=VMEM_LIMIT)


def _rms(x, g):
    ms = jnp.mean(x * x, axis=-1, keepdims=True)
    return x * lax.rsqrt(ms + RMS_EPS) * g


def _log_sigmoid(x):
    return jnp.minimum(x, 0.0) - jnp.log1p(jnp.exp(-jnp.abs(x)))


def _dot(a, b):
    return jnp.dot(a, b, preferred_element_type=F32)


def _dot_nt(a, b):
    return lax.dot_general(a, b, NT_DIMS, preferred_element_type=F32)


def _split3(x):
    x1 = x.astype(BF16)
    r = x - x1.astype(F32)
    x2 = r.astype(BF16)
    x3 = (r - x2.astype(F32)).astype(BF16)
    return x1, x2, x3


def _ffn_kernel(x_ref, g_ref, wg_ref, wu_ref, wd_ref, o_ref, h_ref):
    j = pl.program_id(1)

    @pl.when(j == 0)
    def _():
        x = x_ref[...]
        h_ref[...] = _rms(x, g_ref[...]).astype(BF16)
        o_ref[...] = x

    h = h_ref[...]
    g = _dot(h, wg_ref[...])
    u = _dot(h, wu_ref[...])
    a = (g * jax.nn.sigmoid(g) * u * 0.5).astype(BF16)
    o_ref[...] += _dot(a, wd_ref[...])


def _ffn(x, gain, wg, wu, wd, layer):
    m, d = x.shape
    f = wg.shape[-1]
    return pl.pallas_call(
        _ffn_kernel,
        grid=(m // ROW_TILE, f // FF_TILE),
        in_specs=[
            pl.BlockSpec((ROW_TILE, d), lambda i, j: (i, 0)),
            pl.BlockSpec((None, 1, d), lambda i, j: (layer, 0, 0)),
            pl.BlockSpec((None, d, FF_TILE), lambda i, j: (layer, 0, j)),
            pl.BlockSpec((None, d, FF_TILE), lambda i, j: (layer, 0, j)),
            pl.BlockSpec((None, FF_TILE, d), lambda i, j: (layer, j, 0)),
        ],
        out_specs=pl.BlockSpec((ROW_TILE, d), lambda i, j: (i, 0)),
        out_shape=jax.ShapeDtypeStruct((m, d), F32),
        scratch_shapes=[pltpu.VMEM((ROW_TILE, d), BF16)],
        compiler_params=_params(2),
        name="ffn",
    )(x, gain, wg, wu, wd)


def _proj_kernel(x_ref, g_ref, w_ref, o_ref, h_ref):
    @pl.when(pl.program_id(1) == 0)
    def _():
        h_ref[...] = _rms(x_ref[...], g_ref[...]).astype(BF16)

    o_ref[...] = _dot(h_ref[...], w_ref[...]).astype(o_ref.dtype)


def _proj_gates_kernel(x_ref, g_ref, w_ref, wgt_ref, o_ref, gt_ref, h_ref):
    @pl.when(pl.program_id(1) == 0)
    def _():
        h = _rms(x_ref[...], g_ref[...]).astype(BF16)
        h_ref[...] = h
        gt_ref[...] = _dot_nt(wgt_ref[...], h)

    o_ref[...] = _dot(h_ref[...], w_ref[...]).astype(o_ref.dtype)


def _proj(x, gain, gain_layer, w, layer, col_tile, wgt=None):
    m, d = x.shape
    n = w.shape[-1]
    in_specs = [
        pl.BlockSpec((ROW_TILE, d), lambda i, j: (i, 0)),
        pl.BlockSpec((None, 1, d), lambda i, j: (gain_layer, 0, 0)),
        pl.BlockSpec((None, d, col_tile), lambda i, j: (layer, 0, j)),
    ]
    out_spec = pl.BlockSpec((ROW_TILE, col_tile), lambda i, j: (i, j))
    out_shape = jax.ShapeDtypeStruct((m, n), F32)
    common = dict(
        grid=(m // ROW_TILE, n // col_tile),
        scratch_shapes=[pltpu.VMEM((ROW_TILE, d), BF16)],
        compiler_params=_params(2),
    )
    if wgt is None:
        return pl.pallas_call(
            _proj_kernel, in_specs=in_specs, out_specs=out_spec, out_shape=out_shape,
            name="proj", **common)(x, gain, w)
    return pl.pallas_call(
        _proj_gates_kernel,
        in_specs=in_specs + [pl.BlockSpec((None, GATE_ROWS, d), lambda i, j: (layer, 0, 0))],
        out_specs=(out_spec, pl.BlockSpec((GATE_ROWS, ROW_TILE), lambda i, j: (0, i))),
        out_shape=(out_shape, jax.ShapeDtypeStruct((GATE_ROWS, m), F32)),
        name="proj_gates", **common)(x, gain, w, wgt)


def _kv_kernel(x_ref, g_ref, w_ref, o32_ref, o16_ref, h_ref):
    @pl.when(pl.program_id(1) == 0)
    def _():
        h_ref[...] = _rms(x_ref[...], g_ref[...]).astype(BF16)

    y = _dot(h_ref[...], w_ref[...])
    o32_ref[...] = y
    o16_ref[...] = y.astype(BF16)


def _kv_proj(x, gain, w):
    m, d = x.shape
    n = w.shape[-1]
    out_spec = pl.BlockSpec((None, ROW_TILE, n), lambda i, j: (j, i, 0))
    return pl.pallas_call(
        _kv_kernel,
        grid=(m // ROW_TILE, 2),
        in_specs=[
            pl.BlockSpec((ROW_TILE, d), lambda i, j: (i, 0)),
            pl.BlockSpec((1, d), lambda i, j: (0, 0)),
            pl.BlockSpec((None, d, n), lambda i, j: (j, 0, 0)),
        ],
        out_specs=(out_spec, out_spec),
        out_shape=(jax.ShapeDtypeStruct((2, m, n), F32), jax.ShapeDtypeStruct((2, m, n), BF16)),
        scratch_shapes=[pltpu.VMEM((ROW_TILE, d), BF16)],
        compiler_params=_params(2),
        name="kv_proj",
    )(x, gain, w)


def _out_proj_kernel(x_ref, tok_ref, mo_ref, wt_ref, wm_ref, o_ref):
    o_ref[...] = x_ref[...] + _dot(tok_ref[...], wt_ref[...]) + _dot(mo_ref[...], wm_ref[...])


def _out_proj(x, tok, mo, w, layer):
    m, d = x.shape
    kt = tok.shape[1]
    km = mo.shape[1]
    return pl.pallas_call(
        _out_proj_kernel,
        grid=(m // ROW_TILE,),
        in_specs=[
            pl.BlockSpec((ROW_TILE, d), lambda i: (i, 0)),
            pl.BlockSpec((ROW_TILE, kt), lambda i: (i, 0)),
            pl.BlockSpec((ROW_TILE, km), lambda i: (i, 0)),
            pl.BlockSpec((None, kt, d), lambda i: (layer, 0, 0)),
            pl.BlockSpec((None, km, d), lambda i: (layer, kt // km, 0)),
        ],
        out_specs=pl.BlockSpec((ROW_TILE, d), lambda i: (i, 0)),
        out_shape=jax.ShapeDtypeStruct((m, d), F32),
        compiler_params=_params(1),
        name="out_proj",
    )(x, tok, mo, w, w)


def _mem_kv_kernel(mem_ref, g_ref, wk_ref, wv_ref, kg_ref, mk_ref, mv_ref):
    h = _rms(mem_ref[...], g_ref[...]).astype(BF16)
    k = _dot(h, wk_ref[...].astype(BF16))
    kg = kg_ref[...]
    for hd in range(MEM_HEADS):
        sl = slice(hd * MEM_HD, (hd + 1) * MEM_HD)
        mk_ref[:, sl] = _rms(k[:, sl], kg)
    mv_ref[...] = _dot(h, wv_ref[...].astype(BF16))


def _mem_kv(mem, gain, wk, wv, k_gain):
    depth, d, w = wk.shape
    n = mem.shape[0]
    out_spec = pl.BlockSpec((None, n, w), lambda l: (l, 0, 0))
    out_shape = jax.ShapeDtypeStruct((depth, n, w), F32)
    return pl.pallas_call(
        _mem_kv_kernel,
        grid=(depth,),
        in_specs=[
            pl.BlockSpec((n, d), lambda l: (0, 0)),
            pl.BlockSpec((None, 1, d), lambda l: (l, 0, 0)),
            pl.BlockSpec((None, d, w), lambda l: (l, 0, 0)),
            pl.BlockSpec((None, d, w), lambda l: (l, 0, 0)),
            pl.BlockSpec((None, 1, MEM_HD), lambda l: (l, 0, 0)),
        ],
        out_specs=(out_spec, out_spec),
        out_shape=(out_shape, out_shape),
        compiler_params=_params(1),
        name="mem_kv",
    )(mem, gain, wk, wv, k_gain)


def _mem_attn_kernel(mq_ref, qg_ref, mk_ref, mv_ref, o_ref):
    qg = qg_ref[...]
    scale = MEM_HD ** -0.5
    for hd in range(MEM_HEADS):
        sl = slice(hd * MEM_HD, (hd + 1) * MEM_HD)
        qn = _rms(mq_ref[:, sl], qg).astype(BF16)
        s = _dot_nt(qn, mk_ref[:, sl].astype(BF16)) * scale
        e = jnp.exp(s - jnp.max(s, axis=-1, keepdims=True))
        p = e / jnp.sum(e, axis=-1, keepdims=True)
        o_ref[:, sl] = _dot(p.astype(BF16), mv_ref[:, sl].astype(BF16)).astype(BF16)


def _mem_attn(proj, col_block, row_block0, rows, n_seq, q_gain, layer, mk, mv, kv_index0, kv_stride):
    w = MEM_HEADS * MEM_HD
    slots = mk.shape[1]
    return pl.pallas_call(
        _mem_attn_kernel,
        grid=(n_seq,),
        in_specs=[
            pl.BlockSpec((rows, w), lambda b: (row_block0 + b, col_block)),
            pl.BlockSpec((None, 1, MEM_HD), lambda b: (layer, 0, 0)),
            pl.BlockSpec((None, slots, w), lambda b: (kv_index0 + b * kv_stride, 0, 0)),
            pl.BlockSpec((None, slots, w), lambda b: (kv_index0 + b * kv_stride, 0, 0)),
        ],
        out_specs=pl.BlockSpec((rows, w), lambda b: (b, 0)),
        out_shape=jax.ShapeDtypeStruct((n_seq * rows, w), BF16),
        compiler_params=_params(1),
        name="mem_attn",
    )(proj, q_gain, mk, mv)


def _mlstm_kernel(q_ref, k_ref, v_ref, og_ref, gc_ref, gr_ref, bc_ref, br_ref, hn_ref,
                  c0_ref, n0_ref, m0_ref, tok_ref, c_ref, n_ref, m_ref, *, chunk):
    L = chunk

    @pl.when(pl.program_id(1) == 0)
    def _():
        c_ref[...] = c0_ref[...]
        n_ref[...] = n0_ref[...]
        m_ref[...] = m0_ref[...]

    row = lax.broadcasted_iota(jnp.int32, (L, L), 0)
    col = lax.broadcasted_iota(jnp.int32, (L, L), 1)
    causal = col <= row
    tri = jnp.where(causal, 1.0, 0.0).astype(BF16)

    pre_c = gc_ref[...] + bc_ref[...]
    pre_r = gr_ref[...] + br_ref[...]
    lf_c = _log_sigmoid(pre_c)
    lf_r = _log_sigmoid(pre_r)
    cum_c = sum(_dot(tri, part) for part in _split3(lf_c))
    cum_r = sum(_dot_nt(part, tri) for part in _split3(lf_r))

    scale = A_DQK ** -0.5
    for h in range(A_HEADS):
        qk = slice(h * A_DQK, (h + 1) * A_DQK)
        vv = slice(h * A_DV, (h + 1) * A_DV)
        q = q_ref[:, qk] * scale
        q16 = q.astype(BF16)
        k = k_ref[:, qk]
        k16 = k.astype(BF16)
        v = v_ref[:, vv]
        i_c = pre_c[:, h:h + 1]
        b_c = cum_c[:, A_HEADS + h:A_HEADS + h + 1]
        i_r = pre_r[h:h + 1, :]
        b_r = cum_r[A_HEADS + h:A_HEADS + h + 1, :]
        c_old = c_ref[h]
        n_old = n_ref[h:h + 1, :]
        m_old = m_ref[h:h + 1, 0:1]

        d = jnp.where(causal, b_c - b_r + i_r, -jnp.inf)
        inter = b_c + m_old
        m_t = jnp.maximum(inter, jnp.max(d, axis=-1, keepdims=True))
        w_in = jnp.exp(d - m_t)
        w_st = jnp.exp(inter - m_t)
        s = _dot_nt(q16, k16) * w_in
        num = _dot(s.astype(BF16), v.astype(BF16)) + w_st * _dot_nt(q16, c_old.astype(BF16))
        den = (jnp.sum(s, axis=-1, keepdims=True)
               + w_st * jnp.sum(q * n_old, axis=-1, keepdims=True))
        hh = num * (1.0 / jnp.maximum(jnp.abs(den), jnp.exp(-m_t)))
        out = _rms(hh, hn_ref[:, vv]) * jax.nn.sigmoid(og_ref[:, vv])
        tok_ref[:, vv] = out.astype(BF16)

        b_end = b_c[L - 1:L, :]
        g = b_end - b_c + i_c
        m_new = jnp.maximum(b_end + m_old, jnp.max(g, axis=0, keepdims=True))
        w_k = jnp.exp(g - m_new)
        decay = jnp.exp(b_end + m_old - m_new)
        vw = (v * w_k).astype(BF16)
        c_ref[h] = decay * c_old + lax.dot_general(vw, k16, TN_DIMS, preferred_element_type=F32)
        n_ref[h:h + 1, :] = decay * n_old + jnp.sum(w_k * k, axis=0, keepdims=True)
        m_ref[h:h + 1, :] = jnp.broadcast_to(m_new, (1, LANES))


def _mlstm(proj, gates_r, bias_c, bias_r, head_norm, c0, n0, m0, row_block0, chunk, n_chunks):
    n_seq = c0.shape[0]
    aq = A_HEADS * A_DQK
    av = A_HEADS * A_DV
    gate_block = (2 * aq + 2 * av + MEM_HEADS * MEM_HD) // LANES

    def rows(b, c):
        return row_block0 + b * n_chunks + c

    state = lambda b, c: (b, 0, 0)
    return pl.pallas_call(
        functools.partial(_mlstm_kernel, chunk=chunk),
        grid=(n_seq, n_chunks),
        in_specs=[
            pl.BlockSpec((chunk, aq), lambda b, c: (rows(b, c), 0)),
            pl.BlockSpec((chunk, aq), lambda b, c: (rows(b, c), 1)),
            pl.BlockSpec((chunk, av), lambda b, c: (rows(b, c), 1)),
            pl.BlockSpec((chunk, av), lambda b, c: (rows(b, c), 2)),
            pl.BlockSpec((chunk, LANES), lambda b, c: (rows(b, c), gate_block)),
            pl.BlockSpec((None, GATE_ROWS, chunk), lambda b, c: (b, 0, c)),
            pl.BlockSpec((1, LANES), lambda b, c: (0, 0)),
            pl.BlockSpec((GATE_ROWS, 1), lambda b, c: (0, 0)),
            pl.BlockSpec((1, av), lambda b, c: (0, 0)),
            pl.BlockSpec((None, A_HEADS, A_DV, A_DQK), lambda b, c: (b, 0, 0, 0)),
            pl.BlockSpec((None, 8, A_DQK), state),
            pl.BlockSpec((None, 8, LANES), state),
        ],
        out_specs=(
            pl.BlockSpec((chunk, av), lambda b, c: (b * n_chunks + c, 0)),
            pl.BlockSpec((None, A_HEADS, A_DV, A_DQK), lambda b, c: (b, 0, 0, 0)),
            pl.BlockSpec((None, 8, A_DQK), state),
            pl.BlockSpec((None, 8, LANES), state),
        ),
        out_shape=(
            jax.ShapeDtypeStruct((n_seq * n_chunks * chunk, av), BF16),
            jax.ShapeDtypeStruct((n_seq, A_HEADS, A_DV, A_DQK), F32),
            jax.ShapeDtypeStruct((n_seq, 8, A_DQK), F32),
            jax.ShapeDtypeStruct((n_seq, 8, LANES), F32),
        ),
        compiler_params=_params(2),
        name="mlstm",
    )(proj, proj, proj, proj, proj, gates_r, bias_c, bias_r, head_norm, c0, n0, m0)


def _sb_block(q16, k16, v16, upper, valid, r_prev):
    z = _dot_nt(q16, k16) * (SB_HD ** -0.5)
    u = _log_sigmoid(-z)
    if valid is not None:
        u = jnp.where(valid, u, 0.0)
    u1, u2, u3 = _split3(u)
    rest = _dot(u1, upper) + _dot(u2, upper) + _dot(u3, upper) + r_prev
    a = jnp.exp(z + u + rest)
    if valid is not None:
        a = jnp.where(valid, a, 0.0)
    return _dot(a.astype(BF16), v16), r_prev + jnp.sum(u, axis=-1, keepdims=True)


def _upper(n):
    row = lax.broadcasted_iota(jnp.int32, (n, n), 0)
    col = lax.broadcasted_iota(jnp.int32, (n, n), 1)
    return jnp.where(row > col, 1.0, 0.0).astype(BF16)


def _sb_prompt_kernel(q_ref, k_ref, v_ref, o_ref, acc_ref, r_ref, *, tile):
    qi = pl.program_id(1)
    q16 = q_ref[...].astype(BF16)
    upper = _upper(tile)
    row = lax.broadcasted_iota(jnp.int32, (tile, tile), 0)
    col = lax.broadcasted_iota(jnp.int32, (tile, tile), 1)

    start = pl.multiple_of(qi * tile, tile)
    out, r = _sb_block(q16, k_ref[pl.ds(start, tile), :], v_ref[pl.ds(start, tile), :],
                       upper, col < row, jnp.zeros((tile, 1), F32))
    acc_ref[...] = out
    r_ref[...] = r

    def cond(carry):
        kb, r_max = carry
        return jnp.logical_and(kb >= 0, r_max > EXP_ZERO_BELOW)

    def body(carry):
        kb, _ = carry
        s = pl.multiple_of(kb * tile, tile)
        out, r = _sb_block(q16, k_ref[pl.ds(s, tile), :], v_ref[pl.ds(s, tile), :],
                           upper, None, r_ref[...])
        acc_ref[...] += out
        r_ref[...] = r
        return kb - 1, jnp.max(r)

    lax.while_loop(cond, body, (qi - 1, jnp.max(r)))
    o_ref[...] = acc_ref[...].astype(BF16)


def _sb_prompt(proj, kv16, seq, tile=256):
    return pl.pallas_call(
        functools.partial(_sb_prompt_kernel, tile=tile),
        grid=(SB_HEADS, seq // tile),
        in_specs=[
            pl.BlockSpec((tile, SB_HD), lambda h, i: (i, h)),
            pl.BlockSpec((None, seq, SB_HD), lambda h, i: (0, 0, h)),
            pl.BlockSpec((None, seq, SB_HD), lambda h, i: (1, 0, h)),
        ],
        out_specs=pl.BlockSpec((tile, SB_HD), lambda h, i: (i, h)),
        out_shape=jax.ShapeDtypeStruct((seq, SB_HEADS * SB_HD), BF16),
        scratch_shapes=[pltpu.VMEM((tile, SB_HD), F32), pltpu.VMEM((tile, 1), F32)],
        compiler_params=_params(2),
        name="sb_prompt",
    )(proj, kv16, kv16)


def _sb_sample_kernel(q_ref, kn_ref, vn_ref, kp_ref, vp_ref, o_ref, acc_ref, r_ref, live_ref,
                      *, rows, tile):
    j = pl.program_id(1)

    @pl.when(j == 0)
    def _():
        row = lax.broadcasted_iota(jnp.int32, (rows, rows), 0)
        col = lax.broadcasted_iota(jnp.int32, (rows, rows), 1)
        upper = _upper(rows)
        r_max = jnp.float32(-jnp.inf)
        for h in range(SB_HEADS):
            sl = slice(h * SB_HD, (h + 1) * SB_HD)
            out, r = _sb_block(q_ref[:, sl].astype(BF16), kn_ref[:, sl], vn_ref[:, sl],
                               upper, col < row, jnp.zeros((rows, 1), F32))
            acc_ref[:, sl] = out
            r_ref[h] = r
            r_max = jnp.maximum(r_max, jnp.max(r))
        live_ref[0] = jnp.where(r_max > EXP_ZERO_BELOW, 1, 0).astype(jnp.int32)

    @pl.when(jnp.logical_and(j > 0, live_ref[0] == 1))
    def _():
        upper = _upper(tile)
        r_max = jnp.float32(-jnp.inf)
        for h in range(SB_HEADS):
            sl = slice(h * SB_HD, (h + 1) * SB_HD)
            out, r = _sb_block(q_ref[:, sl].astype(BF16), kp_ref[:, sl].astype(BF16),
                               vp_ref[:, sl].astype(BF16), upper, None, r_ref[h])
            acc_ref[:, sl] += out
            r_ref[h] = r
            r_max = jnp.maximum(r_max, jnp.max(r))
        live_ref[0] = jnp.where(r_max > EXP_ZERO_BELOW, 1, 0).astype(jnp.int32)

    @pl.when(j == pl.num_programs(1) - 1)
    def _():
        o_ref[...] = acc_ref[...].astype(BF16)


def _sb_sample(proj, kv16, past_k, past_v, row_block0, rows, tile=512):
    n_seq, past, w = past_k.shape
    n_past = past // tile

    def past_map(b, j):
        return (b, n_past - jnp.maximum(j, 1), 0)

    return pl.pallas_call(
        functools.partial(_sb_sample_kernel, rows=rows, tile=tile),
        grid=(n_seq, n_past + 1),
        in_specs=[
            pl.BlockSpec((rows, w), lambda b, j: (row_block0 + b, 0)),
            pl.BlockSpec((None, rows, w), lambda b, j: (0, row_block0 + b, 0)),
            pl.BlockSpec((None, rows, w), lambda b, j: (1, row_block0 + b, 0)),
            pl.BlockSpec((None, tile, w), past_map),
            pl.BlockSpec((None, tile, w), past_map),
        ],
        out_specs=pl.BlockSpec((rows, w), lambda b, j: (b, 0)),
        out_shape=jax.ShapeDtypeStruct((n_seq * rows, w), BF16),
        scratch_shapes=[
            pltpu.VMEM((rows, w), F32),
            pltpu.VMEM((SB_HEADS, rows, 1), F32),
            pltpu.SMEM((1,), jnp.int32),
        ],
        compiler_params=_params(2),
        name="sb_sample",
    )(proj, kv16, kv16, past_k, past_v)


def _pad_rows(a, n):
    return jnp.pad(a, [(0, 0)] * (a.ndim - 2) + [(0, n - a.shape[-2]), (0, 0)])


def kernel(x_prompt, x_sample, state_mlstm_C, state_mlstm_n, state_mlstm_m, cache_sb_k, cache_sb_v,
           cache_mem_k, cache_mem_v, mem_prompt, ffn1_norm, ffn1_w_gate, ffn1_w_up, ffn1_w_down,
           ffn2_norm, ffn2_w_gate, ffn2_w_up, ffn2_w_down, mix_norm, a_w_in, a_b_i, a_b_f, a_head_norm,
           b_w_in, w_out, mem_norm, mem_w_k, mem_w_v, mem_q_norm, mem_k_norm, kv_norm, sb_w_k, sb_w_v):
    n_pb, seq, d = x_prompt.shape
    n_sb, dec_seq, _ = x_sample.shape
    assert n_pb == 1
    depth = ffn1_norm.shape[0]
    n_a = a_w_in.shape[0]
    n_slots = mem_prompt.shape[1]
    aq = A_HEADS * A_DQK
    av = A_HEADS * A_DV
    mem_w = MEM_HEADS * MEM_HD
    sb_w = SB_HEADS * SB_HD
    n_sample = n_sb * dec_seq
    sample_block0 = seq // dec_seq

    gains = lambda g: g.reshape(g.shape[0], 1, g.shape[-1])
    ffn1 = (gains(ffn1_norm), ffn1_w_gate.astype(BF16), ffn1_w_up.astype(BF16), ffn1_w_down.astype(BF16))
    ffn2 = (gains(ffn2_norm), ffn2_w_gate.astype(BF16), ffn2_w_up.astype(BF16), ffn2_w_down.astype(BF16))
    mix_gain = gains(mix_norm)
    n_main = 2 * aq + 2 * av
    w_gates = a_w_in[:, :, n_main:n_main + 2 * A_HEADS]
    a_cols = n_main + mem_w + LANES
    a_tile = 768
    a_pad = -a_cols % a_tile
    a_w = jnp.concatenate(
        [a_w_in[:, :, :n_main], a_w_in[:, :, n_main + 2 * A_HEADS:], w_gates,
         jnp.zeros((n_a, d, LANES - 2 * A_HEADS + a_pad), F32)], axis=-1).astype(BF16)
    a_wgt = _pad_rows(jnp.swapaxes(w_gates, 1, 2), GATE_ROWS).astype(BF16)
    gate_bias = jnp.concatenate([a_b_i, a_b_f], axis=-1)
    b_w = b_w_in.astype(BF16)
    w_o = w_out.astype(BF16)
    w_kv = jnp.stack([sb_w_k, sb_w_v]).astype(BF16)
    q_gain = mem_q_norm.reshape(depth, 1, MEM_HD)

    mk_p, mv_p = _mem_kv(mem_prompt[0], gains(mem_norm), mem_w_k, mem_w_v,
                         mem_k_norm.reshape(depth, 1, MEM_HD))
    mk_s = cache_mem_k.reshape(depth * n_sb, n_slots, mem_w)
    mv_s = cache_mem_v.reshape(depth * n_sb, n_slots, mem_w)
    past_k = cache_sb_k.reshape(n_sb, -1, sb_w)
    past_v = cache_sb_v.reshape(n_sb, -1, sb_w)

    x = jnp.concatenate([x_prompt[0], x_sample.reshape(n_sample, d)], axis=0)
    c_p, n_p, m_p, c_s, n_s, m_s = [], [], [], [], [], []
    kv32 = kv16 = None
    mem_tile = 512
    for l in range(depth):
        x = _ffn(x, *ffn1, l)
        if l < n_a:
            proj, gates_r = _proj(x, mix_gain, l, a_w, l, a_tile, wgt=a_wgt)
            bias_c = jnp.pad(gate_bias[l], (0, LANES - 2 * A_HEADS)).reshape(1, LANES)
            bias_r = jnp.pad(gate_bias[l], (0, GATE_ROWS - 2 * A_HEADS)).reshape(GATE_ROWS, 1)
            head_norm = a_head_norm[l].reshape(1, av)
            chunk_p = 128
            tok_p, c, n, m = _mlstm(
                proj, gates_r[:, :seq].reshape(1, GATE_ROWS, seq), bias_c, bias_r, head_norm,
                jnp.zeros((1, A_HEADS, A_DV, A_DQK), F32), jnp.zeros((1, 8, A_DQK), F32),
                jnp.zeros((1, 8, LANES), F32), 0, chunk_p, seq // chunk_p)
            c_p.append(c); n_p.append(n[:, :A_HEADS]); m_p.append(m[:, :A_HEADS, 0])
            gates_s = gates_r[:, seq:].reshape(GATE_ROWS, n_sb, dec_seq).transpose(1, 0, 2)
            m0 = jnp.broadcast_to(state_mlstm_m[l][:, :, None], (n_sb, A_HEADS, LANES))
            tok_s, c, n, m = _mlstm(
                proj, gates_s, bias_c, bias_r, head_norm, state_mlstm_C[l],
                _pad_rows(state_mlstm_n[l], 8), _pad_rows(m0, 8), sample_block0, dec_seq, 1)
            c_s.append(c); n_s.append(n[:, :A_HEADS]); m_s.append(m[:, :A_HEADS, 0])
            mq_block = n_main // mem_w
        else:
            proj = _proj(x, mix_gain, l, b_w, l - n_a, 1024)
            tok_p = _sb_prompt(proj, kv16, seq)
            tok_s = _sb_sample(proj, kv16, past_k, past_v, sample_block0, dec_seq)
            mq_block = sb_w // mem_w
        mo_p = _mem_attn(proj, mq_block, 0, mem_tile, seq // mem_tile, q_gain, l, mk_p, mv_p, l, 0)
        mo_s = _mem_attn(proj, mq_block, sample_block0, dec_seq, n_sb, q_gain, l, mk_s, mv_s, l * n_sb, 1)
        x = _out_proj(x, jnp.concatenate([tok_p, tok_s]), jnp.concatenate([mo_p, mo_s]), w_o, l)
        x = _ffn(x, *ffn2, l)
        if l == n_a - 1:
            kv32, kv16 = _kv_proj(x, kv_norm.reshape(1, d), w_kv)

    y_prompt = x[:seq].reshape(1, seq, d)
    y_sample = x[seq:].reshape(n_sb, dec_seq, d)
    k_p = kv32[0, :seq].reshape(1, seq, SB_HEADS, SB_HD)
    v_p = kv32[1, :seq].reshape(1, seq, SB_HEADS, SB_HD)
    k_s = kv32[0, seq:].reshape(n_sb, dec_seq, SB_HEADS, SB_HD)
    v_s = kv32[1, seq:].reshape(n_sb, dec_seq, SB_HEADS, SB_HD)
    mem_shape = (depth, 1, n_slots, MEM_HEADS, MEM_HD)
    return (y_prompt, y_sample, jnp.stack(c_p), jnp.stack(n_p), jnp.stack(m_p), k_p, v_p,
            mk_p.reshape(mem_shape), mv_p.reshape(mem_shape),
            jnp.stack(c_s), jnp.stack(n_s), jnp.stack(m_s), k_s, v_s)
```

```python
import functools

import jax
import jax.numpy as jnp
from jax import lax
from jax.experimental import pallas as pl
from jax.experimental.pallas import tpu as pltpu

F32 = jnp.float32
BF16 = jnp.bfloat16

RMS_EPS = 1e-6
A_HEADS = 6
A_DQK = 128
A_DV = 256
SB_HEADS = 12
SB_HD = 128
MEM_HEADS = 4
MEM_HD = 128
LANES = 128
SUBLANES = 8
GATE_ROWS = 16

ROW_TILE = 768
FF_TILE = 512
FF_HEAD_TILE = 256
VMEM_LIMIT = 56 * 1024 * 1024

SB_TILE = 256
SB_CHAINS = 4
SB_PAD = (SB_CHAINS - 1) * SB_TILE

NT_DIMS = (((1,), (1,)), ((), ()))
TN_DIMS = (((0,), (0,)), ((), ()))
EXP_ZERO_BELOW = -104.0


def _params(n_axes):
    return pltpu.CompilerParams(
        dimension_semantics=("arbitrary",) * n_axes, vmem_limit_bytes=VMEM_LIMIT)


def _rms(x, g):
    ms = jnp.mean(x * x, axis=-1, keepdims=True)
    return x * lax.rsqrt(ms + RMS_EPS) * g


def _log_sigmoid(x):
    return jnp.minimum(x, 0.0) - jnp.log1p(jnp.exp(-jnp.abs(x)))


def _dot(a, b):
    return jnp.dot(a, b, preferred_element_type=F32)


def _dot_nt(a, b):
    return lax.dot_general(a, b, NT_DIMS, preferred_element_type=F32)


def _split3(x):
    x1 = x.astype(BF16)
    r = x - x1.astype(F32)
    x2 = r.astype(BF16)
    x3 = (r - x2.astype(F32)).astype(BF16)
    return x1, x2, x3


def _ffn_step(h_ref, wg, wu, wd, o_ref):
    h = h_ref[...]
    g = _dot(h, wg)
    u = _dot(h, wu)
    a = (g * jax.nn.sigmoid(g) * u * 0.5).astype(BF16)
    o_ref[...] += _dot(a, wd)


def _ffn_head_kernel(x_ref, g_ref, wg_ref, wu_ref, wd_ref, o_ref, wg16_ref, wu16_ref, wd16_ref, h_ref):
    @pl.when(pl.program_id(0) == 0)
    def _():
        x = x_ref[...]
        h_ref[...] = _rms(x, g_ref[...]).astype(BF16)
        o_ref[...] = x

    wg16_ref[...] = wg_ref[...].astype(BF16)
    wu16_ref[...] = wu_ref[...].astype(BF16)
    wd16_ref[...] = wd_ref[...].astype(BF16)
    _ffn_step(h_ref, wg16_ref[...], wu16_ref[...], wd16_ref[...], o_ref)


def _ffn_rest_kernel(x_ref, head_ref, g_ref, wg_ref, wu_ref, wd_ref, o_ref, h_ref):
    i = pl.program_id(0)
    j = pl.program_id(1)

    @pl.when(jnp.logical_and(i == 0, j == 0))
    def _():
        o_ref[...] = head_ref[...]

    @pl.when(jnp.logical_and(i > 0, j == 0))
    def _():
        x = x_ref[...]
        h_ref[...] = _rms(x, g_ref[...]).astype(BF16)
        o_ref[...] = x

    @pl.when(i > 0)
    def _():
        _ffn_step(h_ref, wg_ref[...], wu_ref[...], wd_ref[...], o_ref)


def _ffn(x, gain, wg, wu, wd, layer):
    m, d = x.shape
    f = wg.shape[-1]
    n_head = f // FF_HEAD_TILE
    w16 = lambda shape: jax.ShapeDtypeStruct(shape, BF16)
    head, wg16, wu16, wd16 = pl.pallas_call(
        _ffn_head_kernel,
        grid=(n_head,),
        in_specs=[
            pl.BlockSpec((ROW_TILE, d), lambda j: (0, 0)),
            pl.BlockSpec((None, 1, d), lambda j: (layer, 0, 0)),
            pl.BlockSpec((None, d, FF_HEAD_TILE), lambda j: (layer, 0, j)),
            pl.BlockSpec((None, d, FF_HEAD_TILE), lambda j: (layer, 0, j)),
            pl.BlockSpec((None, FF_HEAD_TILE, d), lambda j: (layer, j, 0)),
        ],
        out_specs=(
            pl.BlockSpec((ROW_TILE, d), lambda j: (0, 0)),
            pl.BlockSpec((d, FF_HEAD_TILE), lambda j: (0, j)),
            pl.BlockSpec((d, FF_HEAD_TILE), lambda j: (0, j)),
            pl.BlockSpec((FF_HEAD_TILE, d), lambda j: (j, 0)),
        ),
        out_shape=(jax.ShapeDtypeStruct((ROW_TILE, d), F32), w16((d, f)), w16((d, f)), w16((f, d))),
        scratch_shapes=[pltpu.VMEM((ROW_TILE, d), BF16)],
        compiler_params=_params(1),
        name="ffn_head",
    )(x, gain, wg, wu, wd)
    col = lambda i, j: jnp.where(i == 0, 0, j)
    return pl.pallas_call(
        _ffn_rest_kernel,
        grid=(m // ROW_TILE, f // FF_TILE),
        in_specs=[
            pl.BlockSpec((ROW_TILE, d), lambda i, j: (jnp.maximum(i, 1), 0)),
            pl.BlockSpec((ROW_TILE, d), lambda i, j: (0, 0)),
            pl.BlockSpec((None, 1, d), lambda i, j: (layer, 0, 0)),
            pl.BlockSpec((d, FF_TILE), lambda i, j: (0, col(i, j))),
            pl.BlockSpec((d, FF_TILE), lambda i, j: (0, col(i, j))),
            pl.BlockSpec((FF_TILE, d), lambda i, j: (col(i, j), 0)),
        ],
        out_specs=pl.BlockSpec((ROW_TILE, d), lambda i, j: (i, 0)),
        out_shape=jax.ShapeDtypeStruct((m, d), F32),
        scratch_shapes=[pltpu.VMEM((ROW_TILE, d), BF16)],
        compiler_params=_params(2),
        name="ffn",
    )(x, head, gain, wg16, wu16, wd16)


def _proj_kernel(x_ref, g_ref, w_ref, o_ref, h_ref):
    @pl.when(pl.program_id(1) == 0)
    def _():
        h_ref[...] = _rms(x_ref[...], g_ref[...]).astype(BF16)

    o_ref[...] = _dot(h_ref[...], w_ref[...]).astype(o_ref.dtype)


def _proj_gates_kernel(x_ref, g_ref, w_ref, wgt_ref, o_ref, gt_ref, h_ref):
    @pl.when(pl.program_id(1) == 0)
    def _():
        h = _rms(x_ref[...], g_ref[...]).astype(BF16)
        h_ref[...] = h
        gt_ref[...] = _dot_nt(wgt_ref[...], h)

    o_ref[...] = _dot(h_ref[...], w_ref[...]).astype(o_ref.dtype)


def _proj(x, gain, gain_layer, w, layer, col_tile, wgt=None):
    m, d = x.shape
    n = w.shape[-1]
    in_specs = [
        pl.BlockSpec((ROW_TILE, d), lambda i, j: (i, 0)),
        pl.BlockSpec((None, 1, d), lambda i, j: (gain_layer, 0, 0)),
        pl.BlockSpec((None, d, col_tile), lambda i, j: (layer, 0, j)),
    ]
    out_spec = pl.BlockSpec((ROW_TILE, col_tile), lambda i, j: (i, j))
    out_shape = jax.ShapeDtypeStruct((m, n), F32)
    common = dict(
        grid=(m // ROW_TILE, n // col_tile),
        scratch_shapes=[pltpu.VMEM((ROW_TILE, d), BF16)],
        compiler_params=_params(2),
    )
    if wgt is None:
        return pl.pallas_call(
            _proj_kernel, in_specs=in_specs, out_specs=out_spec, out_shape=out_shape,
            name="proj", **common)(x, gain, w)
    return pl.pallas_call(
        _proj_gates_kernel,
        in_specs=in_specs + [pl.BlockSpec((None, GATE_ROWS, d), lambda i, j: (layer, 0, 0))],
        out_specs=(out_spec, pl.BlockSpec((GATE_ROWS, ROW_TILE), lambda i, j: (0, i))),
        out_shape=(out_shape, jax.ShapeDtypeStruct((GATE_ROWS, m), F32)),
        name="proj_gates", **common)(x, gain, w, wgt)


def _kv_kernel(x_ref, g_ref, w_ref, pad_ref, o32_ref, o16_ref, h_ref):
    del pad_ref

    @pl.when(pl.program_id(1) == 0)
    def _():
        h_ref[...] = _rms(x_ref[...], g_ref[...]).astype(BF16)

    y = _dot(h_ref[...], w_ref[...])
    o32_ref[...] = y
    o16_ref[...] = y.astype(BF16)


def _kv_proj(x, gain, w):
    m, d = x.shape
    n = w.shape[-1]
    assert SB_PAD == ROW_TILE
    return pl.pallas_call(
        _kv_kernel,
        grid=(m // ROW_TILE, 2),
        in_specs=[
            pl.BlockSpec((ROW_TILE, d), lambda i, j: (i, 0)),
            pl.BlockSpec((1, d), lambda i, j: (0, 0)),
            pl.BlockSpec((None, d, n), lambda i, j: (j, 0, 0)),
            pl.BlockSpec(memory_space=pl.ANY),
        ],
        out_specs=(pl.BlockSpec((None, ROW_TILE, n), lambda i, j: (j, i, 0)),
                   pl.BlockSpec((None, ROW_TILE, n), lambda i, j: (j, i + 1, 0))),
        out_shape=(jax.ShapeDtypeStruct((2, m, n), F32),
                   jax.ShapeDtypeStruct((2, SB_PAD + m, n), BF16)),
        scratch_shapes=[pltpu.VMEM((ROW_TILE, d), BF16)],
        input_output_aliases={3: 1},
        compiler_params=_params(2),
        name="kv_proj",
    )(x, gain, w, jnp.zeros((2, SB_PAD + m, n), BF16))


def _out_proj_kernel(x_ref, tokp_ref, toks_ref, mop_ref, mos_ref, wt_ref, wm_ref, o_ref, *, n_prompt):
    i = pl.program_id(0)

    @pl.when(i < n_prompt)
    def _():
        o_ref[...] = (x_ref[...] + _dot(tokp_ref[...], wt_ref[...])
                      + _dot(mop_ref[...], wm_ref[...]))

    @pl.when(i >= n_prompt)
    def _():
        o_ref[...] = (x_ref[...] + _dot(toks_ref[...], wt_ref[...])
                      + _dot(mos_ref[...], wm_ref[...]))


def _out_proj(x, tok_p, tok_s, mo_p, mo_s, w, layer):
    m, d = x.shape
    kt = tok_p.shape[1]
    km = mo_p.shape[1]
    tile = tok_s.shape[0]
    n_prompt = tok_p.shape[0] // tile
    assert n_prompt * tile == tok_p.shape[0] and (n_prompt + 1) * tile == m
    prompt = lambda i: (jnp.minimum(i, n_prompt - 1), 0)
    return pl.pallas_call(
        functools.partial(_out_proj_kernel, n_prompt=n_prompt),
        grid=(n_prompt + 1,),
        in_specs=[
            pl.BlockSpec((tile, d), lambda i: (i, 0)),
            pl.BlockSpec((tile, kt), prompt),
            pl.BlockSpec((tile, kt), lambda i: (0, 0)),
            pl.BlockSpec((tile, km), prompt),
            pl.BlockSpec((tile, km), lambda i: (0, 0)),
            pl.BlockSpec((None, kt, d), lambda i: (layer, 0, 0)),
            pl.BlockSpec((None, km, d), lambda i: (layer, kt // km, 0)),
        ],
        out_specs=pl.BlockSpec((tile, d), lambda i: (i, 0)),
        out_shape=jax.ShapeDtypeStruct((m, d), F32),
        compiler_params=_params(1),
        name="out_proj",
    )(x, tok_p, tok_s, mo_p, mo_s, w, w)


def _mem_kv_kernel(mem_ref, g_ref, wk_ref, wv_ref, kg_ref, mk_ref, mv_ref):
    h = _rms(mem_ref[...], g_ref[...]).astype(BF16)
    k = _dot(h, wk_ref[...].astype(BF16))
    v = _dot(h, wv_ref[...].astype(BF16))
    kg = kg_ref[...]
    for hd in range(MEM_HEADS):
        sl = slice(hd * MEM_HD, (hd + 1) * MEM_HD)
        mk_ref[:, hd, :] = _rms(k[:, sl], kg)
        mv_ref[:, hd, :] = v[:, sl]


def _mem_kv(mem, gain, wk, wv, k_gain):
    depth, d, w = wk.shape
    n = mem.shape[0]
    out_spec = pl.BlockSpec((None, n, MEM_HEADS, MEM_HD), lambda l: (l, 0, 0, 0))
    out_shape = jax.ShapeDtypeStruct((depth, n, MEM_HEADS, MEM_HD), F32)
    return pl.pallas_call(
        _mem_kv_kernel,
        grid=(depth,),
        in_specs=[
            pl.BlockSpec((n, d), lambda l: (0, 0)),
            pl.BlockSpec((None, 1, d), lambda l: (l, 0, 0)),
            pl.BlockSpec((None, d, w), lambda l: (l, 0, 0)),
            pl.BlockSpec((None, d, w), lambda l: (l, 0, 0)),
            pl.BlockSpec((None, 1, MEM_HD), lambda l: (l, 0, 0)),
        ],
        out_specs=(out_spec, out_spec),
        out_shape=(out_shape, out_shape),
        compiler_params=_params(1),
        name="mem_kv",
    )(mem, gain, wk, wv, k_gain)


def _mem_attn_kernel(mq_ref, qg_ref, mk_ref, mv_ref, o_ref):
    qg = qg_ref[...]
    scale = MEM_HD ** -0.5
    for hd in range(MEM_HEADS):
        sl = slice(hd * MEM_HD, (hd + 1) * MEM_HD)
        qn = _rms(mq_ref[:, sl], qg).astype(BF16)
        s = _dot_nt(qn, mk_ref[:, hd, :].astype(BF16)) * scale
        e = jnp.exp(s - jnp.max(s, axis=-1, keepdims=True))
        p = e / jnp.sum(e, axis=-1, keepdims=True)
        o_ref[:, sl] = _dot(p.astype(BF16), mv_ref[:, hd, :].astype(BF16)).astype(BF16)


def _mem_attn(proj, col_block, row_block0, rows, n_seq, q_gain, layer, mk, mv, kv_index0, kv_stride):
    w = MEM_HEADS * MEM_HD
    slots = mk.shape[1]
    kv_spec = pl.BlockSpec((None, slots, MEM_HEADS, MEM_HD),
                           lambda b: (kv_index0 + b * kv_stride, 0, 0, 0))
    return pl.pallas_call(
        _mem_attn_kernel,
        grid=(n_seq,),
        in_specs=[
            pl.BlockSpec((rows, w), lambda b: (row_block0 + b, col_block)),
            pl.BlockSpec((None, 1, MEM_HD), lambda b: (layer, 0, 0)),
            kv_spec,
            kv_spec,
        ],
        out_specs=pl.BlockSpec((rows, w), lambda b: (b, 0)),
        out_shape=jax.ShapeDtypeStruct((n_seq * rows, w), BF16),
        compiler_params=_params(1),
        name="mem_attn",
    )(proj, q_gain, mk, mv)


def _mlstm_kernel(q_ref, k_ref, v_ref, og_ref, gc_ref, gr_ref, bc_ref, br_ref, hn_ref,
                  c0_ref, n0_ref, m0_ref, tok_ref, c_ref, n_ref, m_ref, *, chunk):
    L = chunk

    @pl.when(pl.program_id(1) == 0)
    def _():
        c_ref[...] = c0_ref[...]
        n_ref[...] = n0_ref[...]
        m_ref[...] = m0_ref[...]

    row = lax.broadcasted_iota(jnp.int32, (L, L), 0)
    col = lax.broadcasted_iota(jnp.int32, (L, L), 1)
    causal = col <= row
    tri = jnp.where(causal, 1.0, 0.0).astype(BF16)

    pre_c = gc_ref[...] + bc_ref[...]
    pre_r = gr_ref[...] + br_ref[...]
    lf_c = _log_sigmoid(pre_c)
    lf_r = _log_sigmoid(pre_r)
    cum_c = sum(_dot(tri, part) for part in _split3(lf_c))
    cum_r = sum(_dot_nt(part, tri) for part in _split3(lf_r))

    scale = A_DQK ** -0.5
    for h in range(A_HEADS):
        qk = slice(h * A_DQK, (h + 1) * A_DQK)
        vv = slice(h * A_DV, (h + 1) * A_DV)
        q = q_ref[:, qk] * scale
        q16 = q.astype(BF16)
        k = k_ref[:, qk]
        k16 = k.astype(BF16)
        v = v_ref[:, vv]
        i_c = pre_c[:, h:h + 1]
        b_c = cum_c[:, A_HEADS + h:A_HEADS + h + 1]
        i_r = pre_r[h:h + 1, :]
        b_r = cum_r[A_HEADS + h:A_HEADS + h + 1, :]
        c_old = c_ref[h]
        n_old = n_ref[h:h + 1, :]
        m_old = m_ref[h:h + 1, 0:1]

        d = jnp.where(causal, b_c - b_r + i_r, -jnp.inf)
        inter = b_c + m_old
        m_t = jnp.maximum(inter, jnp.max(d, axis=-1, keepdims=True))
        w_in = jnp.exp(d - m_t)
        w_st = jnp.exp(inter - m_t)
        s = _dot_nt(q16, k16) * w_in
        num = _dot(s.astype(BF16), v.astype(BF16)) + w_st * _dot_nt(q16, c_old.astype(BF16))
        den = (jnp.sum(s, axis=-1, keepdims=True)
               + w_st * jnp.sum(q * n_old, axis=-1, keepdims=True))
        hh = num * (1.0 / jnp.maximum(jnp.abs(den), jnp.exp(-m_t)))
        out = _rms(hh, hn_ref[:, vv]) * jax.nn.sigmoid(og_ref[:, vv])
        tok_ref[:, vv] = out.astype(BF16)

        b_end = b_c[L - 1:L, :]
        g = b_end - b_c + i_c
        m_new = jnp.maximum(b_end + m_old, jnp.max(g, axis=0, keepdims=True))
        w_k = jnp.exp(g - m_new)
        decay = jnp.exp(b_end + m_old - m_new)
        vw = (v * w_k).astype(BF16)
        c_ref[h] = decay * c_old + lax.dot_general(vw, k16, TN_DIMS, preferred_element_type=F32)
        n_ref[h:h + 1, :] = decay * n_old + jnp.sum(w_k * k, axis=0, keepdims=True)
        m_ref[h:h + 1, :] = jnp.broadcast_to(m_new, (1, LANES))


def _mlstm(proj, gates_r, bias_c, bias_r, head_norm, c0, n0, m0, row_block0, chunk, n_chunks):
    n_seq = c0.shape[0]
    aq = A_HEADS * A_DQK
    av = A_HEADS * A_DV
    gate_block = (2 * aq + 2 * av + MEM_HEADS * MEM_HD) // LANES

    def rows(b, c):
        return row_block0 + b * n_chunks + c

    state = lambda b, c: (b, 0, 0)
    return pl.pallas_call(
        functools.partial(_mlstm_kernel, chunk=chunk),
        grid=(n_seq, n_chunks),
        in_specs=[
            pl.BlockSpec((chunk, aq), lambda b, c: (rows(b, c), 0)),
            pl.BlockSpec((chunk, aq), lambda b, c: (rows(b, c), 1)),
            pl.BlockSpec((chunk, av), lambda b, c: (rows(b, c), 1)),
            pl.BlockSpec((chunk, av), lambda b, c: (rows(b, c), 2)),
            pl.BlockSpec((chunk, LANES), lambda b, c: (rows(b, c), gate_block)),
            pl.BlockSpec((None, GATE_ROWS, chunk), lambda b, c: (b, 0, c)),
            pl.BlockSpec((1, LANES), lambda b, c: (0, 0)),
            pl.BlockSpec((GATE_ROWS, 1), lambda b, c: (0, 0)),
            pl.BlockSpec((1, av), lambda b, c: (0, 0)),
            pl.BlockSpec((None, A_HEADS, A_DV, A_DQK), lambda b, c: (b, 0, 0, 0)),
            pl.BlockSpec((None, SUBLANES, A_DQK), state),
            pl.BlockSpec((None, SUBLANES, LANES), state),
        ],
        out_specs=(
            pl.BlockSpec((chunk, av), lambda b, c: (b * n_chunks + c, 0)),
            pl.BlockSpec((None, A_HEADS, A_DV, A_DQK), lambda b, c: (b, 0, 0, 0)),
            pl.BlockSpec((None, SUBLANES, A_DQK), state),
            pl.BlockSpec((None, SUBLANES, LANES), state),
        ),
        out_shape=(
            jax.ShapeDtypeStruct((n_seq * n_chunks * chunk, av), BF16),
            jax.ShapeDtypeStruct((n_seq, A_HEADS, A_DV, A_DQK), F32),
            jax.ShapeDtypeStruct((n_seq, SUBLANES, A_DQK), F32),
            jax.ShapeDtypeStruct((n_seq, SUBLANES, LANES), F32),
        ),
        compiler_params=_params(2),
        name="mlstm",
    )(proj, proj, proj, proj, proj, gates_r, bias_c, bias_r, head_norm, c0, n0, m0)


def _sb_scores(q16, k16, valid):
    z = _dot_nt(q16, k16) * (SB_HD ** -0.5)
    sp = jnp.maximum(z, 0.0) + jnp.log(1.0 + jnp.exp(-jnp.abs(z)))
    if valid is not None:
        sp = jnp.where(valid, sp, 0.0)
    return z, sp


def _sb_newer(sp, upper):
    s1 = sp.astype(BF16)
    s2 = (sp - s1.astype(F32)).astype(BF16)
    return _dot(s1, upper) + _dot(s2, upper)


def _sb_weights(z, sp, newer, valid, r_prev):
    a = jnp.exp(z - sp - newer + r_prev)
    if valid is not None:
        a = jnp.where(valid, a, 0.0)
    return a.astype(BF16)


def _sb_tiles(qs, ks, vs, upper, valid, r_prevs):
    scores = [_sb_scores(q, k, valid) for q, k in zip(qs, ks)]
    newer = [_sb_newer(sp, upper) for _, sp in scores]
    outs = [_dot(_sb_weights(z, sp, nw, valid, r), v)
            for (z, sp), nw, r, v in zip(scores, newer, r_prevs, vs)]
    sums = [r - jnp.sum(sp, axis=-1, keepdims=True) for (_, sp), r in zip(scores, r_prevs)]
    return outs, sums


def _upper(n):
    row = lax.broadcasted_iota(jnp.int32, (n, n), 0)
    col = lax.broadcasted_iota(jnp.int32, (n, n), 1)
    return jnp.where(row > col, 1.0, 0.0).astype(BF16)


def _sb_prompt_kernel(q_ref, k_ref, v_ref, o_ref, q16_ref, acc_ref, r_ref):
    t = SB_TILE
    base = pl.program_id(1) * SB_CHAINS + (SB_CHAINS - 1)
    upper = _upper(t)
    row = lax.broadcasted_iota(jnp.int32, (t, t), 0)
    col = lax.broadcasted_iota(jnp.int32, (t, t), 1)
    q16_ref[...] = q_ref[...].astype(BF16)

    def walk(j, diagonal):
        rows = [slice(c * t, (c + 1) * t) for c in range(SB_CHAINS)]
        starts = [pl.multiple_of((base + c - j) * t, t) for c in range(SB_CHAINS)]
        outs, sums = _sb_tiles(
            [q16_ref[r, :] for r in rows],
            [k_ref[pl.ds(s, t), :] for s in starts],
            [v_ref[pl.ds(s, t), :] for s in starts],
            upper, col < row if diagonal else None,
            [jnp.zeros((t, 1), F32) if diagonal else r_ref[r, :] for r in rows])
        r_max = None
        for r, out, total in zip(rows, outs, sums):
            if diagonal:
                acc_ref[r, :] = out
            else:
                acc_ref[r, :] += out
            r_ref[r, :] = total
            r_c = jnp.max(total)
            r_max = r_c if r_max is None else jnp.maximum(r_max, r_c)
        return r_max

    def cond(carry):
        j, r_max = carry
        return jnp.logical_and(j <= base, r_max > EXP_ZERO_BELOW)

    def body(carry):
        j, _ = carry
        return j + 1, walk(j, False)

    lax.while_loop(cond, body, (jnp.int32(1), walk(0, True)))
    o_ref[...] = acc_ref[...].astype(BF16)


def _sb_prompt(proj, kv16, seq):
    step = SB_CHAINS * SB_TILE
    return pl.pallas_call(
        _sb_prompt_kernel,
        grid=(SB_HEADS, seq // step),
        in_specs=[
            pl.BlockSpec((step, SB_HD), lambda h, i: (i, h)),
            pl.BlockSpec((None, SB_PAD + seq, SB_HD), lambda h, i: (0, 0, h)),
            pl.BlockSpec((None, SB_PAD + seq, SB_HD), lambda h, i: (1, 0, h)),
        ],
        out_specs=pl.BlockSpec((step, SB_HD), lambda h, i: (i, h)),
        out_shape=jax.ShapeDtypeStruct((seq, SB_HEADS * SB_HD), BF16),
        scratch_shapes=[pltpu.VMEM((step, SB_HD), BF16), pltpu.VMEM((step, SB_HD), F32),
                        pltpu.VMEM((step, 1), F32)],
        compiler_params=_params(2),
        name="sb_prompt",
    )(proj, kv16, kv16)


def _sb_sample_kernel(q_ref, kn_ref, vn_ref, pk_hbm, pv_hbm, o_ref, kbuf, vbuf, sem, acc_ref, r_ref,
                      *, rows, n_past):
    b = pl.program_id(0)
    t = SB_TILE

    def copies(tile_index):
        start = pl.multiple_of(tile_index * t, t)
        return (
            pltpu.make_async_copy(pk_hbm.at[b, pl.ds(start, t)],
                                  kbuf.at[:, pl.ds(0, SB_HEADS), :], sem.at[0]),
            pltpu.make_async_copy(pv_hbm.at[b, pl.ds(start, t)],
                                  vbuf.at[:, pl.ds(0, SB_HEADS), :], sem.at[1]),
        )

    def fetch(tile_index):
        for cp in copies(tile_index):
            cp.start()

    def wait(tile_index):
        for cp in copies(tile_index):
            cp.wait()

    heads = [slice(h * SB_HD, (h + 1) * SB_HD) for h in range(SB_HEADS)]

    def past_tile():
        outs, sums = _sb_tiles(
            [q_ref[:, sl].astype(BF16) for sl in heads],
            [kbuf[:, h, :].astype(BF16) for h in range(SB_HEADS)],
            [vbuf[:, h, :].astype(BF16) for h in range(SB_HEADS)],
            _upper(t), None, [r_ref[h] for h in range(SB_HEADS)])
        r_max = None
        for h, (out, total) in enumerate(zip(outs, sums)):
            acc_ref[:, heads[h]] += out
            r_ref[h] = total
            r_h = jnp.max(total)
            r_max = r_h if r_max is None else jnp.maximum(r_max, r_h)
        return r_max

    fetch(n_past - 1)
    row = lax.broadcasted_iota(jnp.int32, (rows, rows), 0)
    col = lax.broadcasted_iota(jnp.int32, (rows, rows), 1)
    outs, sums = _sb_tiles(
        [q_ref[:, sl].astype(BF16) for sl in heads], [kn_ref[:, sl] for sl in heads],
        [vn_ref[:, sl] for sl in heads], _upper(rows), col < row,
        [jnp.zeros((rows, 1), F32)] * SB_HEADS)
    for h, (out, total) in enumerate(zip(outs, sums)):
        acc_ref[:, heads[h]] = out
        r_ref[h] = total
    wait(n_past - 1)
    r_max = past_tile()

    def cond(carry):
        tile_index, r_max = carry
        return jnp.logical_and(tile_index >= 0, r_max > EXP_ZERO_BELOW)

    def body(carry):
        tile_index, _ = carry
        fetch(tile_index)
        wait(tile_index)
        return tile_index - 1, past_tile()

    lax.while_loop(cond, body, (jnp.int32(n_past - 2), r_max))
    o_ref[...] = acc_ref[...].astype(BF16)


def _sb_sample(proj, kv16, past_k, past_v, row_block0, rows):
    n_seq, past, heads, hd = past_k.shape
    w = heads * hd
    head_rows = -(-heads // SUBLANES) * SUBLANES
    new_block0 = row_block0 + SB_PAD // rows
    return pl.pallas_call(
        functools.partial(_sb_sample_kernel, rows=rows, n_past=past // SB_TILE),
        grid=(n_seq,),
        in_specs=[
            pl.BlockSpec((rows, w), lambda b: (row_block0 + b, 0)),
            pl.BlockSpec((None, rows, w), lambda b: (0, new_block0 + b, 0)),
            pl.BlockSpec((None, rows, w), lambda b: (1, new_block0 + b, 0)),
            pl.BlockSpec(memory_space=pl.ANY),
            pl.BlockSpec(memory_space=pl.ANY),
        ],
        out_specs=pl.BlockSpec((rows, w), lambda b: (b, 0)),
        out_shape=jax.ShapeDtypeStruct((n_seq * rows, w), BF16),
        scratch_shapes=[
            pltpu.VMEM((SB_TILE, head_rows, hd), F32),
            pltpu.VMEM((SB_TILE, head_rows, hd), F32),
            pltpu.SemaphoreType.DMA((2,)),
            pltpu.VMEM((rows, w), F32),
            pltpu.VMEM((heads, rows, 1), F32),
        ],
        compiler_params=_params(1),
        name="sb_sample",
    )(proj, kv16, kv16, past_k, past_v)


def _pad_rows(a, n):
    return jnp.pad(a, [(0, 0)] * (a.ndim - 2) + [(0, n - a.shape[-2]), (0, 0)])


def kernel(x_prompt, x_sample, state_mlstm_C, state_mlstm_n, state_mlstm_m, cache_sb_k, cache_sb_v,
           cache_mem_k, cache_mem_v, mem_prompt, ffn1_norm, ffn1_w_gate, ffn1_w_up, ffn1_w_down,
           ffn2_norm, ffn2_w_gate, ffn2_w_up, ffn2_w_down, mix_norm, a_w_in, a_b_i, a_b_f, a_head_norm,
           b_w_in, w_out, mem_norm, mem_w_k, mem_w_v, mem_q_norm, mem_k_norm, kv_norm, sb_w_k, sb_w_v):
    n_pb, seq, d = x_prompt.shape
    n_sb, dec_seq, _ = x_sample.shape
    assert n_pb == 1
    depth = ffn1_norm.shape[0]
    n_a = a_w_in.shape[0]
    n_slots = mem_prompt.shape[1]
    aq = A_HEADS * A_DQK
    av = A_HEADS * A_DV
    mem_w = MEM_HEADS * MEM_HD
    sb_w = SB_HEADS * SB_HD
    n_sample = n_sb * dec_seq
    sample_block0 = seq // dec_seq

    gains = lambda g: g.reshape(g.shape[0], 1, g.shape[-1])
    ffn1 = (gains(ffn1_norm), ffn1_w_gate, ffn1_w_up, ffn1_w_down)
    ffn2 = (gains(ffn2_norm), ffn2_w_gate, ffn2_w_up, ffn2_w_down)
    mix_gain = gains(mix_norm)
    n_main = 2 * aq + 2 * av
    w_gates = a_w_in[:, :, n_main:n_main + 2 * A_HEADS]
    a_cols = n_main + mem_w + LANES
    a_tile = 1792
    a_pad = -a_cols % a_tile
    a_w = jnp.concatenate(
        [a_w_in[:, :, :n_main], a_w_in[:, :, n_main + 2 * A_HEADS:], w_gates,
         jnp.zeros((n_a, d, LANES - 2 * A_HEADS + a_pad), F32)], axis=-1).astype(BF16)
    a_wgt = _pad_rows(jnp.swapaxes(w_gates, 1, 2), GATE_ROWS).astype(BF16)
    gate_bias = jnp.concatenate([a_b_i, a_b_f], axis=-1)
    b_w = b_w_in.astype(BF16)
    w_o = w_out.astype(BF16)
    w_kv = jnp.stack([sb_w_k, sb_w_v]).astype(BF16)
    q_gain = mem_q_norm.reshape(depth, 1, MEM_HD)

    mk_p, mv_p = _mem_kv(mem_prompt[0], gains(mem_norm), mem_w_k, mem_w_v,
                         mem_k_norm.reshape(depth, 1, MEM_HD))
    mk_s = cache_mem_k.reshape(depth * n_sb, n_slots, MEM_HEADS, MEM_HD)
    mv_s = cache_mem_v.reshape(depth * n_sb, n_slots, MEM_HEADS, MEM_HD)

    x = jnp.concatenate([x_prompt[0], x_sample.reshape(n_sample, d)], axis=0)
    c_p, n_p, m_p, c_s, n_s, m_s = [], [], [], [], [], []
    kv32 = kv16 = None
    mem_tile = 1024
    for l in range(depth):
        x = _ffn(x, *ffn1, l)
        if l < n_a:
            proj, gates_r = _proj(x, mix_gain, l, a_w, l, a_tile, wgt=a_wgt)
            bias_c = jnp.pad(gate_bias[l], (0, LANES - 2 * A_HEADS)).reshape(1, LANES)
            bias_r = jnp.pad(gate_bias[l], (0, GATE_ROWS - 2 * A_HEADS)).reshape(GATE_ROWS, 1)
            head_norm = a_head_norm[l].reshape(1, av)
            chunk_p = 128
            tok_p, c, n, m = _mlstm(
                proj, gates_r[:, :seq].reshape(1, GATE_ROWS, seq), bias_c, bias_r, head_norm,
                jnp.zeros((1, A_HEADS, A_DV, A_DQK), F32), jnp.zeros((1, SUBLANES, A_DQK), F32),
                jnp.zeros((1, SUBLANES, LANES), F32), 0, chunk_p, seq // chunk_p)
            c_p.append(c); n_p.append(n[:, :A_HEADS]); m_p.append(m[:, :A_HEADS, 0])
            gates_s = gates_r[:, seq:].reshape(GATE_ROWS, n_sb, dec_seq).transpose(1, 0, 2)
            m0 = jnp.broadcast_to(state_mlstm_m[l][:, :, None], (n_sb, A_HEADS, LANES))
            tok_s, c, n, m = _mlstm(
                proj, gates_s, bias_c, bias_r, head_norm, state_mlstm_C[l],
                _pad_rows(state_mlstm_n[l], SUBLANES), _pad_rows(m0, SUBLANES), sample_block0, dec_seq, 1)
            c_s.append(c); n_s.append(n[:, :A_HEADS]); m_s.append(m[:, :A_HEADS, 0])
            mq_block = n_main // mem_w
        else:
            proj = _proj(x, mix_gain, l, b_w, l - n_a, b_w.shape[-1])
            tok_p = _sb_prompt(proj, kv16, seq)
            tok_s = _sb_sample(proj, kv16, cache_sb_k, cache_sb_v, sample_block0, dec_seq)
            mq_block = sb_w // mem_w
        mo_p = _mem_attn(proj, mq_block, 0, mem_tile, seq // mem_tile, q_gain, l, mk_p, mv_p, l, 0)
        mo_s = _mem_attn(proj, mq_block, sample_block0, dec_seq, n_sb, q_gain, l, mk_s, mv_s, l * n_sb, 1)
        x = _out_proj(x, tok_p, tok_s, mo_p, mo_s, w_o, l)
        x = _ffn(x, *ffn2, l)
        if l == n_a - 1:
            kv32, kv16 = _kv_proj(x, kv_norm.reshape(1, d), w_kv)

    y_prompt = x[:seq].reshape(1, seq, d)
    y_sample = x[seq:].reshape(n_sb, dec_seq, d)
    k_p = kv32[0, :seq].reshape(1, seq, SB_HEADS, SB_HD)
    v_p = kv32[1, :seq].reshape(1, seq, SB_HEADS, SB_HD)
    k_s = kv32[0, seq:].reshape(n_sb, dec_seq, SB_HEADS, SB_HD)
    v_s = kv32[1, seq:].reshape(n_sb, dec_seq, SB_HEADS, SB_HD)
    mem_shape = (depth, 1, n_slots, MEM_HEADS, MEM_HD)
    return (y_prompt, y_sample, jnp.stack(c_p), jnp.stack(n_p), jnp.stack(m_p), k_p, v_p,
            mk_p.reshape(mem_shape), mv_p.reshape(mem_shape),
            jnp.stack(c_s), jnp.stack(n_s), jnp.stack(m_s), k_s, v_s)
```

```python
import functools

import jax
import jax.numpy as jnp
from jax import lax
from jax.experimental import pallas as pl
from jax.experimental.pallas import tpu as pltpu

F32 = jnp.float32
BF16 = jnp.bfloat16

RMS_EPS = 1e-6
A_HEADS = 6
A_DQK = 128
A_DV = 256
SB_HEADS = 12
SB_HD = 128
MEM_HEADS = 4
MEM_HD = 128
LANES = 128
SUBLANES = 8
GATE_ROWS = 16

ROW_TILE = 768
FF_TILE = 512
FF_HEAD_TILE = 256
VMEM_LIMIT = 56 * 1024 * 1024

SB_TILE = 256
SB_CHAINS = 4
SB_PAD = (SB_CHAINS - 1) * SB_TILE

NT_DIMS = (((1,), (1,)), ((), ()))
TN_DIMS = (((0,), (0,)), ((), ()))
EXP_ZERO_BELOW = -104.0


def _params(n_axes):
    return pltpu.CompilerParams(
        dimension_semantics=("arbitrary",) * n_axes, vmem_limit_bytes=VMEM_LIMIT)


def _rms(x, g):
    ms = jnp.mean(x * x, axis=-1, keepdims=True)
    return x * lax.rsqrt(ms + RMS_EPS) * g


def _log_sigmoid(x):
    return jnp.minimum(x, 0.0) - jnp.log1p(jnp.exp(-jnp.abs(x)))


def _dot(a, b):
    return jnp.dot(a, b, preferred_element_type=F32)


def _dot_nt(a, b):
    return lax.dot_general(a, b, NT_DIMS, preferred_element_type=F32)


def _split3(x):
    x1 = x.astype(BF16)
    r = x - x1.astype(F32)
    x2 = r.astype(BF16)
    x3 = (r - x2.astype(F32)).astype(BF16)
    return x1, x2, x3


def _ffn_step(h_ref, wg, wu, wd, o_ref):
    h = h_ref[...]
    g = _dot(h, wg)
    u = _dot(h, wu)
    a = (g * jax.nn.sigmoid(g) * u * 0.5).astype(BF16)
    o_ref[...] += _dot(a, wd)


def _ffn_head_kernel(x_ref, g_ref, wg_ref, wu_ref, wd_ref, o_ref, wg16_ref, wu16_ref, wd16_ref, h_ref):
    @pl.when(pl.program_id(0) == 0)
    def _():
        x = x_ref[...]
        h_ref[...] = _rms(x, g_ref[...]).astype(BF16)
        o_ref[...] = x

    wg16_ref[...] = wg_ref[...].astype(BF16)
    wu16_ref[...] = wu_ref[...].astype(BF16)
    wd16_ref[...] = wd_ref[...].astype(BF16)
    _ffn_step(h_ref, wg16_ref[...], wu16_ref[...], wd16_ref[...], o_ref)


def _ffn_rest_kernel(x_ref, head_ref, g_ref, wg_ref, wu_ref, wd_ref, o_ref, h_ref):
    i = pl.program_id(0)
    j = pl.program_id(1)

    @pl.when(jnp.logical_and(i == 0, j == 0))
    def _():
        o_ref[...] = head_ref[...]

    @pl.when(jnp.logical_and(i > 0, j == 0))
    def _():
        x = x_ref[...]
        h_ref[...] = _rms(x, g_ref[...]).astype(BF16)
        o_ref[...] = x

    @pl.when(i > 0)
    def _():
        _ffn_step(h_ref, wg_ref[...], wu_ref[...], wd_ref[...], o_ref)


def _ffn(x, gain, wg, wu, wd, layer):
    m, d = x.shape
    f = wg.shape[-1]
    n_head = f // FF_HEAD_TILE
    w16 = lambda shape: jax.ShapeDtypeStruct(shape, BF16)
    head, wg16, wu16, wd16 = pl.pallas_call(
        _ffn_head_kernel,
        grid=(n_head,),
        in_specs=[
            pl.BlockSpec((ROW_TILE, d), lambda j: (0, 0)),
            pl.BlockSpec((None, 1, d), lambda j: (layer, 0, 0)),
            pl.BlockSpec((None, d, FF_HEAD_TILE), lambda j: (layer, 0, j)),
            pl.BlockSpec((None, d, FF_HEAD_TILE), lambda j: (layer, 0, j)),
            pl.BlockSpec((None, FF_HEAD_TILE, d), lambda j: (layer, j, 0)),
        ],
        out_specs=(
            pl.BlockSpec((ROW_TILE, d), lambda j: (0, 0)),
            pl.BlockSpec((d, FF_HEAD_TILE), lambda j: (0, j)),
            pl.BlockSpec((d, FF_HEAD_TILE), lambda j: (0, j)),
            pl.BlockSpec((FF_HEAD_TILE, d), lambda j: (j, 0)),
        ),
        out_shape=(jax.ShapeDtypeStruct((ROW_TILE, d), F32), w16((d, f)), w16((d, f)), w16((f, d))),
        scratch_shapes=[pltpu.VMEM((ROW_TILE, d), BF16)],
        compiler_params=_params(1),
        name="ffn_head",
    )(x, gain, wg, wu, wd)
    col = lambda i, j: jnp.where(i == 0, 0, j)
    return pl.pallas_call(
        _ffn_rest_kernel,
        grid=(m // ROW_TILE, f // FF_TILE),
        in_specs=[
            pl.BlockSpec((ROW_TILE, d), lambda i, j: (jnp.maximum(i, 1), 0)),
            pl.BlockSpec((ROW_TILE, d), lambda i, j: (0, 0)),
            pl.BlockSpec((None, 1, d), lambda i, j: (layer, 0, 0)),
            pl.BlockSpec((d, FF_TILE), lambda i, j: (0, col(i, j))),
            pl.BlockSpec((d, FF_TILE), lambda i, j: (0, col(i, j))),
            pl.BlockSpec((FF_TILE, d), lambda i, j: (col(i, j), 0)),
        ],
        out_specs=pl.BlockSpec((ROW_TILE, d), lambda i, j: (i, 0)),
        out_shape=jax.ShapeDtypeStruct((m, d), F32),
        scratch_shapes=[pltpu.VMEM((ROW_TILE, d), BF16)],
        compiler_params=_params(2),
        name="ffn",
    )(x, head, gain, wg16, wu16, wd16)


def _proj_kernel(x_ref, g_ref, w_ref, o_ref, h_ref):
    @pl.when(pl.program_id(1) == 0)
    def _():
        h_ref[...] = _rms(x_ref[...], g_ref[...]).astype(BF16)

    o_ref[...] = _dot(h_ref[...], w_ref[...]).astype(o_ref.dtype)


def _proj_gates_kernel(x_ref, g_ref, w_ref, wgt_ref, o_ref, gt_ref, h_ref):
    @pl.when(pl.program_id(1) == 0)
    def _():
        h = _rms(x_ref[...], g_ref[...]).astype(BF16)
        h_ref[...] = h
        gt_ref[...] = _dot_nt(wgt_ref[...], h)

    o_ref[...] = _dot(h_ref[...], w_ref[...]).astype(o_ref.dtype)


def _proj(x, gain, gain_layer, w, layer, col_tile, wgt=None):
    m, d = x.shape
    n = w.shape[-1]
    in_specs = [
        pl.BlockSpec((ROW_TILE, d), lambda i, j: (i, 0)),
        pl.BlockSpec((None, 1, d), lambda i, j: (gain_layer, 0, 0)),
        pl.BlockSpec((None, d, col_tile), lambda i, j: (layer, 0, j)),
    ]
    out_spec = pl.BlockSpec((ROW_TILE, col_tile), lambda i, j: (i, j))
    out_shape = jax.ShapeDtypeStruct((m, n), F32)
    common = dict(
        grid=(m // ROW_TILE, n // col_tile),
        scratch_shapes=[pltpu.VMEM((ROW_TILE, d), BF16)],
        compiler_params=_params(2),
    )
    if wgt is None:
        return pl.pallas_call(
            _proj_kernel, in_specs=in_specs, out_specs=out_spec, out_shape=out_shape,
            name="proj", **common)(x, gain, w)
    return pl.pallas_call(
        _proj_gates_kernel,
        in_specs=in_specs + [pl.BlockSpec((None, GATE_ROWS, d), lambda i, j: (layer, 0, 0))],
        out_specs=(out_spec, pl.BlockSpec((GATE_ROWS, ROW_TILE), lambda i, j: (0, i))),
        out_shape=(out_shape, jax.ShapeDtypeStruct((GATE_ROWS, m), F32)),
        name="proj_gates", **common)(x, gain, w, wgt)


def _kv_kernel(x_ref, g_ref, w_ref, pad_ref, kp_ref, vp_ref, ks_ref, vs_ref, o16_ref,
               *, n_prompt, n_seq, rows):
    del pad_ref
    i = pl.program_id(0)
    h = _rms(x_ref[...], g_ref[...]).astype(BF16)
    heads = [slice(hd * SB_HD, (hd + 1) * SB_HD) for hd in range(SB_HEADS)]
    for j, (p_ref, s_ref) in enumerate(((kp_ref, ks_ref), (vp_ref, vs_ref))):
        y = _dot(h, w_ref[j])
        o16_ref[j] = y.astype(BF16)

        @pl.when(i < n_prompt)
        def _():
            for hd, sl in enumerate(heads):
                p_ref[hd] = y[:, sl]

        @pl.when(i >= n_prompt)
        def _():
            for b in range(n_seq):
                for hd, sl in enumerate(heads):
                    s_ref[b, hd] = y[b * rows:(b + 1) * rows, sl]


def _kv_proj(x, gain, w, seq, n_seq, rows):
    m, d = x.shape
    n = w.shape[-1]
    tile = n_seq * rows
    n_prompt = seq // tile
    assert n_prompt * tile == seq and seq + tile == m and SB_PAD % tile == 0
    prompt_spec = pl.BlockSpec((SB_HEADS, tile, SB_HD), lambda i: (0, jnp.minimum(i, n_prompt - 1), 0))
    sample_spec = pl.BlockSpec((n_seq, SB_HEADS, rows, SB_HD), lambda i: (0, 0, 0, 0))
    prompt_shape = jax.ShapeDtypeStruct((SB_HEADS, seq, SB_HD), F32)
    sample_shape = jax.ShapeDtypeStruct((n_seq, SB_HEADS, rows, SB_HD), F32)
    return pl.pallas_call(
        functools.partial(_kv_kernel, n_prompt=n_prompt, n_seq=n_seq, rows=rows),
        grid=(n_prompt + 1,),
        in_specs=[
            pl.BlockSpec((tile, d), lambda i: (i, 0)),
            pl.BlockSpec((1, d), lambda i: (0, 0)),
            pl.BlockSpec((2, d, n), lambda i: (0, 0, 0)),
            pl.BlockSpec(memory_space=pl.ANY),
        ],
        out_specs=(prompt_spec, prompt_spec, sample_spec, sample_spec,
                   pl.BlockSpec((2, tile, n), lambda i: (0, i + SB_PAD // tile, 0))),
        out_shape=(prompt_shape, prompt_shape, sample_shape, sample_shape,
                   jax.ShapeDtypeStruct((2, SB_PAD + m, n), BF16)),
        input_output_aliases={3: 4},
        compiler_params=_params(1),
        name="kv_proj",
    )(x, gain, w, jnp.zeros((2, SB_PAD + m, n), BF16))


def _out_proj_kernel(x_ref, tokp_ref, toks_ref, mop_ref, mos_ref, wt_ref, wm_ref, o_ref, *, n_prompt):
    i = pl.program_id(0)

    @pl.when(i < n_prompt)
    def _():
        o_ref[...] = (x_ref[...] + _dot(tokp_ref[...], wt_ref[...])
                      + _dot(mop_ref[...], wm_ref[...]))

    @pl.when(i >= n_prompt)
    def _():
        o_ref[...] = (x_ref[...] + _dot(toks_ref[...], wt_ref[...])
                      + _dot(mos_ref[...], wm_ref[...]))


def _out_proj(x, tok_p, tok_s, mo_p, mo_s, w, layer):
    m, d = x.shape
    kt = tok_p.shape[1]
    km = mo_p.shape[1]
    tile = tok_s.shape[0]
    n_prompt = tok_p.shape[0] // tile
    assert n_prompt * tile == tok_p.shape[0] and (n_prompt + 1) * tile == m
    prompt = lambda i: (jnp.minimum(i, n_prompt - 1), 0)
    return pl.pallas_call(
        functools.partial(_out_proj_kernel, n_prompt=n_prompt),
        grid=(n_prompt + 1,),
        in_specs=[
            pl.BlockSpec((tile, d), lambda i: (i, 0)),
            pl.BlockSpec((tile, kt), prompt),
            pl.BlockSpec((tile, kt), lambda i: (0, 0)),
            pl.BlockSpec((tile, km), prompt),
            pl.BlockSpec((tile, km), lambda i: (0, 0)),
            pl.BlockSpec((None, kt, d), lambda i: (layer, 0, 0)),
            pl.BlockSpec((None, km, d), lambda i: (layer, kt // km, 0)),
        ],
        out_specs=pl.BlockSpec((tile, d), lambda i: (i, 0)),
        out_shape=jax.ShapeDtypeStruct((m, d), F32),
        compiler_params=_params(1),
        name="out_proj",
    )(x, tok_p, tok_s, mo_p, mo_s, w, w)


def _mem_kv_kernel(mem_ref, g_ref, wk_ref, wv_ref, kg_ref, mk_ref, mv_ref):
    h = _rms(mem_ref[...], g_ref[...]).astype(BF16)
    k = _dot(h, wk_ref[...].astype(BF16))
    v = _dot(h, wv_ref[...].astype(BF16))
    kg = kg_ref[...]
    for hd in range(MEM_HEADS):
        sl = slice(hd * MEM_HD, (hd + 1) * MEM_HD)
        mk_ref[:, hd, :] = _rms(k[:, sl], kg)
        mv_ref[:, hd, :] = v[:, sl]


def _mem_kv(mem, gain, wk, wv, k_gain):
    depth, d, w = wk.shape
    n = mem.shape[0]
    out_spec = pl.BlockSpec((None, n, MEM_HEADS, MEM_HD), lambda l: (l, 0, 0, 0))
    out_shape = jax.ShapeDtypeStruct((depth, n, MEM_HEADS, MEM_HD), F32)
    return pl.pallas_call(
        _mem_kv_kernel,
        grid=(depth,),
        in_specs=[
            pl.BlockSpec((n, d), lambda l: (0, 0)),
            pl.BlockSpec((None, 1, d), lambda l: (l, 0, 0)),
            pl.BlockSpec((None, d, w), lambda l: (l, 0, 0)),
            pl.BlockSpec((None, d, w), lambda l: (l, 0, 0)),
            pl.BlockSpec((None, 1, MEM_HD), lambda l: (l, 0, 0)),
        ],
        out_specs=(out_spec, out_spec),
        out_shape=(out_shape, out_shape),
        compiler_params=_params(1),
        name="mem_kv",
    )(mem, gain, wk, wv, k_gain)


def _mem_attn_kernel(mq_ref, qg_ref, mk_ref, mv_ref, o_ref):
    qg = qg_ref[...]
    scale = MEM_HD ** -0.5
    for hd in range(MEM_HEADS):
        sl = slice(hd * MEM_HD, (hd + 1) * MEM_HD)
        qn = _rms(mq_ref[:, sl], qg).astype(BF16)
        s = _dot_nt(qn, mk_ref[:, hd, :].astype(BF16)) * scale
        e = jnp.exp(s - jnp.max(s, axis=-1, keepdims=True))
        p = e / jnp.sum(e, axis=-1, keepdims=True)
        o_ref[:, sl] = _dot(p.astype(BF16), mv_ref[:, hd, :].astype(BF16)).astype(BF16)


def _mem_attn(proj, col_block, row_block0, rows, n_seq, q_gain, layer, mk, mv, kv_index0, kv_stride):
    w = MEM_HEADS * MEM_HD
    slots = mk.shape[1]
    kv_spec = pl.BlockSpec((None, slots, MEM_HEADS, MEM_HD),
                           lambda b: (kv_index0 + b * kv_stride, 0, 0, 0))
    return pl.pallas_call(
        _mem_attn_kernel,
        grid=(n_seq,),
        in_specs=[
            pl.BlockSpec((rows, w), lambda b: (row_block0 + b, col_block)),
            pl.BlockSpec((None, 1, MEM_HD), lambda b: (layer, 0, 0)),
            kv_spec,
            kv_spec,
        ],
        out_specs=pl.BlockSpec((rows, w), lambda b: (b, 0)),
        out_shape=jax.ShapeDtypeStruct((n_seq * rows, w), BF16),
        compiler_params=_params(1),
        name="mem_attn",
    )(proj, q_gain, mk, mv)


def _mlstm_kernel(q_ref, k_ref, v_ref, og_ref, gc_ref, gr_ref, bc_ref, br_ref, hn_ref,
                  c0_ref, n0_ref, m0_ref, tok_ref, c_ref, n_ref, m_ref, *, chunk):
    L = chunk

    @pl.when(pl.program_id(1) == 0)
    def _():
        c_ref[...] = c0_ref[...]
        n_ref[...] = n0_ref[...]
        m_ref[...] = m0_ref[...]

    row = lax.broadcasted_iota(jnp.int32, (L, L), 0)
    col = lax.broadcasted_iota(jnp.int32, (L, L), 1)
    causal = col <= row
    tri = jnp.where(causal, 1.0, 0.0).astype(BF16)

    pre_c = gc_ref[...] + bc_ref[...]
    pre_r = gr_ref[...] + br_ref[...]
    lf_c = _log_sigmoid(pre_c)
    lf_r = _log_sigmoid(pre_r)
    cum_c = sum(_dot(tri, part) for part in _split3(lf_c))
    cum_r = sum(_dot_nt(part, tri) for part in _split3(lf_r))

    scale = A_DQK ** -0.5
    heads = range(A_HEADS)
    qk = [slice(h * A_DQK, (h + 1) * A_DQK) for h in heads]
    vv = [slice(h * A_DV, (h + 1) * A_DV) for h in heads]
    q = [q_ref[:, qk[h]] * scale for h in heads]
    q16 = [q[h].astype(BF16) for h in heads]
    k16 = [k_ref[:, qk[h]].astype(BF16) for h in heads]
    c_old = [c_ref[h] for h in heads]
    n_old = [n_ref[h:h + 1, :] for h in heads]
    m_old = [m_ref[h:h + 1, 0:1] for h in heads]
    i_c = [pre_c[:, h:h + 1] for h in heads]
    b_c = [cum_c[:, A_HEADS + h:A_HEADS + h + 1] for h in heads]
    i_r = [pre_r[h:h + 1, :] for h in heads]
    b_r = [cum_r[A_HEADS + h:A_HEADS + h + 1, :] for h in heads]

    qk_t = [_dot_nt(q16[h], k16[h]) for h in heads]
    q_c = [_dot_nt(q16[h], c_old[h].astype(BF16)) for h in heads]

    s, w_st, m_t = [], [], []
    for h in heads:
        d = jnp.where(causal, b_c[h] - b_r[h] + i_r[h], -jnp.inf)
        inter = b_c[h] + m_old[h]
        m_t.append(jnp.maximum(inter, jnp.max(d, axis=-1, keepdims=True)))
        w_st.append(jnp.exp(inter - m_t[h]))
        s.append(qk_t[h] * jnp.exp(d - m_t[h]))
    s_v = [_dot(s[h].astype(BF16), v_ref[:, vv[h]].astype(BF16)) for h in heads]

    for h in heads:
        num = s_v[h] + w_st[h] * q_c[h]
        den = (jnp.sum(s[h], axis=-1, keepdims=True)
               + w_st[h] * jnp.sum(q[h] * n_old[h], axis=-1, keepdims=True))
        hh = num * (1.0 / jnp.maximum(jnp.abs(den), jnp.exp(-m_t[h])))
        out = _rms(hh, hn_ref[:, vv[h]]) * jax.nn.sigmoid(og_ref[:, vv[h]])
        tok_ref[:, vv[h]] = out.astype(BF16)

    decay, w_k = [], []
    for h in heads:
        b_end = b_c[h][L - 1:L, :]
        g = b_end - b_c[h] + i_c[h]
        m_new = jnp.maximum(b_end + m_old[h], jnp.max(g, axis=0, keepdims=True))
        w_k.append(jnp.exp(g - m_new))
        decay.append(jnp.exp(b_end + m_old[h] - m_new))
        m_ref[h:h + 1, :] = jnp.broadcast_to(m_new, (1, LANES))
    vw_k = [lax.dot_general((v_ref[:, vv[h]] * w_k[h]).astype(BF16), k16[h], TN_DIMS,
                            preferred_element_type=F32) for h in heads]
    for h in heads:
        c_ref[h] = decay[h] * c_old[h] + vw_k[h]
        n_ref[h:h + 1, :] = (decay[h] * n_old[h]
                             + jnp.sum(w_k[h] * k_ref[:, qk[h]], axis=0, keepdims=True))


def _mlstm(proj, gates_r, bias_c, bias_r, head_norm, c0, n0, m0, row_block0, chunk, n_chunks):
    n_seq = c0.shape[0]
    aq = A_HEADS * A_DQK
    av = A_HEADS * A_DV
    gate_block = (2 * aq + 2 * av + MEM_HEADS * MEM_HD) // LANES

    def rows(b, c):
        return row_block0 + b * n_chunks + c

    state = lambda b, c: (b, 0, 0)
    return pl.pallas_call(
        functools.partial(_mlstm_kernel, chunk=chunk),
        grid=(n_seq, n_chunks),
        in_specs=[
            pl.BlockSpec((chunk, aq), lambda b, c: (rows(b, c), 0)),
            pl.BlockSpec((chunk, aq), lambda b, c: (rows(b, c), 1)),
            pl.BlockSpec((chunk, av), lambda b, c: (rows(b, c), 1)),
            pl.BlockSpec((chunk, av), lambda b, c: (rows(b, c), 2)),
            pl.BlockSpec((chunk, LANES), lambda b, c: (rows(b, c), gate_block)),
            pl.BlockSpec((None, GATE_ROWS, chunk), lambda b, c: (b, 0, c)),
            pl.BlockSpec((1, LANES), lambda b, c: (0, 0)),
            pl.BlockSpec((GATE_ROWS, 1), lambda b, c: (0, 0)),
            pl.BlockSpec((1, av), lambda b, c: (0, 0)),
            pl.BlockSpec((None, A_HEADS, A_DV, A_DQK), lambda b, c: (b, 0, 0, 0)),
            pl.BlockSpec((None, SUBLANES, A_DQK), state),
            pl.BlockSpec((None, SUBLANES, LANES), state),
        ],
        out_specs=(
            pl.BlockSpec((chunk, av), lambda b, c: (b * n_chunks + c, 0)),
            pl.BlockSpec((None, A_HEADS, A_DV, A_DQK), lambda b, c: (b, 0, 0, 0)),
            pl.BlockSpec((None, SUBLANES, A_DQK), state),
            pl.BlockSpec((None, SUBLANES, LANES), state),
        ),
        out_shape=(
            jax.ShapeDtypeStruct((n_seq * n_chunks * chunk, av), BF16),
            jax.ShapeDtypeStruct((n_seq, A_HEADS, A_DV, A_DQK), F32),
            jax.ShapeDtypeStruct((n_seq, SUBLANES, A_DQK), F32),
            jax.ShapeDtypeStruct((n_seq, SUBLANES, LANES), F32),
        ),
        compiler_params=_params(2),
        name="mlstm",
    )(proj, proj, proj, proj, proj, gates_r, bias_c, bias_r, head_norm, c0, n0, m0)


def _sb_scores(q16, k16, valid):
    z = _dot_nt(q16, k16) * (SB_HD ** -0.5)
    sp = jnp.maximum(z, 0.0) + jnp.log(1.0 + jnp.exp(-jnp.abs(z)))
    if valid is not None:
        sp = jnp.where(valid, sp, 0.0)
    return z, sp


def _sb_newer(sp, upper):
    s1 = sp.astype(BF16)
    s2 = (sp - s1.astype(F32)).astype(BF16)
    return _dot(s1, upper) + _dot(s2, upper)


def _sb_weights(z, sp, newer, valid, r_prev):
    a = jnp.exp(z - sp - newer + r_prev)
    if valid is not None:
        a = jnp.where(valid, a, 0.0)
    return a.astype(BF16)


def _sb_tiles(qs, ks, vs, upper, valid, r_prevs):
    scores = [_sb_scores(q, k, valid) for q, k in zip(qs, ks)]
    newer = [_sb_newer(sp, upper) for _, sp in scores]
    outs = [_dot(_sb_weights(z, sp, nw, valid, r), v)
            for (z, sp), nw, r, v in zip(scores, newer, r_prevs, vs)]
    sums = [r - jnp.sum(sp, axis=-1, keepdims=True) for (_, sp), r in zip(scores, r_prevs)]
    return outs, sums


def _upper(n):
    row = lax.broadcasted_iota(jnp.int32, (n, n), 0)
    col = lax.broadcasted_iota(jnp.int32, (n, n), 1)
    return jnp.where(row > col, 1.0, 0.0).astype(BF16)


def _sb_prompt_kernel(q_ref, k_ref, v_ref, o_ref, q16_ref, acc_ref, r_ref):
    t = SB_TILE
    base = pl.program_id(1) * SB_CHAINS + (SB_CHAINS - 1)
    upper = _upper(t)
    row = lax.broadcasted_iota(jnp.int32, (t, t), 0)
    col = lax.broadcasted_iota(jnp.int32, (t, t), 1)
    q16_ref[...] = q_ref[...].astype(BF16)

    def walk(j, diagonal):
        rows = [slice(c * t, (c + 1) * t) for c in range(SB_CHAINS)]
        starts = [pl.multiple_of((base + c - j) * t, t) for c in range(SB_CHAINS)]
        outs, sums = _sb_tiles(
            [q16_ref[r, :] for r in rows],
            [k_ref[pl.ds(s, t), :] for s in starts],
            [v_ref[pl.ds(s, t), :] for s in starts],
            upper, col < row if diagonal else None,
            [jnp.zeros((t, 1), F32) if diagonal else r_ref[r, :] for r in rows])
        r_max = None
        for r, out, total in zip(rows, outs, sums):
            if diagonal:
                acc_ref[r, :] = out
            else:
                acc_ref[r, :] += out
            r_ref[r, :] = total
            r_c = jnp.max(total)
            r_max = r_c if r_max is None else jnp.maximum(r_max, r_c)
        return r_max

    def cond(carry):
        j, r_max = carry
        return jnp.logical_and(j <= base, r_max > EXP_ZERO_BELOW)

    def body(carry):
        j, _ = carry
        return j + 1, walk(j, False)

    lax.while_loop(cond, body, (jnp.int32(1), walk(0, True)))
    o_ref[...] = acc_ref[...].astype(BF16)


def _sb_prompt(proj, kv16, seq):
    step = SB_CHAINS * SB_TILE
    return pl.pallas_call(
        _sb_prompt_kernel,
        grid=(SB_HEADS, seq // step),
        in_specs=[
            pl.BlockSpec((step, SB_HD), lambda h, i: (i, h)),
            pl.BlockSpec((None, SB_PAD + seq, SB_HD), lambda h, i: (0, 0, h)),
            pl.BlockSpec((None, SB_PAD + seq, SB_HD), lambda h, i: (1, 0, h)),
        ],
        out_specs=pl.BlockSpec((step, SB_HD), lambda h, i: (i, h)),
        out_shape=jax.ShapeDtypeStruct((seq, SB_HEADS * SB_HD), BF16),
        scratch_shapes=[pltpu.VMEM((step, SB_HD), BF16), pltpu.VMEM((step, SB_HD), F32),
                        pltpu.VMEM((step, 1), F32)],
        compiler_params=_params(2),
        name="sb_prompt",
    )(proj, kv16, kv16)


def _sb_sample_kernel(q_ref, kn_ref, vn_ref, pk_hbm, pv_hbm, o_ref, kbuf, vbuf, sem, acc_ref, r_ref,
                      *, rows, n_past):
    b = pl.program_id(0)
    t = SB_TILE

    def copies(tile_index):
        start = pl.multiple_of(tile_index * t, t)
        return (
            pltpu.make_async_copy(pk_hbm.at[b, :, pl.ds(start, t), :], kbuf, sem.at[0]),
            pltpu.make_async_copy(pv_hbm.at[b, :, pl.ds(start, t), :], vbuf, sem.at[1]),
        )

    def fetch(tile_index):
        for cp in copies(tile_index):
            cp.start()

    def wait(tile_index):
        for cp in copies(tile_index):
            cp.wait()

    heads = [slice(h * SB_HD, (h + 1) * SB_HD) for h in range(SB_HEADS)]

    def past_tile():
        outs, sums = _sb_tiles(
            [q_ref[:, sl].astype(BF16) for sl in heads],
            [kbuf[h].astype(BF16) for h in range(SB_HEADS)],
            [vbuf[h].astype(BF16) for h in range(SB_HEADS)],
            _upper(t), None, [r_ref[h] for h in range(SB_HEADS)])
        r_max = None
        for h, (out, total) in enumerate(zip(outs, sums)):
            acc_ref[:, heads[h]] += out
            r_ref[h] = total
            r_h = jnp.max(total)
            r_max = r_h if r_max is None else jnp.maximum(r_max, r_h)
        return r_max

    fetch(n_past - 1)
    row = lax.broadcasted_iota(jnp.int32, (rows, rows), 0)
    col = lax.broadcasted_iota(jnp.int32, (rows, rows), 1)
    outs, sums = _sb_tiles(
        [q_ref[:, sl].astype(BF16) for sl in heads], [kn_ref[:, sl] for sl in heads],
        [vn_ref[:, sl] for sl in heads], _upper(rows), col < row,
        [jnp.zeros((rows, 1), F32)] * SB_HEADS)
    for h, (out, total) in enumerate(zip(outs, sums)):
        acc_ref[:, heads[h]] = out
        r_ref[h] = total
    wait(n_past - 1)
    r_max = past_tile()

    def cond(carry):
        tile_index, r_max = carry
        return jnp.logical_and(tile_index >= 0, r_max > EXP_ZERO_BELOW)

    def body(carry):
        tile_index, _ = carry
        fetch(tile_index)
        wait(tile_index)
        return tile_index - 1, past_tile()

    lax.while_loop(cond, body, (jnp.int32(n_past - 2), r_max))
    o_ref[...] = acc_ref[...].astype(BF16)


def _sb_sample(proj, kv16, past_k, past_v, row_block0, rows):
    n_seq, heads, past, hd = past_k.shape
    w = heads * hd
    new_block0 = row_block0 + SB_PAD // rows
    return pl.pallas_call(
        functools.partial(_sb_sample_kernel, rows=rows, n_past=past // SB_TILE),
        grid=(n_seq,),
        in_specs=[
            pl.BlockSpec((rows, w), lambda b: (row_block0 + b, 0)),
            pl.BlockSpec((None, rows, w), lambda b: (0, new_block0 + b, 0)),
            pl.BlockSpec((None, rows, w), lambda b: (1, new_block0 + b, 0)),
            pl.BlockSpec(memory_space=pl.ANY),
            pl.BlockSpec(memory_space=pl.ANY),
        ],
        out_specs=pl.BlockSpec((rows, w), lambda b: (b, 0)),
        out_shape=jax.ShapeDtypeStruct((n_seq * rows, w), BF16),
        scratch_shapes=[
            pltpu.VMEM((heads, SB_TILE, hd), F32),
            pltpu.VMEM((heads, SB_TILE, hd), F32),
            pltpu.SemaphoreType.DMA((2,)),
            pltpu.VMEM((rows, w), F32),
            pltpu.VMEM((heads, rows, 1), F32),
        ],
        compiler_params=_params(1),
        name="sb_sample",
    )(proj, kv16, kv16, past_k, past_v)


def _pad_rows(a, n):
    return jnp.pad(a, [(0, 0)] * (a.ndim - 2) + [(0, n - a.shape[-2]), (0, 0)])


def kernel(x_prompt, x_sample, state_mlstm_C, state_mlstm_n, state_mlstm_m, cache_sb_k, cache_sb_v,
           cache_mem_k, cache_mem_v, mem_prompt, ffn1_norm, ffn1_w_gate, ffn1_w_up, ffn1_w_down,
           ffn2_norm, ffn2_w_gate, ffn2_w_up, ffn2_w_down, mix_norm, a_w_in, a_b_i, a_b_f, a_head_norm,
           b_w_in, w_out, mem_norm, mem_w_k, mem_w_v, mem_q_norm, mem_k_norm, kv_norm, sb_w_k, sb_w_v):
    n_pb, seq, d = x_prompt.shape
    n_sb, dec_seq, _ = x_sample.shape
    assert n_pb == 1
    depth = ffn1_norm.shape[0]
    n_a = a_w_in.shape[0]
    n_slots = mem_prompt.shape[1]
    aq = A_HEADS * A_DQK
    av = A_HEADS * A_DV
    mem_w = MEM_HEADS * MEM_HD
    sb_w = SB_HEADS * SB_HD
    n_sample = n_sb * dec_seq
    sample_block0 = seq // dec_seq

    gains = lambda g: g.reshape(g.shape[0], 1, g.shape[-1])
    ffn1 = (gains(ffn1_norm), ffn1_w_gate, ffn1_w_up, ffn1_w_down)
    ffn2 = (gains(ffn2_norm), ffn2_w_gate, ffn2_w_up, ffn2_w_down)
    mix_gain = gains(mix_norm)
    n_main = 2 * aq + 2 * av
    w_gates = a_w_in[:, :, n_main:n_main + 2 * A_HEADS]
    a_cols = n_main + mem_w + LANES
    a_tile = 1792
    a_pad = -a_cols % a_tile
    a_w = jnp.concatenate(
        [a_w_in[:, :, :n_main], a_w_in[:, :, n_main + 2 * A_HEADS:], w_gates,
         jnp.zeros((n_a, d, LANES - 2 * A_HEADS + a_pad), F32)], axis=-1).astype(BF16)
    a_wgt = _pad_rows(jnp.swapaxes(w_gates, 1, 2), GATE_ROWS).astype(BF16)
    gate_bias = jnp.concatenate([a_b_i, a_b_f], axis=-1)
    b_w = b_w_in.astype(BF16)
    w_o = w_out.astype(BF16)
    w_kv = jnp.stack([sb_w_k, sb_w_v]).astype(BF16)
    q_gain = mem_q_norm.reshape(depth, 1, MEM_HD)

    mk_p, mv_p = _mem_kv(mem_prompt[0], gains(mem_norm), mem_w_k, mem_w_v,
                         mem_k_norm.reshape(depth, 1, MEM_HD))
    mk_s = cache_mem_k.reshape(depth * n_sb, n_slots, MEM_HEADS, MEM_HD)
    mv_s = cache_mem_v.reshape(depth * n_sb, n_slots, MEM_HEADS, MEM_HD)

    past_k = jnp.transpose(cache_sb_k, (0, 2, 1, 3))
    past_v = jnp.transpose(cache_sb_v, (0, 2, 1, 3))

    x = jnp.concatenate([x_prompt[0], x_sample.reshape(n_sample, d)], axis=0)
    c_p, n_p, m_p, c_s, n_s, m_s = [], [], [], [], [], []
    k_p = v_p = k_s = v_s = kv16 = None
    mem_tile = 1024
    for l in range(depth):
        x = _ffn(x, *ffn1, l)
        if l < n_a:
            proj, gates_r = _proj(x, mix_gain, l, a_w, l, a_tile, wgt=a_wgt)
            bias_c = jnp.pad(gate_bias[l], (0, LANES - 2 * A_HEADS)).reshape(1, LANES)
            bias_r = jnp.pad(gate_bias[l], (0, GATE_ROWS - 2 * A_HEADS)).reshape(GATE_ROWS, 1)
            head_norm = a_head_norm[l].reshape(1, av)
            chunk_p = 128
            tok_p, c, n, m = _mlstm(
                proj, gates_r[:, :seq].reshape(1, GATE_ROWS, seq), bias_c, bias_r, head_norm,
                jnp.zeros((1, A_HEADS, A_DV, A_DQK), F32), jnp.zeros((1, SUBLANES, A_DQK), F32),
                jnp.zeros((1, SUBLANES, LANES), F32), 0, chunk_p, seq // chunk_p)
            c_p.append(c); n_p.append(n[:, :A_HEADS]); m_p.append(m[:, :A_HEADS, 0])
            gates_s = gates_r[:, seq:].reshape(GATE_ROWS, n_sb, dec_seq).transpose(1, 0, 2)
            m0 = jnp.broadcast_to(state_mlstm_m[l][:, :, None], (n_sb, A_HEADS, LANES))
            tok_s, c, n, m = _mlstm(
                proj, gates_s, bias_c, bias_r, head_norm, state_mlstm_C[l],
                _pad_rows(state_mlstm_n[l], SUBLANES), _pad_rows(m0, SUBLANES), sample_block0, dec_seq, 1)
            c_s.append(c); n_s.append(n[:, :A_HEADS]); m_s.append(m[:, :A_HEADS, 0])
            mq_block = n_main // mem_w
        else:
            proj = _proj(x, mix_gain, l, b_w, l - n_a, b_w.shape[-1])
            tok_p = _sb_prompt(proj, kv16, seq)
            tok_s = _sb_sample(proj, kv16, past_k, past_v, sample_block0, dec_seq)
            mq_block = sb_w // mem_w
        mo_p = _mem_attn(proj, mq_block, 0, mem_tile, seq // mem_tile, q_gain, l, mk_p, mv_p, l, 0)
        mo_s = _mem_attn(proj, mq_block, sample_block0, dec_seq, n_sb, q_gain, l, mk_s, mv_s, l * n_sb, 1)
        x = _out_proj(x, tok_p, tok_s, mo_p, mo_s, w_o, l)
        x = _ffn(x, *ffn2, l)
        if l == n_a - 1:
            k_p, v_p, k_s, v_s, kv16 = _kv_proj(x, kv_norm.reshape(1, d), w_kv, seq, n_sb, dec_seq)

    y_prompt = x[:seq].reshape(1, seq, d)
    y_sample = x[seq:].reshape(n_sb, dec_seq, d)
    k_p, v_p = (jnp.transpose(a, (1, 0, 2))[None] for a in (k_p, v_p))
    k_s, v_s = (jnp.transpose(a, (0, 2, 1, 3)) for a in (k_s, v_s))
    mem_shape = (depth, 1, n_slots, MEM_HEADS, MEM_HD)
    return (y_prompt, y_sample, jnp.stack(c_p), jnp.stack(n_p), jnp.stack(m_p), k_p, v_p,
            mk_p.reshape(mem_shape), mv_p.reshape(mem_shape),
            jnp.stack(c_s), jnp.stack(n_s), jnp.stack(m_s), k_s, v_s)
```

```python
import functools

import jax
import jax.numpy as jnp
from jax import lax
from jax.experimental import pallas as pl
from jax.experimental.pallas import tpu as pltpu

F32 = jnp.float32
BF16 = jnp.bfloat16

RMS_EPS = 1e-6
A_HEADS = 6
A_DQK = 128
A_DV = 256
SB_HEADS = 12
SB_HD = 128
MEM_HEADS = 4
MEM_HD = 128
LANES = 128
SUBLANES = 8
GATE_ROWS = 16

ROW_TILE = 768
FF_TILE = 512
FF_HEAD_TILE = 256
VMEM_LIMIT = 56 * 1024 * 1024

SB_TILE = 128
SB_CHAINS = 8
SB_PAST_TILE = 256
SB_PAD = 1024

NT_DIMS = (((1,), (1,)), ((), ()))
TN_DIMS = (((0,), (0,)), ((), ()))
EXP_ZERO_BELOW = -104.0


def _params(n_axes):
    return pltpu.CompilerParams(
        dimension_semantics=("arbitrary",) * n_axes, vmem_limit_bytes=VMEM_LIMIT)


def _rms(x, g):
    ms = jnp.mean(x * x, axis=-1, keepdims=True)
    return x * lax.rsqrt(ms + RMS_EPS) * g


def _log_sigmoid(x):
    return jnp.minimum(x, 0.0) - jnp.log1p(jnp.exp(-jnp.abs(x)))


def _dot(a, b):
    return jnp.dot(a, b, preferred_element_type=F32)


def _dot_nt(a, b):
    return lax.dot_general(a, b, NT_DIMS, preferred_element_type=F32)


def _split3(x):
    x1 = x.astype(BF16)
    r = x - x1.astype(F32)
    x2 = r.astype(BF16)
    x3 = (r - x2.astype(F32)).astype(BF16)
    return x1, x2, x3


def _ffn_step(h_ref, wg, wu, wd, o_ref):
    h = h_ref[...]
    g = _dot(h, wg)
    u = _dot(h, wu)
    a = (g * jax.nn.sigmoid(g) * u * 0.5).astype(BF16)
    o_ref[...] += _dot(a, wd)


def _ffn_head_kernel(x_ref, g_ref, wg_ref, wu_ref, wd_ref, o_ref, wg16_ref, wu16_ref, wd16_ref, h_ref):
    @pl.when(pl.program_id(0) == 0)
    def _():
        x = x_ref[...]
        h_ref[...] = _rms(x, g_ref[...]).astype(BF16)
        o_ref[...] = x

    wg16_ref[...] = wg_ref[...].astype(BF16)
    wu16_ref[...] = wu_ref[...].astype(BF16)
    wd16_ref[...] = wd_ref[...].astype(BF16)
    _ffn_step(h_ref, wg16_ref[...], wu16_ref[...], wd16_ref[...], o_ref)


def _ffn_rest_kernel(x_ref, head_ref, g_ref, wg_ref, wu_ref, wd_ref, o_ref, h_ref):
    i = pl.program_id(0)
    j = pl.program_id(1)

    @pl.when(jnp.logical_and(i == 0, j == 0))
    def _():
        o_ref[...] = head_ref[...]

    @pl.when(jnp.logical_and(i > 0, j == 0))
    def _():
        x = x_ref[...]
        h_ref[...] = _rms(x, g_ref[...]).astype(BF16)
        o_ref[...] = x

    @pl.when(i > 0)
    def _():
        _ffn_step(h_ref, wg_ref[...], wu_ref[...], wd_ref[...], o_ref)


def _ffn(x, gain, wg, wu, wd, layer):
    m, d = x.shape
    f = wg.shape[-1]
    n_head = f // FF_HEAD_TILE
    w16 = lambda shape: jax.ShapeDtypeStruct(shape, BF16)
    head, wg16, wu16, wd16 = pl.pallas_call(
        _ffn_head_kernel,
        grid=(n_head,),
        in_specs=[
            pl.BlockSpec((ROW_TILE, d), lambda j: (0, 0)),
            pl.BlockSpec((None, 1, d), lambda j: (layer, 0, 0)),
            pl.BlockSpec((None, d, FF_HEAD_TILE), lambda j: (layer, 0, j)),
            pl.BlockSpec((None, d, FF_HEAD_TILE), lambda j: (layer, 0, j)),
            pl.BlockSpec((None, FF_HEAD_TILE, d), lambda j: (layer, j, 0)),
        ],
        out_specs=(
            pl.BlockSpec((ROW_TILE, d), lambda j: (0, 0)),
            pl.BlockSpec((d, FF_HEAD_TILE), lambda j: (0, j)),
            pl.BlockSpec((d, FF_HEAD_TILE), lambda j: (0, j)),
            pl.BlockSpec((FF_HEAD_TILE, d), lambda j: (j, 0)),
        ),
        out_shape=(jax.ShapeDtypeStruct((ROW_TILE, d), F32), w16((d, f)), w16((d, f)), w16((f, d))),
        scratch_shapes=[pltpu.VMEM((ROW_TILE, d), BF16)],
        compiler_params=_params(1),
        name="ffn_head",
    )(x, gain, wg, wu, wd)
    col = lambda i, j: jnp.where(i == 0, 0, j)
    return pl.pallas_call(
        _ffn_rest_kernel,
        grid=(m // ROW_TILE, f // FF_TILE),
        in_specs=[
            pl.BlockSpec((ROW_TILE, d), lambda i, j: (jnp.maximum(i, 1), 0)),
            pl.BlockSpec((ROW_TILE, d), lambda i, j: (0, 0)),
            pl.BlockSpec((None, 1, d), lambda i, j: (layer, 0, 0)),
            pl.BlockSpec((d, FF_TILE), lambda i, j: (0, col(i, j))),
            pl.BlockSpec((d, FF_TILE), lambda i, j: (0, col(i, j))),
            pl.BlockSpec((FF_TILE, d), lambda i, j: (col(i, j), 0)),
        ],
        out_specs=pl.BlockSpec((ROW_TILE, d), lambda i, j: (i, 0)),
        out_shape=jax.ShapeDtypeStruct((m, d), F32),
        scratch_shapes=[pltpu.VMEM((ROW_TILE, d), BF16)],
        compiler_params=_params(2),
        name="ffn",
    )(x, head, gain, wg16, wu16, wd16)


def _proj_kernel(x_ref, g_ref, w_ref, o_ref, h_ref):
    @pl.when(pl.program_id(1) == 0)
    def _():
        h_ref[...] = _rms(x_ref[...], g_ref[...]).astype(BF16)

    o_ref[...] = _dot(h_ref[...], w_ref[...]).astype(o_ref.dtype)


def _proj_gates_kernel(x_ref, g_ref, w_ref, wgt_ref, o_ref, gt_ref, h_ref):
    @pl.when(pl.program_id(1) == 0)
    def _():
        h = _rms(x_ref[...], g_ref[...]).astype(BF16)
        h_ref[...] = h
        gt_ref[...] = _dot_nt(wgt_ref[...], h)

    o_ref[...] = _dot(h_ref[...], w_ref[...]).astype(o_ref.dtype)


def _proj(x, gain, gain_layer, w, layer, col_tile, wgt=None):
    m, d = x.shape
    n = w.shape[-1]
    in_specs = [
        pl.BlockSpec((ROW_TILE, d), lambda i, j: (i, 0)),
        pl.BlockSpec((None, 1, d), lambda i, j: (gain_layer, 0, 0)),
        pl.BlockSpec((None, d, col_tile), lambda i, j: (layer, 0, j)),
    ]
    out_spec = pl.BlockSpec((ROW_TILE, col_tile), lambda i, j: (i, j))
    out_shape = jax.ShapeDtypeStruct((m, n), F32)
    common = dict(
        grid=(m // ROW_TILE, n // col_tile),
        scratch_shapes=[pltpu.VMEM((ROW_TILE, d), BF16)],
        compiler_params=_params(2),
    )
    if wgt is None:
        return pl.pallas_call(
            _proj_kernel, in_specs=in_specs, out_specs=out_spec, out_shape=out_shape,
            name="proj", **common)(x, gain, w)
    return pl.pallas_call(
        _proj_gates_kernel,
        in_specs=in_specs + [pl.BlockSpec((None, GATE_ROWS, d), lambda i, j: (layer, 0, 0))],
        out_specs=(out_spec, pl.BlockSpec((GATE_ROWS, ROW_TILE), lambda i, j: (0, i))),
        out_shape=(out_shape, jax.ShapeDtypeStruct((GATE_ROWS, m), F32)),
        name="proj_gates", **common)(x, gain, w, wgt)


def _kv_kernel(x_ref, g_ref, w_ref, pad_ref, kp_ref, vp_ref, ks_ref, vs_ref, o16_ref,
               *, n_prompt, n_seq, rows):
    del pad_ref
    i = pl.program_id(0)
    h = _rms(x_ref[...], g_ref[...]).astype(BF16)
    heads = [slice(hd * SB_HD, (hd + 1) * SB_HD) for hd in range(SB_HEADS)]
    for j, (p_ref, s_ref) in enumerate(((kp_ref, ks_ref), (vp_ref, vs_ref))):
        y = _dot(h, w_ref[j])
        o16_ref[j] = y.astype(BF16)

        @pl.when(i < n_prompt)
        def _():
            for hd, sl in enumerate(heads):
                p_ref[hd] = y[:, sl]

        @pl.when(i >= n_prompt)
        def _():
            for b in range(n_seq):
                for hd, sl in enumerate(heads):
                    s_ref[b, hd] = y[b * rows:(b + 1) * rows, sl]


def _kv_proj(x, gain, w, seq, n_seq, rows):
    m, d = x.shape
    n = w.shape[-1]
    tile = n_seq * rows
    n_prompt = seq // tile
    assert n_prompt * tile == seq and seq + tile == m and SB_PAD % tile == 0
    prompt_spec = pl.BlockSpec((SB_HEADS, tile, SB_HD), lambda i: (0, jnp.minimum(i, n_prompt - 1), 0))
    sample_spec = pl.BlockSpec((n_seq, SB_HEADS, rows, SB_HD), lambda i: (0, 0, 0, 0))
    prompt_shape = jax.ShapeDtypeStruct((SB_HEADS, seq, SB_HD), F32)
    sample_shape = jax.ShapeDtypeStruct((n_seq, SB_HEADS, rows, SB_HD), F32)
    return pl.pallas_call(
        functools.partial(_kv_kernel, n_prompt=n_prompt, n_seq=n_seq, rows=rows),
        grid=(n_prompt + 1,),
        in_specs=[
            pl.BlockSpec((tile, d), lambda i: (i, 0)),
            pl.BlockSpec((1, d), lambda i: (0, 0)),
            pl.BlockSpec((2, d, n), lambda i: (0, 0, 0)),
            pl.BlockSpec(memory_space=pl.ANY),
        ],
        out_specs=(prompt_spec, prompt_spec, sample_spec, sample_spec,
                   pl.BlockSpec((2, tile, n), lambda i: (0, i + SB_PAD // tile, 0))),
        out_shape=(prompt_shape, prompt_shape, sample_shape, sample_shape,
                   jax.ShapeDtypeStruct((2, SB_PAD + m, n), BF16)),
        input_output_aliases={3: 4},
        compiler_params=_params(1),
        name="kv_proj",
    )(x, gain, w, jnp.zeros((2, SB_PAD + m, n), BF16))


def _out_proj_kernel(x_ref, tokp_ref, toks_ref, mop_ref, mos_ref, wt_ref, wm_ref, o_ref, *, n_prompt):
    i = pl.program_id(0)

    @pl.when(i < n_prompt)
    def _():
        o_ref[...] = (x_ref[...] + _dot(tokp_ref[...], wt_ref[...])
                      + _dot(mop_ref[...], wm_ref[...]))

    @pl.when(i >= n_prompt)
    def _():
        o_ref[...] = (x_ref[...] + _dot(toks_ref[...], wt_ref[...])
                      + _dot(mos_ref[...], wm_ref[...]))


def _out_proj(x, tok_p, tok_s, mo_p, mo_s, w, layer):
    m, d = x.shape
    kt = tok_p.shape[1]
    km = mo_p.shape[1]
    tile = tok_s.shape[0]
    n_prompt = tok_p.shape[0] // tile
    assert n_prompt * tile == tok_p.shape[0] and (n_prompt + 1) * tile == m
    prompt = lambda i: (jnp.minimum(i, n_prompt - 1), 0)
    return pl.pallas_call(
        functools.partial(_out_proj_kernel, n_prompt=n_prompt),
        grid=(n_prompt + 1,),
        in_specs=[
            pl.BlockSpec((tile, d), lambda i: (i, 0)),
            pl.BlockSpec((tile, kt), prompt),
            pl.BlockSpec((tile, kt), lambda i: (0, 0)),
            pl.BlockSpec((tile, km), prompt),
            pl.BlockSpec((tile, km), lambda i: (0, 0)),
            pl.BlockSpec((None, kt, d), lambda i: (layer, 0, 0)),
            pl.BlockSpec((None, km, d), lambda i: (layer, kt // km, 0)),
        ],
        out_specs=pl.BlockSpec((tile, d), lambda i: (i, 0)),
        out_shape=jax.ShapeDtypeStruct((m, d), F32),
        compiler_params=_params(1),
        name="out_proj",
    )(x, tok_p, tok_s, mo_p, mo_s, w, w)


def _mem_kv_kernel(mem_ref, g_ref, wk_ref, wv_ref, kg_ref, mk_ref, mv_ref):
    h = _rms(mem_ref[...], g_ref[...]).astype(BF16)
    k = _dot(h, wk_ref[...].astype(BF16))
    v = _dot(h, wv_ref[...].astype(BF16))
    kg = kg_ref[...]
    for hd in range(MEM_HEADS):
        sl = slice(hd * MEM_HD, (hd + 1) * MEM_HD)
        mk_ref[hd] = _rms(k[:, sl], kg)
        mv_ref[hd] = v[:, sl]


def _mem_kv(mem, gain, wk, wv, k_gain):
    depth, d, w = wk.shape
    n = mem.shape[0]
    out_spec = pl.BlockSpec((None, MEM_HEADS, n, MEM_HD), lambda l: (l, 0, 0, 0))
    out_shape = jax.ShapeDtypeStruct((depth, MEM_HEADS, n, MEM_HD), F32)
    return pl.pallas_call(
        _mem_kv_kernel,
        grid=(depth,),
        in_specs=[
            pl.BlockSpec((n, d), lambda l: (0, 0)),
            pl.BlockSpec((None, 1, d), lambda l: (l, 0, 0)),
            pl.BlockSpec((None, d, w), lambda l: (l, 0, 0)),
            pl.BlockSpec((None, d, w), lambda l: (l, 0, 0)),
            pl.BlockSpec((None, 1, MEM_HD), lambda l: (l, 0, 0)),
        ],
        out_specs=(out_spec, out_spec),
        out_shape=(out_shape, out_shape),
        compiler_params=_params(1),
        name="mem_kv",
    )(mem, gain, wk, wv, k_gain)


def _mem_attn_heads(mq_ref, qg_ref, keys, values, masks, o_ref):
    heads = range(MEM_HEADS)
    cols = [slice(hd * MEM_HD, (hd + 1) * MEM_HD) for hd in heads]
    qg = qg_ref[...]
    qn = [_rms(mq_ref[:, cols[hd]], qg).astype(BF16) for hd in heads]
    s = [_dot_nt(qn[hd], keys[hd]) * (MEM_HD ** -0.5) for hd in heads]
    if masks is not None:
        s = [jnp.where(masks[hd], s[hd], -jnp.inf) for hd in heads]
    e = [jnp.exp(s[hd] - jnp.max(s[hd], axis=-1, keepdims=True)) for hd in heads]
    p = [(e[hd] / jnp.sum(e[hd], axis=-1, keepdims=True)).astype(BF16) for hd in heads]
    out = [_dot(p[hd], values[hd]) for hd in heads]
    for hd in heads:
        o_ref[:, cols[hd]] = out[hd].astype(BF16)


def _mem_attn_prompt_kernel(mq_ref, qg_ref, mk_ref, mv_ref, o_ref):
    _mem_attn_heads(mq_ref, qg_ref, [mk_ref[hd].astype(BF16) for hd in range(MEM_HEADS)],
                    [mv_ref[hd].astype(BF16) for hd in range(MEM_HEADS)], None, o_ref)


def _mem_attn_sample_kernel(mq_ref, qg_ref, mk_ref, mv_ref, o_ref):
    k16 = mk_ref[...].astype(BF16)
    v16 = mv_ref[...].astype(BF16)
    row_head = lax.broadcasted_iota(jnp.int32, (mq_ref.shape[0], k16.shape[0]), 1) % MEM_HEADS
    _mem_attn_heads(mq_ref, qg_ref, [k16] * MEM_HEADS, [v16] * MEM_HEADS,
                    [row_head == hd for hd in range(MEM_HEADS)], o_ref)


def _mem_attn(proj, col_block, row_block0, rows, n_tiles, q_gain, layer, mk, mv, kv_index0, per_tile_kv):
    w = MEM_HEADS * MEM_HD
    if per_tile_kv:
        body = _mem_attn_sample_kernel
        kv_spec = pl.BlockSpec((None,) + mk.shape[1:], lambda b: (kv_index0 + b, 0, 0))
    else:
        body = _mem_attn_prompt_kernel
        kv_spec = pl.BlockSpec((None,) + mk.shape[1:], lambda b: (kv_index0, 0, 0, 0))
    return pl.pallas_call(
        body,
        grid=(n_tiles,),
        in_specs=[
            pl.BlockSpec((rows, w), lambda b: (row_block0 + b, col_block)),
            pl.BlockSpec((None, 1, MEM_HD), lambda b: (layer, 0, 0)),
            kv_spec,
            kv_spec,
        ],
        out_specs=pl.BlockSpec((rows, w), lambda b: (b, 0)),
        out_shape=jax.ShapeDtypeStruct((n_tiles * rows, w), BF16),
        compiler_params=_params(1),
        name="mem_attn",
    )(proj, q_gain, mk, mv)


def _mlstm_kernel(q_ref, k_ref, v_ref, og_ref, gc_ref, gr_ref, bc_ref, br_ref, hn_ref,
                  c0_ref, n0_ref, m0_ref, tok_ref, c_ref, n_ref, m_ref, *, chunk):
    L = chunk

    @pl.when(pl.program_id(1) == 0)
    def _():
        c_ref[...] = c0_ref[...]
        n_ref[...] = n0_ref[...]
        m_ref[...] = m0_ref[...]

    row = lax.broadcasted_iota(jnp.int32, (L, L), 0)
    col = lax.broadcasted_iota(jnp.int32, (L, L), 1)
    causal = col <= row
    tri = jnp.where(causal, 1.0, 0.0).astype(BF16)

    pre_c = gc_ref[...] + bc_ref[...]
    pre_r = gr_ref[...] + br_ref[...]
    lf_c = _log_sigmoid(pre_c)
    lf_r = _log_sigmoid(pre_r)
    cum_c = sum(_dot(tri, part) for part in _split3(lf_c))
    cum_r = sum(_dot_nt(part, tri) for part in _split3(lf_r))

    scale = A_DQK ** -0.5
    heads = range(A_HEADS)
    qk = [slice(h * A_DQK, (h + 1) * A_DQK) for h in heads]
    vv = [slice(h * A_DV, (h + 1) * A_DV) for h in heads]
    q = [q_ref[:, qk[h]] * scale for h in heads]
    q16 = [q[h].astype(BF16) for h in heads]
    k16 = [k_ref[:, qk[h]].astype(BF16) for h in heads]
    c_old = [c_ref[h] for h in heads]
    n_old = [n_ref[h:h + 1, :] for h in heads]
    m_old = [m_ref[h:h + 1, 0:1] for h in heads]
    i_c = [pre_c[:, h:h + 1] for h in heads]
    b_c = [cum_c[:, A_HEADS + h:A_HEADS + h + 1] for h in heads]
    i_r = [pre_r[h:h + 1, :] for h in heads]
    b_r = [cum_r[A_HEADS + h:A_HEADS + h + 1, :] for h in heads]

    qk_t = [_dot_nt(q16[h], k16[h]) for h in heads]
    q_c = [_dot_nt(q16[h], c_old[h].astype(BF16)) for h in heads]

    s, w_st, m_t = [], [], []
    for h in heads:
        d = jnp.where(causal, b_c[h] - b_r[h] + i_r[h], -jnp.inf)
        inter = b_c[h] + m_old[h]
        m_t.append(jnp.maximum(inter, jnp.max(d, axis=-1, keepdims=True)))
        w_st.append(jnp.exp(inter - m_t[h]))
        s.append(qk_t[h] * jnp.exp(d - m_t[h]))
    s_v = [_dot(s[h].astype(BF16), v_ref[:, vv[h]].astype(BF16)) for h in heads]

    for h in heads:
        num = s_v[h] + w_st[h] * q_c[h]
        den = (jnp.sum(s[h], axis=-1, keepdims=True)
               + w_st[h] * jnp.sum(q[h] * n_old[h], axis=-1, keepdims=True))
        hh = num * (1.0 / jnp.maximum(jnp.abs(den), jnp.exp(-m_t[h])))
        out = _rms(hh, hn_ref[:, vv[h]]) * jax.nn.sigmoid(og_ref[:, vv[h]])
        tok_ref[:, vv[h]] = out.astype(BF16)

    decay, w_k = [], []
    for h in heads:
        b_end = b_c[h][L - 1:L, :]
        g = b_end - b_c[h] + i_c[h]
        m_new = jnp.maximum(b_end + m_old[h], jnp.max(g, axis=0, keepdims=True))
        w_k.append(jnp.exp(g - m_new))
        decay.append(jnp.exp(b_end + m_old[h] - m_new))
        m_ref[h:h + 1, :] = jnp.broadcast_to(m_new, (1, LANES))
    vw_k = [lax.dot_general((v_ref[:, vv[h]] * w_k[h]).astype(BF16), k16[h], TN_DIMS,
                            preferred_element_type=F32) for h in heads]
    for h in heads:
        c_ref[h] = decay[h] * c_old[h] + vw_k[h]
        n_ref[h:h + 1, :] = (decay[h] * n_old[h]
                             + jnp.sum(w_k[h] * k_ref[:, qk[h]], axis=0, keepdims=True))


def _mlstm(proj, gates_r, bias_c, bias_r, head_norm, c0, n0, m0, row_block0, chunk, n_chunks):
    n_seq = c0.shape[0]
    aq = A_HEADS * A_DQK
    av = A_HEADS * A_DV
    gate_block = (2 * aq + 2 * av + MEM_HEADS * MEM_HD) // LANES

    def rows(b, c):
        return row_block0 + b * n_chunks + c

    state = lambda b, c: (b, 0, 0)
    return pl.pallas_call(
        functools.partial(_mlstm_kernel, chunk=chunk),
        grid=(n_seq, n_chunks),
        in_specs=[
            pl.BlockSpec((chunk, aq), lambda b, c: (rows(b, c), 0)),
            pl.BlockSpec((chunk, aq), lambda b, c: (rows(b, c), 1)),
            pl.BlockSpec((chunk, av), lambda b, c: (rows(b, c), 1)),
            pl.BlockSpec((chunk, av), lambda b, c: (rows(b, c), 2)),
            pl.BlockSpec((chunk, LANES), lambda b, c: (rows(b, c), gate_block)),
            pl.BlockSpec((None, GATE_ROWS, chunk), lambda b, c: (b, 0, c)),
            pl.BlockSpec((1, LANES), lambda b, c: (0, 0)),
            pl.BlockSpec((GATE_ROWS, 1), lambda b, c: (0, 0)),
            pl.BlockSpec((1, av), lambda b, c: (0, 0)),
            pl.BlockSpec((None, A_HEADS, A_DV, A_DQK), lambda b, c: (b, 0, 0, 0)),
            pl.BlockSpec((None, SUBLANES, A_DQK), state),
            pl.BlockSpec((None, SUBLANES, LANES), state),
        ],
        out_specs=(
            pl.BlockSpec((chunk, av), lambda b, c: (b * n_chunks + c, 0)),
            pl.BlockSpec((None, A_HEADS, A_DV, A_DQK), lambda b, c: (b, 0, 0, 0)),
            pl.BlockSpec((None, SUBLANES, A_DQK), state),
            pl.BlockSpec((None, SUBLANES, LANES), state),
        ),
        out_shape=(
            jax.ShapeDtypeStruct((n_seq * n_chunks * chunk, av), BF16),
            jax.ShapeDtypeStruct((n_seq, A_HEADS, A_DV, A_DQK), F32),
            jax.ShapeDtypeStruct((n_seq, SUBLANES, A_DQK), F32),
            jax.ShapeDtypeStruct((n_seq, SUBLANES, LANES), F32),
        ),
        compiler_params=_params(2),
        name="mlstm",
    )(proj, proj, proj, proj, proj, gates_r, bias_c, bias_r, head_norm, c0, n0, m0)


def _sb_scores(q16, k16, valid):
    z = _dot_nt(q16, k16) * (SB_HD ** -0.5)
    sp = jnp.maximum(z, 0.0) + jnp.log(1.0 + jnp.exp(-jnp.abs(z)))
    if valid is not None:
        sp = jnp.where(valid, sp, 0.0)
    return z, sp


def _sb_newer(sp, upper):
    s1 = sp.astype(BF16)
    s2 = (sp - s1.astype(F32)).astype(BF16)
    return _dot(s1, upper) + _dot(s2, upper)


def _sb_weights(z, sp, newer, valid, r_prev):
    a = jnp.exp(z - sp - newer + r_prev)
    if valid is not None:
        a = jnp.where(valid, a, 0.0)
    return a.astype(BF16)


def _sb_tiles(qs, ks, vs, upper, valid, r_prevs):
    scores = [_sb_scores(q, k, valid) for q, k in zip(qs, ks)]
    newer = [_sb_newer(sp, upper) for _, sp in scores]
    outs = [_dot(_sb_weights(z, sp, nw, valid, r), v)
            for (z, sp), nw, r, v in zip(scores, newer, r_prevs, vs)]
    sums = [r - jnp.sum(sp, axis=-1, keepdims=True) for (_, sp), r in zip(scores, r_prevs)]
    return outs, sums


def _upper(n):
    row = lax.broadcasted_iota(jnp.int32, (n, n), 0)
    col = lax.broadcasted_iota(jnp.int32, (n, n), 1)
    return jnp.where(row > col, 1.0, 0.0).astype(BF16)


def _sb_prompt_kernel(q_ref, k_ref, v_ref, o_ref, q16_ref, acc_ref, r_ref):
    t = SB_TILE
    base = pl.program_id(1) * SB_CHAINS + SB_PAD // SB_TILE
    upper = _upper(t)
    row = lax.broadcasted_iota(jnp.int32, (t, t), 0)
    col = lax.broadcasted_iota(jnp.int32, (t, t), 1)
    q16_ref[...] = q_ref[...].astype(BF16)

    def walk(j, diagonal):
        rows = [slice(c * t, (c + 1) * t) for c in range(SB_CHAINS)]
        starts = [pl.multiple_of((base + c - j) * t, t) for c in range(SB_CHAINS)]
        outs, sums = _sb_tiles(
            [q16_ref[r, :] for r in rows],
            [k_ref[pl.ds(s, t), :] for s in starts],
            [v_ref[pl.ds(s, t), :] for s in starts],
            upper, col < row if diagonal else None,
            [jnp.zeros((t, 1), F32) if diagonal else r_ref[r, :] for r in rows])
        r_max = None
        for r, out, total in zip(rows, outs, sums):
            if diagonal:
                acc_ref[r, :] = out
            else:
                acc_ref[r, :] += out
            r_ref[r, :] = total
            r_c = jnp.max(total)
            r_max = r_c if r_max is None else jnp.maximum(r_max, r_c)
        return r_max

    def cond(carry):
        j, r_max = carry
        return jnp.logical_and(j <= base, r_max > EXP_ZERO_BELOW)

    def body(carry):
        j, _ = carry
        return j + 1, walk(j, False)

    lax.while_loop(cond, body, (jnp.int32(1), walk(0, True)))
    o_ref[...] = acc_ref[...].astype(BF16)


def _sb_prompt(proj, kv16, seq):
    step = SB_CHAINS * SB_TILE
    return pl.pallas_call(
        _sb_prompt_kernel,
        grid=(SB_HEADS, seq // step),
        in_specs=[
            pl.BlockSpec((step, SB_HD), lambda h, i: (i, h)),
            pl.BlockSpec((None, SB_PAD + seq, SB_HD), lambda h, i: (0, 0, h)),
            pl.BlockSpec((None, SB_PAD + seq, SB_HD), lambda h, i: (1, 0, h)),
        ],
        out_specs=pl.BlockSpec((step, SB_HD), lambda h, i: (i, h)),
        out_shape=jax.ShapeDtypeStruct((seq, SB_HEADS * SB_HD), BF16),
        scratch_shapes=[pltpu.VMEM((step, SB_HD), BF16), pltpu.VMEM((step, SB_HD), F32),
                        pltpu.VMEM((step, 1), F32)],
        compiler_params=_params(2),
        name="sb_prompt",
    )(proj, kv16, kv16)


def _sb_sample_kernel(q_ref, kn_ref, vn_ref, pk_hbm, pv_hbm, o_ref, kbuf, vbuf, sem, acc_ref, r_ref,
                      *, rows, n_past):
    b = pl.program_id(0)
    t = SB_PAST_TILE

    def copies(tile_index):
        start = pl.multiple_of(tile_index * t, t)
        return (
            pltpu.make_async_copy(pk_hbm.at[b, :, pl.ds(start, t), :], kbuf, sem.at[0]),
            pltpu.make_async_copy(pv_hbm.at[b, :, pl.ds(start, t), :], vbuf, sem.at[1]),
        )

    def fetch(tile_index):
        for cp in copies(tile_index):
            cp.start()

    def wait(tile_index):
        for cp in copies(tile_index):
            cp.wait()

    heads = [slice(h * SB_HD, (h + 1) * SB_HD) for h in range(SB_HEADS)]

    def past_tile():
        outs, sums = _sb_tiles(
            [q_ref[:, sl].astype(BF16) for sl in heads],
            [kbuf[h].astype(BF16) for h in range(SB_HEADS)],
            [vbuf[h].astype(BF16) for h in range(SB_HEADS)],
            _upper(t), None, [r_ref[h] for h in range(SB_HEADS)])
        r_max = None
        for h, (out, total) in enumerate(zip(outs, sums)):
            acc_ref[:, heads[h]] += out
            r_ref[h] = total
            r_h = jnp.max(total)
            r_max = r_h if r_max is None else jnp.maximum(r_max, r_h)
        return r_max

    fetch(n_past - 1)
    row = lax.broadcasted_iota(jnp.int32, (rows, rows), 0)
    col = lax.broadcasted_iota(jnp.int32, (rows, rows), 1)
    outs, sums = _sb_tiles(
        [q_ref[:, sl].astype(BF16) for sl in heads], [kn_ref[:, sl] for sl in heads],
        [vn_ref[:, sl] for sl in heads], _upper(rows), col < row,
        [jnp.zeros((rows, 1), F32)] * SB_HEADS)
    for h, (out, total) in enumerate(zip(outs, sums)):
        acc_ref[:, heads[h]] = out
        r_ref[h] = total
    wait(n_past - 1)
    r_max = past_tile()

    def cond(carry):
        tile_index, r_max = carry
        return jnp.logical_and(tile_index >= 0, r_max > EXP_ZERO_BELOW)

    def body(carry):
        tile_index, _ = carry
        fetch(tile_index)
        wait(tile_index)
        return tile_index - 1, past_tile()

    lax.while_loop(cond, body, (jnp.int32(n_past - 2), r_max))
    o_ref[...] = acc_ref[...].astype(BF16)


def _sb_sample(proj, kv16, past_k, past_v, row_block0, rows):
    n_seq, heads, past, hd = past_k.shape
    w = heads * hd
    new_block0 = row_block0 + SB_PAD // rows
    return pl.pallas_call(
        functools.partial(_sb_sample_kernel, rows=rows, n_past=past // SB_PAST_TILE),
        grid=(n_seq,),
        in_specs=[
            pl.BlockSpec((rows, w), lambda b: (row_block0 + b, 0)),
            pl.BlockSpec((None, rows, w), lambda b: (0, new_block0 + b, 0)),
            pl.BlockSpec((None, rows, w), lambda b: (1, new_block0 + b, 0)),
            pl.BlockSpec(memory_space=pl.ANY),
            pl.BlockSpec(memory_space=pl.ANY),
        ],
        out_specs=pl.BlockSpec((rows, w), lambda b: (b, 0)),
        out_shape=jax.ShapeDtypeStruct((n_seq * rows, w), BF16),
        scratch_shapes=[
            pltpu.VMEM((heads, SB_PAST_TILE, hd), F32),
            pltpu.VMEM((heads, SB_PAST_TILE, hd), F32),
            pltpu.SemaphoreType.DMA((2,)),
            pltpu.VMEM((rows, w), F32),
            pltpu.VMEM((heads, rows, 1), F32),
        ],
        compiler_params=_params(1),
        name="sb_sample",
    )(proj, kv16, kv16, past_k, past_v)


def _pad_rows(a, n):
    return jnp.pad(a, [(0, 0)] * (a.ndim - 2) + [(0, n - a.shape[-2]), (0, 0)])


def kernel(x_prompt, x_sample, state_mlstm_C, state_mlstm_n, state_mlstm_m, cache_sb_k, cache_sb_v,
           cache_mem_k, cache_mem_v, mem_prompt, ffn1_norm, ffn1_w_gate, ffn1_w_up, ffn1_w_down,
           ffn2_norm, ffn2_w_gate, ffn2_w_up, ffn2_w_down, mix_norm, a_w_in, a_b_i, a_b_f, a_head_norm,
           b_w_in, w_out, mem_norm, mem_w_k, mem_w_v, mem_q_norm, mem_k_norm, kv_norm, sb_w_k, sb_w_v):
    n_pb, seq, d = x_prompt.shape
    n_sb, dec_seq, _ = x_sample.shape
    assert n_pb == 1
    depth = ffn1_norm.shape[0]
    n_a = a_w_in.shape[0]
    n_slots = mem_prompt.shape[1]
    aq = A_HEADS * A_DQK
    av = A_HEADS * A_DV
    mem_w = MEM_HEADS * MEM_HD
    sb_w = SB_HEADS * SB_HD
    n_sample = n_sb * dec_seq
    sample_block0 = seq // dec_seq

    gains = lambda g: g.reshape(g.shape[0], 1, g.shape[-1])
    ffn1 = (gains(ffn1_norm), ffn1_w_gate, ffn1_w_up, ffn1_w_down)
    ffn2 = (gains(ffn2_norm), ffn2_w_gate, ffn2_w_up, ffn2_w_down)
    mix_gain = gains(mix_norm)
    n_main = 2 * aq + 2 * av
    w_gates = a_w_in[:, :, n_main:n_main + 2 * A_HEADS]
    a_cols = n_main + mem_w + LANES
    a_tile = 1792
    a_pad = -a_cols % a_tile
    a_w = jnp.concatenate(
        [a_w_in[:, :, :n_main], a_w_in[:, :, n_main + 2 * A_HEADS:], w_gates,
         jnp.zeros((n_a, d, LANES - 2 * A_HEADS + a_pad), F32)], axis=-1).astype(BF16)
    a_wgt = _pad_rows(jnp.swapaxes(w_gates, 1, 2), GATE_ROWS).astype(BF16)
    gate_bias = jnp.concatenate([a_b_i, a_b_f], axis=-1)
    b_w = b_w_in.astype(BF16)
    w_o = w_out.astype(BF16)
    w_kv = jnp.stack([sb_w_k, sb_w_v]).astype(BF16)
    q_gain = mem_q_norm.reshape(depth, 1, MEM_HD)

    mk_p, mv_p = _mem_kv(mem_prompt[0], gains(mem_norm), mem_w_k, mem_w_v,
                         mem_k_norm.reshape(depth, 1, MEM_HD))
    mk_s = cache_mem_k.reshape(depth * n_sb, n_slots * MEM_HEADS, MEM_HD)
    mv_s = cache_mem_v.reshape(depth * n_sb, n_slots * MEM_HEADS, MEM_HD)

    past_k = jnp.transpose(cache_sb_k, (0, 2, 1, 3))
    past_v = jnp.transpose(cache_sb_v, (0, 2, 1, 3))

    x = jnp.concatenate([x_prompt[0], x_sample.reshape(n_sample, d)], axis=0)
    c_p, n_p, m_p, c_s, n_s, m_s = [], [], [], [], [], []
    k_p = v_p = k_s = v_s = kv16 = None
    mem_tile = 1024
    for l in range(depth):
        x = _ffn(x, *ffn1, l)
        if l < n_a:
            proj, gates_r = _proj(x, mix_gain, l, a_w, l, a_tile, wgt=a_wgt)
            bias_c = jnp.pad(gate_bias[l], (0, LANES - 2 * A_HEADS)).reshape(1, LANES)
            bias_r = jnp.pad(gate_bias[l], (0, GATE_ROWS - 2 * A_HEADS)).reshape(GATE_ROWS, 1)
            head_norm = a_head_norm[l].reshape(1, av)
            chunk_p = 128
            tok_p, c, n, m = _mlstm(
                proj, gates_r[:, :seq].reshape(1, GATE_ROWS, seq), bias_c, bias_r, head_norm,
                jnp.zeros((1, A_HEADS, A_DV, A_DQK), F32), jnp.zeros((1, SUBLANES, A_DQK), F32),
                jnp.zeros((1, SUBLANES, LANES), F32), 0, chunk_p, seq // chunk_p)
            c_p.append(c); n_p.append(n[:, :A_HEADS]); m_p.append(m[:, :A_HEADS, 0])
            gates_s = gates_r[:, seq:].reshape(GATE_ROWS, n_sb, dec_seq).transpose(1, 0, 2)
            m0 = jnp.broadcast_to(state_mlstm_m[l][:, :, None], (n_sb, A_HEADS, LANES))
            tok_s, c, n, m = _mlstm(
                proj, gates_s, bias_c, bias_r, head_norm, state_mlstm_C[l],
                _pad_rows(state_mlstm_n[l], SUBLANES), _pad_rows(m0, SUBLANES), sample_block0, dec_seq, 1)
            c_s.append(c); n_s.append(n[:, :A_HEADS]); m_s.append(m[:, :A_HEADS, 0])
            mq_block = n_main // mem_w
        else:
            proj = _proj(x, mix_gain, l, b_w, l - n_a, b_w.shape[-1])
            tok_p = _sb_prompt(proj, kv16, seq)
            tok_s = _sb_sample(proj, kv16, past_k, past_v, sample_block0, dec_seq)
            mq_block = sb_w // mem_w
        mo_p = _mem_attn(proj, mq_block, 0, mem_tile, seq // mem_tile, q_gain, l, mk_p, mv_p, l, False)
        mo_s = _mem_attn(proj, mq_block, sample_block0, dec_seq, n_sb, q_gain, l, mk_s, mv_s, l * n_sb, True)
        x = _out_proj(x, tok_p, tok_s, mo_p, mo_s, w_o, l)
        x = _ffn(x, *ffn2, l)
        if l == n_a - 1:
            k_p, v_p, k_s, v_s, kv16 = _kv_proj(x, kv_norm.reshape(1, d), w_kv, seq, n_sb, dec_seq)

    y_prompt = x[:seq].reshape(1, seq, d)
    y_sample = x[seq:].reshape(n_sb, dec_seq, d)
    k_p, v_p = (jnp.transpose(a, (1, 0, 2))[None] for a in (k_p, v_p))
    k_s, v_s = (jnp.transpose(a, (0, 2, 1, 3)) for a in (k_s, v_s))
    mem_out = lambda a: jnp.transpose(a, (0, 2, 1, 3))[:, None]
    return (y_prompt, y_sample, jnp.stack(c_p), jnp.stack(n_p), jnp.stack(m_p), k_p, v_p,
            mem_out(mk_p), mem_out(mv_p),
            jnp.stack(c_s), jnp.stack(n_s), jnp.stack(m_s), k_s, v_s)
```

```python
import functools

import jax
import jax.numpy as jnp
from jax import lax
from jax.experimental import pallas as pl
from jax.experimental.pallas import tpu as pltpu

F32 = jnp.float32
BF16 = jnp.bfloat16

RMS_EPS = 1e-6
A_HEADS = 6
A_DQK = 128
A_DV = 256
SB_HEADS = 12
SB_HD = 128
MEM_HEADS = 4
MEM_HD = 128
LANES = 128
SUBLANES = 8
GATE_ROWS = 16

ROW_TILE = 768
FFN_ROW_TILE = 1056
FF_TILE = 512
FF_HEAD_TILE = 256
VMEM_LIMIT = 58 * 1024 * 1024

SB_TILE = 256
SB_CHAINS = 4
SB_PAST_TILE = 256
SB_PAD = 768

NT_DIMS = (((1,), (1,)), ((), ()))
TN_DIMS = (((0,), (0,)), ((), ()))
EXP_ZERO_BELOW = -104.0


def _params(n_axes):
    return pltpu.CompilerParams(
        dimension_semantics=("arbitrary",) * n_axes, vmem_limit_bytes=VMEM_LIMIT)


def _rms(x, g):
    ms = jnp.mean(x * x, axis=-1, keepdims=True)
    return x * lax.rsqrt(ms + RMS_EPS) * g


def _log_sigmoid(x):
    return jnp.minimum(x, 0.0) - jnp.log1p(jnp.exp(-jnp.abs(x)))


def _dot(a, b):
    return jnp.dot(a, b, preferred_element_type=F32)


def _dot_nt(a, b):
    return lax.dot_general(a, b, NT_DIMS, preferred_element_type=F32)


def _split3(x):
    x1 = x.astype(BF16)
    r = x - x1.astype(F32)
    x2 = r.astype(BF16)
    x3 = (r - x2.astype(F32)).astype(BF16)
    return x1, x2, x3


def _ffn_step(h_ref, wg, wu, wd, o_ref):
    h = h_ref[...]
    g = _dot(h, wg)
    u = _dot(h, wu)
    a = (g * jax.nn.sigmoid(g) * u * 0.5).astype(BF16)
    o_ref[...] += _dot(a, wd)


def _ffn_head_kernel(x_ref, g_ref, wg_ref, wu_ref, wd_ref, o_ref, wg16_ref, wu16_ref, wd16_ref, h_ref):
    @pl.when(pl.program_id(0) == 0)
    def _():
        x = x_ref[...]
        h_ref[...] = _rms(x, g_ref[...]).astype(BF16)
        o_ref[...] = x

    wg16_ref[...] = wg_ref[...].astype(BF16)
    wu16_ref[...] = wu_ref[...].astype(BF16)
    wd16_ref[...] = wd_ref[...].astype(BF16)
    _ffn_step(h_ref, wg16_ref[...], wu16_ref[...], wd16_ref[...], o_ref)


def _ffn_rest_kernel(x_ref, g_ref, wg_ref, wu_ref, wd_ref, o_ref, h_ref):
    @pl.when(pl.program_id(1) == 0)
    def _():
        x = x_ref[...]
        h_ref[...] = _rms(x, g_ref[...]).astype(BF16)
        o_ref[...] = x

    _ffn_step(h_ref, wg_ref[...], wu_ref[...], wd_ref[...], o_ref)


def _ffn(x, gain, wg, wu, wd, layer):
    m, d = x.shape
    f = wg.shape[-1]
    n_head = f // FF_HEAD_TILE
    w16 = lambda shape: jax.ShapeDtypeStruct(shape, BF16)
    x, wg16, wu16, wd16 = pl.pallas_call(
        _ffn_head_kernel,
        grid=(n_head,),
        in_specs=[
            pl.BlockSpec((FFN_ROW_TILE, d), lambda j: (0, 0)),
            pl.BlockSpec((None, 1, d), lambda j: (layer, 0, 0)),
            pl.BlockSpec((None, d, FF_HEAD_TILE), lambda j: (layer, 0, j)),
            pl.BlockSpec((None, d, FF_HEAD_TILE), lambda j: (layer, 0, j)),
            pl.BlockSpec((None, FF_HEAD_TILE, d), lambda j: (layer, j, 0)),
        ],
        out_specs=(
            pl.BlockSpec((FFN_ROW_TILE, d), lambda j: (0, 0)),
            pl.BlockSpec((d, FF_HEAD_TILE), lambda j: (0, j)),
            pl.BlockSpec((d, FF_HEAD_TILE), lambda j: (0, j)),
            pl.BlockSpec((FF_HEAD_TILE, d), lambda j: (j, 0)),
        ),
        out_shape=(jax.ShapeDtypeStruct((m, d), F32), w16((d, f)), w16((d, f)), w16((f, d))),
        scratch_shapes=[pltpu.VMEM((FFN_ROW_TILE, d), BF16)],
        input_output_aliases={0: 0},
        compiler_params=_params(1),
        name="ffn_head",
    )(x, gain, wg, wu, wd)
    return pl.pallas_call(
        _ffn_rest_kernel,
        grid=(m // FFN_ROW_TILE - 1, f // FF_TILE),
        in_specs=[
            pl.BlockSpec((FFN_ROW_TILE, d), lambda i, j: (i + 1, 0)),
            pl.BlockSpec((None, 1, d), lambda i, j: (layer, 0, 0)),
            pl.BlockSpec((d, FF_TILE), lambda i, j: (0, j)),
            pl.BlockSpec((d, FF_TILE), lambda i, j: (0, j)),
            pl.BlockSpec((FF_TILE, d), lambda i, j: (j, 0)),
        ],
        out_specs=pl.BlockSpec((FFN_ROW_TILE, d), lambda i, j: (i + 1, 0)),
        out_shape=jax.ShapeDtypeStruct((m, d), F32),
        scratch_shapes=[pltpu.VMEM((FFN_ROW_TILE, d), BF16)],
        input_output_aliases={0: 0},
        compiler_params=_params(2),
        name="ffn",
    )(x, gain, wg16, wu16, wd16)


def _proj_kernel(x_ref, g_ref, w_ref, o_ref, h_ref):
    @pl.when(pl.program_id(1) == 0)
    def _():
        h_ref[...] = _rms(x_ref[...], g_ref[...]).astype(BF16)

    o_ref[...] = _dot(h_ref[...], w_ref[...]).astype(o_ref.dtype)


def _proj_gates_kernel(x_ref, g_ref, w_ref, wgt_ref, o_ref, gt_ref, h_ref):
    @pl.when(pl.program_id(1) == 0)
    def _():
        h = _rms(x_ref[...], g_ref[...]).astype(BF16)
        h_ref[...] = h
        gt_ref[...] = _dot_nt(wgt_ref[...], h)

    o_ref[...] = _dot(h_ref[...], w_ref[...]).astype(o_ref.dtype)


def _proj(x, gain, gain_layer, w, layer, col_tile, wgt=None):
    m, d = x.shape
    n = w.shape[-1]
    in_specs = [
        pl.BlockSpec((ROW_TILE, d), lambda i, j: (i, 0)),
        pl.BlockSpec((None, 1, d), lambda i, j: (gain_layer, 0, 0)),
        pl.BlockSpec((None, d, col_tile), lambda i, j: (layer, 0, j)),
    ]
    out_spec = pl.BlockSpec((ROW_TILE, col_tile), lambda i, j: (i, j))
    out_shape = jax.ShapeDtypeStruct((m, n), F32)
    common = dict(
        grid=(m // ROW_TILE, n // col_tile),
        scratch_shapes=[pltpu.VMEM((ROW_TILE, d), BF16)],
        compiler_params=_params(2),
    )
    if wgt is None:
        return pl.pallas_call(
            _proj_kernel, in_specs=in_specs, out_specs=out_spec, out_shape=out_shape,
            name="proj", **common)(x, gain, w)
    return pl.pallas_call(
        _proj_gates_kernel,
        in_specs=in_specs + [pl.BlockSpec((None, GATE_ROWS, d), lambda i, j: (layer, 0, 0))],
        out_specs=(out_spec, pl.BlockSpec((GATE_ROWS, ROW_TILE), lambda i, j: (0, i))),
        out_shape=(out_shape, jax.ShapeDtypeStruct((GATE_ROWS, m), F32)),
        name="proj_gates", **common)(x, gain, w, wgt)


def _kv_kernel(x_ref, g_ref, w_ref, pad_ref, kp_ref, vp_ref, ks_ref, vs_ref, o16_ref,
               *, n_prompt, n_seq, rows):
    del pad_ref
    i = pl.program_id(0)
    h = _rms(x_ref[...], g_ref[...]).astype(BF16)
    heads = [slice(hd * SB_HD, (hd + 1) * SB_HD) for hd in range(SB_HEADS)]
    for j, (p_ref, s_ref) in enumerate(((kp_ref, ks_ref), (vp_ref, vs_ref))):
        y = _dot(h, w_ref[j])
        o16_ref[j] = y.astype(BF16)

        @pl.when(i < n_prompt)
        def _():
            for hd, sl in enumerate(heads):
                p_ref[hd] = y[:, sl]

        @pl.when(i >= n_prompt)
        def _():
            for b in range(n_seq):
                for hd, sl in enumerate(heads):
                    s_ref[b, hd] = y[b * rows:(b + 1) * rows, sl]


def _kv_proj(x, gain, w, seq, n_seq, rows):
    m, d = x.shape
    n = w.shape[-1]
    tile = n_seq * rows
    n_prompt = seq // tile
    assert n_prompt * tile == seq and seq + tile == m and SB_PAD % tile == 0
    prompt_spec = pl.BlockSpec((SB_HEADS, tile, SB_HD), lambda i: (0, jnp.minimum(i, n_prompt - 1), 0))
    sample_spec = pl.BlockSpec((n_seq, SB_HEADS, rows, SB_HD), lambda i: (0, 0, 0, 0))
    prompt_shape = jax.ShapeDtypeStruct((SB_HEADS, seq, SB_HD), F32)
    sample_shape = jax.ShapeDtypeStruct((n_seq, SB_HEADS, rows, SB_HD), F32)
    return pl.pallas_call(
        functools.partial(_kv_kernel, n_prompt=n_prompt, n_seq=n_seq, rows=rows),
        grid=(n_prompt + 1,),
        in_specs=[
            pl.BlockSpec((tile, d), lambda i: (i, 0)),
            pl.BlockSpec((1, d), lambda i: (0, 0)),
            pl.BlockSpec((2, d, n), lambda i: (0, 0, 0)),
            pl.BlockSpec(memory_space=pl.ANY),
        ],
        out_specs=(prompt_spec, prompt_spec, sample_spec, sample_spec,
                   pl.BlockSpec((2, tile, n), lambda i: (0, i + SB_PAD // tile, 0))),
        out_shape=(prompt_shape, prompt_shape, sample_shape, sample_shape,
                   jax.ShapeDtypeStruct((2, SB_PAD + m, n), BF16)),
        input_output_aliases={3: 4},
        compiler_params=_params(1),
        name="kv_proj",
    )(x, gain, w, jnp.zeros((2, SB_PAD + m, n), BF16))


def _out_proj_kernel(x_ref, tokp_ref, toks_ref, mop_ref, mos_ref, wt_ref, wm_ref, o_ref, *, n_prompt):
    i = pl.program_id(0)

    @pl.when(i < n_prompt)
    def _():
        o_ref[...] = (x_ref[...] + _dot(tokp_ref[...], wt_ref[...])
                      + _dot(mop_ref[...], wm_ref[...]))

    @pl.when(i >= n_prompt)
    def _():
        o_ref[...] = (x_ref[...] + _dot(toks_ref[...], wt_ref[...])
                      + _dot(mos_ref[...], wm_ref[...]))


def _out_proj(x, tok_p, tok_s, mo_p, mo_s, w, layer):
    m, d = x.shape
    kt = tok_p.shape[1]
    km = mo_p.shape[1]
    tile = tok_s.shape[0]
    n_prompt = tok_p.shape[0] // tile
    assert n_prompt * tile == tok_p.shape[0] and (n_prompt + 1) * tile == m
    prompt = lambda i: (jnp.minimum(i, n_prompt - 1), 0)
    return pl.pallas_call(
        functools.partial(_out_proj_kernel, n_prompt=n_prompt),
        grid=(n_prompt + 1,),
        in_specs=[
            pl.BlockSpec((tile, d), lambda i: (i, 0)),
            pl.BlockSpec((tile, kt), prompt),
            pl.BlockSpec((tile, kt), lambda i: (0, 0)),
            pl.BlockSpec((tile, km), prompt),
            pl.BlockSpec((tile, km), lambda i: (0, 0)),
            pl.BlockSpec((None, kt, d), lambda i: (layer, 0, 0)),
            pl.BlockSpec((None, km, d), lambda i: (layer, kt // km, 0)),
        ],
        out_specs=pl.BlockSpec((tile, d), lambda i: (i, 0)),
        out_shape=jax.ShapeDtypeStruct((m, d), F32),
        compiler_params=_params(1),
        name="out_proj",
    )(x, tok_p, tok_s, mo_p, mo_s, w, w)


def _mem_kv_kernel(mem_ref, g_ref, wk_ref, wv_ref, kg_ref, mk_ref, mv_ref):
    h = _rms(mem_ref[...], g_ref[...]).astype(BF16)
    k = _dot(h, wk_ref[...].astype(BF16))
    v = _dot(h, wv_ref[...].astype(BF16))
    kg = kg_ref[...]
    for hd in range(MEM_HEADS):
        sl = slice(hd * MEM_HD, (hd + 1) * MEM_HD)
        mk_ref[hd] = _rms(k[:, sl], kg)
        mv_ref[hd] = v[:, sl]


def _mem_kv(mem, gain, wk, wv, k_gain):
    depth, d, w = wk.shape
    n = mem.shape[0]
    out_spec = pl.BlockSpec((None, MEM_HEADS, n, MEM_HD), lambda l: (l, 0, 0, 0))
    out_shape = jax.ShapeDtypeStruct((depth, MEM_HEADS, n, MEM_HD), F32)
    return pl.pallas_call(
        _mem_kv_kernel,
        grid=(depth,),
        in_specs=[
            pl.BlockSpec((n, d), lambda l: (0, 0)),
            pl.BlockSpec((None, 1, d), lambda l: (l, 0, 0)),
            pl.BlockSpec((None, d, w), lambda l: (l, 0, 0)),
            pl.BlockSpec((None, d, w), lambda l: (l, 0, 0)),
            pl.BlockSpec((None, 1, MEM_HD), lambda l: (l, 0, 0)),
        ],
        out_specs=(out_spec, out_spec),
        out_shape=(out_shape, out_shape),
        compiler_params=_params(1),
        name="mem_kv",
    )(mem, gain, wk, wv, k_gain)


def _mem_attn_heads(mq_ref, qg_ref, keys, values, masks, o_ref):
    heads = range(MEM_HEADS)
    cols = [slice(hd * MEM_HD, (hd + 1) * MEM_HD) for hd in heads]
    qg = qg_ref[...]
    qn = [_rms(mq_ref[:, cols[hd]], qg).astype(BF16) for hd in heads]
    s = [_dot_nt(qn[hd], keys[hd]) * (MEM_HD ** -0.5) for hd in heads]
    if masks is not None:
        s = [jnp.where(masks[hd], s[hd], -jnp.inf) for hd in heads]
    e = [jnp.exp(s[hd] - jnp.max(s[hd], axis=-1, keepdims=True)) for hd in heads]
    p = [(e[hd] / jnp.sum(e[hd], axis=-1, keepdims=True)).astype(BF16) for hd in heads]
    out = [_dot(p[hd], values[hd]) for hd in heads]
    for hd in heads:
        o_ref[:, cols[hd]] = out[hd].astype(BF16)


def _mem_attn_prompt_kernel(mq_ref, qg_ref, mk_ref, mv_ref, o_ref):
    _mem_attn_heads(mq_ref, qg_ref, [mk_ref[hd].astype(BF16) for hd in range(MEM_HEADS)],
                    [mv_ref[hd].astype(BF16) for hd in range(MEM_HEADS)], None, o_ref)


def _mem_attn_sample_kernel(mq_ref, qg_ref, mk_ref, mv_ref, o_ref):
    k16 = mk_ref[...].astype(BF16)
    v16 = mv_ref[...].astype(BF16)
    row_head = lax.broadcasted_iota(jnp.int32, (mq_ref.shape[0], k16.shape[0]), 1) % MEM_HEADS
    _mem_attn_heads(mq_ref, qg_ref, [k16] * MEM_HEADS, [v16] * MEM_HEADS,
                    [row_head == hd for hd in range(MEM_HEADS)], o_ref)


def _mem_attn(proj, col_block, row_block0, rows, n_tiles, q_gain, layer, mk, mv, kv_index0, per_tile_kv):
    w = MEM_HEADS * MEM_HD
    if per_tile_kv:
        body = _mem_attn_sample_kernel
        kv_spec = pl.BlockSpec((None,) + mk.shape[1:], lambda b: (kv_index0 + b, 0, 0))
    else:
        body = _mem_attn_prompt_kernel
        kv_spec = pl.BlockSpec((None,) + mk.shape[1:], lambda b: (kv_index0, 0, 0, 0))
    return pl.pallas_call(
        body,
        grid=(n_tiles,),
        in_specs=[
            pl.BlockSpec((rows, w), lambda b: (row_block0 + b, col_block)),
            pl.BlockSpec((None, 1, MEM_HD), lambda b: (layer, 0, 0)),
            kv_spec,
            kv_spec,
        ],
        out_specs=pl.BlockSpec((rows, w), lambda b: (b, 0)),
        out_shape=jax.ShapeDtypeStruct((n_tiles * rows, w), BF16),
        compiler_params=_params(1),
        name="mem_attn",
    )(proj, q_gain, mk, mv)


def _mlstm_kernel(q_ref, k_ref, v_ref, og_ref, gc_ref, gr_ref, bc_ref, br_ref, hn_ref,
                  c0_ref, n0_ref, m0_ref, tok_ref, c_ref, n_ref, m_ref, *, chunk):
    L = chunk

    @pl.when(pl.program_id(1) == 0)
    def _():
        c_ref[...] = c0_ref[...]
        n_ref[...] = n0_ref[...]
        m_ref[...] = m0_ref[...]

    row = lax.broadcasted_iota(jnp.int32, (L, L), 0)
    col = lax.broadcasted_iota(jnp.int32, (L, L), 1)
    causal = col <= row
    tri = jnp.where(causal, 1.0, 0.0).astype(BF16)

    pre_c = gc_ref[...] + bc_ref[...]
    pre_r = gr_ref[...] + br_ref[...]
    lf_c = _log_sigmoid(pre_c)
    lf_r = _log_sigmoid(pre_r)
    cum_c = sum(_dot(tri, part) for part in _split3(lf_c))
    cum_r = sum(_dot_nt(part, tri) for part in _split3(lf_r))

    scale = A_DQK ** -0.5
    heads = range(A_HEADS)
    qk = [slice(h * A_DQK, (h + 1) * A_DQK) for h in heads]
    vv = [slice(h * A_DV, (h + 1) * A_DV) for h in heads]
    q = [q_ref[:, qk[h]] * scale for h in heads]
    q16 = [q[h].astype(BF16) for h in heads]
    k16 = [k_ref[:, qk[h]].astype(BF16) for h in heads]
    c_old = [c_ref[h] for h in heads]
    n_old = [n_ref[h:h + 1, :] for h in heads]
    m_old = [m_ref[h:h + 1, 0:1] for h in heads]
    i_c = [pre_c[:, h:h + 1] for h in heads]
    b_c = [cum_c[:, A_HEADS + h:A_HEADS + h + 1] for h in heads]
    i_r = [pre_r[h:h + 1, :] for h in heads]
    b_r = [cum_r[A_HEADS + h:A_HEADS + h + 1, :] for h in heads]

    qk_t = [_dot_nt(q16[h], k16[h]) for h in heads]
    q_c = [_dot_nt(q16[h], c_old[h].astype(BF16)) for h in heads]

    s, w_st, m_t = [], [], []
    for h in heads:
        d = jnp.where(causal, b_c[h] - b_r[h] + i_r[h], -jnp.inf)
        inter = b_c[h] + m_old[h]
        m_t.append(jnp.maximum(inter, jnp.max(d, axis=-1, keepdims=True)))
        w_st.append(jnp.exp(inter - m_t[h]))
        s.append(qk_t[h] * jnp.exp(d - m_t[h]))
    s_v = [_dot(s[h].astype(BF16), v_ref[:, vv[h]].astype(BF16)) for h in heads]

    for h in heads:
        num = s_v[h] + w_st[h] * q_c[h]
        den = (jnp.sum(s[h], axis=-1, keepdims=True)
               + w_st[h] * jnp.sum(q[h] * n_old[h], axis=-1, keepdims=True))
        hh = num * (1.0 / jnp.maximum(jnp.abs(den), jnp.exp(-m_t[h])))
        out = _rms(hh, hn_ref[:, vv[h]]) * jax.nn.sigmoid(og_ref[:, vv[h]])
        tok_ref[:, vv[h]] = out.astype(BF16)

    decay, w_k = [], []
    for h in heads:
        b_end = b_c[h][L - 1:L, :]
        g = b_end - b_c[h] + i_c[h]
        m_new = jnp.maximum(b_end + m_old[h], jnp.max(g, axis=0, keepdims=True))
        w_k.append(jnp.exp(g - m_new))
        decay.append(jnp.exp(b_end + m_old[h] - m_new))
        m_ref[h:h + 1, :] = jnp.broadcast_to(m_new, (1, LANES))
    vw_k = [lax.dot_general((v_ref[:, vv[h]] * w_k[h]).astype(BF16), k16[h], TN_DIMS,
                            preferred_element_type=F32) for h in heads]
    for h in heads:
        c_ref[h] = decay[h] * c_old[h] + vw_k[h]
        n_ref[h:h + 1, :] = (decay[h] * n_old[h]
                             + jnp.sum(w_k[h] * k_ref[:, qk[h]], axis=0, keepdims=True))


def _mlstm(proj, gates_r, bias_c, bias_r, head_norm, c0, n0, m0, row_block0, chunk, n_chunks):
    n_seq = c0.shape[0]
    aq = A_HEADS * A_DQK
    av = A_HEADS * A_DV
    gate_block = (2 * aq + 2 * av + MEM_HEADS * MEM_HD) // LANES

    def rows(b, c):
        return row_block0 + b * n_chunks + c

    state = lambda b, c: (b, 0, 0)
    return pl.pallas_call(
        functools.partial(_mlstm_kernel, chunk=chunk),
        grid=(n_seq, n_chunks),
        in_specs=[
            pl.BlockSpec((chunk, aq), lambda b, c: (rows(b, c), 0)),
            pl.BlockSpec((chunk, aq), lambda b, c: (rows(b, c), 1)),
            pl.BlockSpec((chunk, av), lambda b, c: (rows(b, c), 1)),
            pl.BlockSpec((chunk, av), lambda b, c: (rows(b, c), 2)),
            pl.BlockSpec((chunk, LANES), lambda b, c: (rows(b, c), gate_block)),
            pl.BlockSpec((None, GATE_ROWS, chunk), lambda b, c: (b, 0, c)),
            pl.BlockSpec((1, LANES), lambda b, c: (0, 0)),
            pl.BlockSpec((GATE_ROWS, 1), lambda b, c: (0, 0)),
            pl.BlockSpec((1, av), lambda b, c: (0, 0)),
            pl.BlockSpec((None, A_HEADS, A_DV, A_DQK), lambda b, c: (b, 0, 0, 0)),
            pl.BlockSpec((None, SUBLANES, A_DQK), state),
            pl.BlockSpec((None, SUBLANES, LANES), state),
        ],
        out_specs=(
            pl.BlockSpec((chunk, av), lambda b, c: (b * n_chunks + c, 0)),
            pl.BlockSpec((None, A_HEADS, A_DV, A_DQK), lambda b, c: (b, 0, 0, 0)),
            pl.BlockSpec((None, SUBLANES, A_DQK), state),
            pl.BlockSpec((None, SUBLANES, LANES), state),
        ),
        out_shape=(
            jax.ShapeDtypeStruct((n_seq * n_chunks * chunk, av), BF16),
            jax.ShapeDtypeStruct((n_seq, A_HEADS, A_DV, A_DQK), F32),
            jax.ShapeDtypeStruct((n_seq, SUBLANES, A_DQK), F32),
            jax.ShapeDtypeStruct((n_seq, SUBLANES, LANES), F32),
        ),
        compiler_params=_params(2),
        name="mlstm",
    )(proj, proj, proj, proj, proj, gates_r, bias_c, bias_r, head_norm, c0, n0, m0)


def _sb_scores(q16, k16, valid):
    z = _dot_nt(q16, k16) * (SB_HD ** -0.5)
    sp = jnp.maximum(z, 0.0) + jnp.log(1.0 + jnp.exp(-jnp.abs(z)))
    if valid is not None:
        sp = jnp.where(valid, sp, 0.0)
    return z, sp


def _sb_newer(sp, upper):
    s1 = sp.astype(BF16)
    s2 = (sp - s1.astype(F32)).astype(BF16)
    return _dot(s1, upper) + _dot(s2, upper)


def _sb_weights(z, sp, newer, valid, r_prev):
    a = jnp.exp(z - sp - newer + r_prev)
    if valid is not None:
        a = jnp.where(valid, a, 0.0)
    return a.astype(BF16)


def _sb_tiles(qs, ks, vs, upper, valid, r_prevs):
    scores = [_sb_scores(q, k, valid) for q, k in zip(qs, ks)]
    newer = [_sb_newer(sp, upper) for _, sp in scores]
    outs = [_dot(_sb_weights(z, sp, nw, valid, r), v)
            for (z, sp), nw, r, v in zip(scores, newer, r_prevs, vs)]
    sums = [r - jnp.sum(sp, axis=-1, keepdims=True) for (_, sp), r in zip(scores, r_prevs)]
    return outs, sums


def _upper(n):
    row = lax.broadcasted_iota(jnp.int32, (n, n), 0)
    col = lax.broadcasted_iota(jnp.int32, (n, n), 1)
    return jnp.where(row > col, 1.0, 0.0).astype(BF16)


def _sb_prompt_kernel(q_ref, k_ref, v_ref, o_ref, q16_ref, acc_ref, r_ref):
    t = SB_TILE
    base = pl.program_id(1) * SB_CHAINS + SB_PAD // SB_TILE
    upper = _upper(t)
    row = lax.broadcasted_iota(jnp.int32, (t, t), 0)
    col = lax.broadcasted_iota(jnp.int32, (t, t), 1)
    q16_ref[...] = q_ref[...].astype(BF16)

    def walk(j, diagonal):
        rows = [slice(c * t, (c + 1) * t) for c in range(SB_CHAINS)]
        starts = [pl.multiple_of((base + c - j) * t, t) for c in range(SB_CHAINS)]
        outs, sums = _sb_tiles(
            [q16_ref[r, :] for r in rows],
            [k_ref[pl.ds(s, t), :] for s in starts],
            [v_ref[pl.ds(s, t), :] for s in starts],
            upper, col < row if diagonal else None,
            [jnp.zeros((t, 1), F32) if diagonal else r_ref[r, :] for r in rows])
        r_max = None
        for r, out, total in zip(rows, outs, sums):
            if diagonal:
                acc_ref[r, :] = out
            else:
                acc_ref[r, :] += out
            r_ref[r, :] = total
            r_c = jnp.max(total)
            r_max = r_c if r_max is None else jnp.maximum(r_max, r_c)
        return r_max

    def cond(carry):
        j, r_max = carry
        return jnp.logical_and(j <= base, r_max > EXP_ZERO_BELOW)

    def body(carry):
        j, _ = carry
        return j + 1, walk(j, False)

    lax.while_loop(cond, body, (jnp.int32(1), walk(0, True)))
    o_ref[...] = acc_ref[...].astype(BF16)


def _sb_prompt(proj, kv16, seq):
    step = SB_CHAINS * SB_TILE
    return pl.pallas_call(
        _sb_prompt_kernel,
        grid=(SB_HEADS, seq // step),
        in_specs=[
            pl.BlockSpec((step, SB_HD), lambda h, i: (i, h)),
            pl.BlockSpec((None, SB_PAD + seq, SB_HD), lambda h, i: (0, 0, h)),
            pl.BlockSpec((None, SB_PAD + seq, SB_HD), lambda h, i: (1, 0, h)),
        ],
        out_specs=pl.BlockSpec((step, SB_HD), lambda h, i: (i, h)),
        out_shape=jax.ShapeDtypeStruct((seq, SB_HEADS * SB_HD), BF16),
        scratch_shapes=[pltpu.VMEM((step, SB_HD), BF16), pltpu.VMEM((step, SB_HD), F32),
                        pltpu.VMEM((step, 1), F32)],
        compiler_params=_params(2),
        name="sb_prompt",
    )(proj, kv16, kv16)


def _sb_sample_kernel(q_ref, kn_ref, vn_ref, pk_hbm, pv_hbm, o_ref, kbuf, vbuf, sem, acc_ref, r_ref,
                      *, rows, n_past):
    b = pl.program_id(0)
    t = SB_PAST_TILE

    def copies(tile_index):
        start = pl.multiple_of(tile_index * t, t)
        return (
            pltpu.make_async_copy(pk_hbm.at[b, :, pl.ds(start, t), :], kbuf, sem.at[0]),
            pltpu.make_async_copy(pv_hbm.at[b, :, pl.ds(start, t), :], vbuf, sem.at[1]),
        )

    def fetch(tile_index):
        for cp in copies(tile_index):
            cp.start()

    def wait(tile_index):
        for cp in copies(tile_index):
            cp.wait()

    heads = [slice(h * SB_HD, (h + 1) * SB_HD) for h in range(SB_HEADS)]

    def past_tile():
        outs, sums = _sb_tiles(
            [q_ref[:, sl].astype(BF16) for sl in heads],
            [kbuf[h].astype(BF16) for h in range(SB_HEADS)],
            [vbuf[h].astype(BF16) for h in range(SB_HEADS)],
            _upper(t), None, [r_ref[h] for h in range(SB_HEADS)])
        r_max = None
        for h, (out, total) in enumerate(zip(outs, sums)):
            acc_ref[:, heads[h]] += out
            r_ref[h] = total
            r_h = jnp.max(total)
            r_max = r_h if r_max is None else jnp.maximum(r_max, r_h)
        return r_max

    fetch(n_past - 1)
    row = lax.broadcasted_iota(jnp.int32, (rows, rows), 0)
    col = lax.broadcasted_iota(jnp.int32, (rows, rows), 1)
    outs, sums = _sb_tiles(
        [q_ref[:, sl].astype(BF16) for sl in heads], [kn_ref[:, sl] for sl in heads],
        [vn_ref[:, sl] for sl in heads], _upper(rows), col < row,
        [jnp.zeros((rows, 1), F32)] * SB_HEADS)
    for h, (out, total) in enumerate(zip(outs, sums)):
        acc_ref[:, heads[h]] = out
        r_ref[h] = total
    wait(n_past - 1)
    r_max = past_tile()

    def cond(carry):
        tile_index, r_max = carry
        return jnp.logical_and(tile_index >= 0, r_max > EXP_ZERO_BELOW)

    def body(carry):
        tile_index, _ = carry
        fetch(tile_index)
        wait(tile_index)
        return tile_index - 1, past_tile()

    lax.while_loop(cond, body, (jnp.int32(n_past - 2), r_max))
    o_ref[...] = acc_ref[...].astype(BF16)


def _sb_sample(proj, kv16, past_k, past_v, row_block0, rows):
    n_seq, heads, past, hd = past_k.shape
    w = heads * hd
    new_block0 = row_block0 + SB_PAD // rows
    return pl.pallas_call(
        functools.partial(_sb_sample_kernel, rows=rows, n_past=past // SB_PAST_TILE),
        grid=(n_seq,),
        in_specs=[
            pl.BlockSpec((rows, w), lambda b: (row_block0 + b, 0)),
            pl.BlockSpec((None, rows, w), lambda b: (0, new_block0 + b, 0)),
            pl.BlockSpec((None, rows, w), lambda b: (1, new_block0 + b, 0)),
            pl.BlockSpec(memory_space=pl.ANY),
            pl.BlockSpec(memory_space=pl.ANY),
        ],
        out_specs=pl.BlockSpec((rows, w), lambda b: (b, 0)),
        out_shape=jax.ShapeDtypeStruct((n_seq * rows, w), BF16),
        scratch_shapes=[
            pltpu.VMEM((heads, SB_PAST_TILE, hd), F32),
            pltpu.VMEM((heads, SB_PAST_TILE, hd), F32),
            pltpu.SemaphoreType.DMA((2,)),
            pltpu.VMEM((rows, w), F32),
            pltpu.VMEM((heads, rows, 1), F32),
        ],
        compiler_params=_params(1),
        name="sb_sample",
    )(proj, kv16, kv16, past_k, past_v)


def _pad_rows(a, n):
    return jnp.pad(a, [(0, 0)] * (a.ndim - 2) + [(0, n - a.shape[-2]), (0, 0)])


def kernel(x_prompt, x_sample, state_mlstm_C, state_mlstm_n, state_mlstm_m, cache_sb_k, cache_sb_v,
           cache_mem_k, cache_mem_v, mem_prompt, ffn1_norm, ffn1_w_gate, ffn1_w_up, ffn1_w_down,
           ffn2_norm, ffn2_w_gate, ffn2_w_up, ffn2_w_down, mix_norm, a_w_in, a_b_i, a_b_f, a_head_norm,
           b_w_in, w_out, mem_norm, mem_w_k, mem_w_v, mem_q_norm, mem_k_norm, kv_norm, sb_w_k, sb_w_v):
    n_pb, seq, d = x_prompt.shape
    n_sb, dec_seq, _ = x_sample.shape
    assert n_pb == 1
    depth = ffn1_norm.shape[0]
    n_a = a_w_in.shape[0]
    n_slots = mem_prompt.shape[1]
    aq = A_HEADS * A_DQK
    av = A_HEADS * A_DV
    mem_w = MEM_HEADS * MEM_HD
    sb_w = SB_HEADS * SB_HD
    n_sample = n_sb * dec_seq
    sample_block0 = seq // dec_seq

    gains = lambda g: g.reshape(g.shape[0], 1, g.shape[-1])
    ffn1 = (gains(ffn1_norm), ffn1_w_gate, ffn1_w_up, ffn1_w_down)
    ffn2 = (gains(ffn2_norm), ffn2_w_gate, ffn2_w_up, ffn2_w_down)
    mix_gain = gains(mix_norm)
    n_main = 2 * aq + 2 * av
    w_gates = a_w_in[:, :, n_main:n_main + 2 * A_HEADS]
    a_cols = n_main + mem_w + LANES
    a_tile = 1792
    a_pad = -a_cols % a_tile
    a_w = jnp.concatenate(
        [a_w_in[:, :, :n_main], a_w_in[:, :, n_main + 2 * A_HEADS:], w_gates,
         jnp.zeros((n_a, d, LANES - 2 * A_HEADS + a_pad), F32)], axis=-1).astype(BF16)
    a_wgt = _pad_rows(jnp.swapaxes(w_gates, 1, 2), GATE_ROWS).astype(BF16)
    gate_bias = jnp.concatenate([a_b_i, a_b_f], axis=-1)
    b_w = b_w_in.astype(BF16)
    w_o = w_out.astype(BF16)
    w_kv = jnp.stack([sb_w_k, sb_w_v]).astype(BF16)
    q_gain = mem_q_norm.reshape(depth, 1, MEM_HD)

    mk_p, mv_p = _mem_kv(mem_prompt[0], gains(mem_norm), mem_w_k, mem_w_v,
                         mem_k_norm.reshape(depth, 1, MEM_HD))
    mk_s = cache_mem_k.reshape(depth * n_sb, n_slots * MEM_HEADS, MEM_HD)
    mv_s = cache_mem_v.reshape(depth * n_sb, n_slots * MEM_HEADS, MEM_HD)

    past_k = jnp.transpose(cache_sb_k, (0, 2, 1, 3))
    past_v = jnp.transpose(cache_sb_v, (0, 2, 1, 3))

    x = jnp.concatenate([x_prompt[0], x_sample.reshape(n_sample, d)], axis=0)
    c_p, n_p, m_p, c_s, n_s, m_s = [], [], [], [], [], []
    k_p = v_p = k_s = v_s = kv16 = None
    mem_tile = 1024
    for l in range(depth):
        x = _ffn(x, *ffn1, l)
        if l < n_a:
            proj, gates_r = _proj(x, mix_gain, l, a_w, l, a_tile, wgt=a_wgt)
            bias_c = jnp.pad(gate_bias[l], (0, LANES - 2 * A_HEADS)).reshape(1, LANES)
            bias_r = jnp.pad(gate_bias[l], (0, GATE_ROWS - 2 * A_HEADS)).reshape(GATE_ROWS, 1)
            head_norm = a_head_norm[l].reshape(1, av)
            chunk_p = 128
            tok_p, c, n, m = _mlstm(
                proj, gates_r[:, :seq].reshape(1, GATE_ROWS, seq), bias_c, bias_r, head_norm,
                jnp.zeros((1, A_HEADS, A_DV, A_DQK), F32), jnp.zeros((1, SUBLANES, A_DQK), F32),
                jnp.zeros((1, SUBLANES, LANES), F32), 0, chunk_p, seq // chunk_p)
            c_p.append(c); n_p.append(n[:, :A_HEADS]); m_p.append(m[:, :A_HEADS, 0])
            gates_s = gates_r[:, seq:].reshape(GATE_ROWS, n_sb, dec_seq).transpose(1, 0, 2)
            m0 = jnp.broadcast_to(state_mlstm_m[l][:, :, None], (n_sb, A_HEADS, LANES))
            tok_s, c, n, m = _mlstm(
                proj, gates_s, bias_c, bias_r, head_norm, state_mlstm_C[l],
                _pad_rows(state_mlstm_n[l], SUBLANES), _pad_rows(m0, SUBLANES), sample_block0, dec_seq, 1)
            c_s.append(c); n_s.append(n[:, :A_HEADS]); m_s.append(m[:, :A_HEADS, 0])
            mq_block = n_main // mem_w
        else:
            proj = _proj(x, mix_gain, l, b_w, l - n_a, b_w.shape[-1])
            tok_p = _sb_prompt(proj, kv16, seq)
            tok_s = _sb_sample(proj, kv16, past_k, past_v, sample_block0, dec_seq)
            mq_block = sb_w // mem_w
        mo_p = _mem_attn(proj, mq_block, 0, mem_tile, seq // mem_tile, q_gain, l, mk_p, mv_p, l, False)
        mo_s = _mem_attn(proj, mq_block, sample_block0, dec_seq, n_sb, q_gain, l, mk_s, mv_s, l * n_sb, True)
        x = _out_proj(x, tok_p, tok_s, mo_p, mo_s, w_o, l)
        x = _ffn(x, *ffn2, l)
        if l == n_a - 1:
            k_p, v_p, k_s, v_s, kv16 = _kv_proj(x, kv_norm.reshape(1, d), w_kv, seq, n_sb, dec_seq)

    y_prompt = x[:seq].reshape(1, seq, d)
    y_sample = x[seq:].reshape(n_sb, dec_seq, d)
    k_p, v_p = (jnp.transpose(a, (1, 0, 2))[None] for a in (k_p, v_p))
    k_s, v_s = (jnp.transpose(a, (0, 2, 1, 3)) for a in (k_s, v_s))
    mem_out = lambda a: jnp.transpose(a, (0, 2, 1, 3))[:, None]
    return (y_prompt, y_sample, jnp.stack(c_p), jnp.stack(n_p), jnp.stack(m_p), k_p, v_p,
            mem_out(mk_p), mem_out(mv_p),
            jnp.stack(c_s), jnp.stack(n_s), jnp.stack(m_s), k_s, v_s)
```

```python
import functools

import jax
import jax.numpy as jnp
from jax import lax
from jax.experimental import pallas as pl
from jax.experimental.pallas import tpu as pltpu

F32 = jnp.float32
BF16 = jnp.bfloat16

RMS_EPS = 1e-6
A_HEADS = 6
A_DQK = 128
A_DV = 256
SB_HEADS = 12
SB_HD = 128
MEM_HEADS = 4
MEM_HD = 128
LANES = 128
SUBLANES = 8
GATE_ROWS = 16

ROW_TILE = 768
FFN_ROW_TILE = 1056
FF_TILE = 512
FF_HEAD_TILE = 256
VMEM_LIMIT = 58 * 1024 * 1024

SB_TILE = 256
SB_CHAINS = 4
SB_PAST_TILE = 256
SB_PAD = 768

NT_DIMS = (((1,), (1,)), ((), ()))
TN_DIMS = (((0,), (0,)), ((), ()))
EXP_ZERO_BELOW = -104.0


def _params(n_axes):
    return pltpu.CompilerParams(
        dimension_semantics=("arbitrary",) * n_axes, vmem_limit_bytes=VMEM_LIMIT)


def _rms(x, g):
    ms = jnp.mean(x * x, axis=-1, keepdims=True)
    return x * lax.rsqrt(ms + RMS_EPS) * g


def _log_sigmoid(x):
    return jnp.minimum(x, 0.0) - jnp.log1p(jnp.exp(-jnp.abs(x)))


def _dot(a, b):
    return jnp.dot(a, b, preferred_element_type=F32)


def _dot_nt(a, b):
    return lax.dot_general(a, b, NT_DIMS, preferred_element_type=F32)


def _split3(x):
    x1 = x.astype(BF16)
    r = x - x1.astype(F32)
    x2 = r.astype(BF16)
    x3 = (r - x2.astype(F32)).astype(BF16)
    return x1, x2, x3


def _ffn_step(h_ref, wg, wu, wd, o_ref):
    h = h_ref[...]
    g = _dot(h, wg)
    u = _dot(h, wu)
    a = (g * jax.nn.sigmoid(g) * u * 0.5).astype(BF16)
    o_ref[...] += _dot(a, wd)


def _ffn_head_kernel(x_ref, g_ref, wg_ref, wu_ref, wd_ref, o_ref, wg16_ref, wu16_ref, wd16_ref, h_ref):
    @pl.when(pl.program_id(0) == 0)
    def _():
        x = x_ref[...]
        h_ref[...] = _rms(x, g_ref[...]).astype(BF16)
        o_ref[...] = x

    wg16_ref[...] = wg_ref[...].astype(BF16)
    wu16_ref[...] = wu_ref[...].astype(BF16)
    wd16_ref[...] = wd_ref[...].astype(BF16)
    _ffn_step(h_ref, wg16_ref[...], wu16_ref[...], wd16_ref[...], o_ref)


def _ffn_rest_kernel(x_ref, g_ref, wg_ref, wu_ref, wd_ref, o_ref, h_ref):
    @pl.when(pl.program_id(1) == 0)
    def _():
        x = x_ref[...]
        h_ref[...] = _rms(x, g_ref[...]).astype(BF16)
        o_ref[...] = x

    _ffn_step(h_ref, wg_ref[...], wu_ref[...], wd_ref[...], o_ref)


def _ffn(x, gain, wg, wu, wd, layer):
    m, d = x.shape
    f = wg.shape[-1]
    n_head = f // FF_HEAD_TILE
    w16 = lambda shape: jax.ShapeDtypeStruct(shape, BF16)
    x, wg16, wu16, wd16 = pl.pallas_call(
        _ffn_head_kernel,
        grid=(n_head,),
        in_specs=[
            pl.BlockSpec((FFN_ROW_TILE, d), lambda j: (0, 0)),
            pl.BlockSpec((None, 1, d), lambda j: (layer, 0, 0)),
            pl.BlockSpec((None, d, FF_HEAD_TILE), lambda j: (layer, 0, j)),
            pl.BlockSpec((None, d, FF_HEAD_TILE), lambda j: (layer, 0, j)),
            pl.BlockSpec((None, FF_HEAD_TILE, d), lambda j: (layer, j, 0)),
        ],
        out_specs=(
            pl.BlockSpec((FFN_ROW_TILE, d), lambda j: (0, 0)),
            pl.BlockSpec((d, FF_HEAD_TILE), lambda j: (0, j)),
            pl.BlockSpec((d, FF_HEAD_TILE), lambda j: (0, j)),
            pl.BlockSpec((FF_HEAD_TILE, d), lambda j: (j, 0)),
        ),
        out_shape=(jax.ShapeDtypeStruct((m, d), F32), w16((d, f)), w16((d, f)), w16((f, d))),
        scratch_shapes=[pltpu.VMEM((FFN_ROW_TILE, d), BF16)],
        input_output_aliases={0: 0},
        compiler_params=_params(1),
        name="ffn_head",
    )(x, gain, wg, wu, wd)
    return pl.pallas_call(
        _ffn_rest_kernel,
        grid=(m // FFN_ROW_TILE - 1, f // FF_TILE),
        in_specs=[
            pl.BlockSpec((FFN_ROW_TILE, d), lambda i, j: (i + 1, 0)),
            pl.BlockSpec((None, 1, d), lambda i, j: (layer, 0, 0)),
            pl.BlockSpec((d, FF_TILE), lambda i, j: (0, j)),
            pl.BlockSpec((d, FF_TILE), lambda i, j: (0, j)),
            pl.BlockSpec((FF_TILE, d), lambda i, j: (j, 0)),
        ],
        out_specs=pl.BlockSpec((FFN_ROW_TILE, d), lambda i, j: (i + 1, 0)),
        out_shape=jax.ShapeDtypeStruct((m, d), F32),
        scratch_shapes=[pltpu.VMEM((FFN_ROW_TILE, d), BF16)],
        input_output_aliases={0: 0},
        compiler_params=_params(2),
        name="ffn",
    )(x, gain, wg16, wu16, wd16)


def _proj_kernel(x_ref, g_ref, w_ref, o_ref, h_ref):
    @pl.when(pl.program_id(1) == 0)
    def _():
        h_ref[...] = _rms(x_ref[...], g_ref[...]).astype(BF16)

    o_ref[...] = _dot(h_ref[...], w_ref[...]).astype(o_ref.dtype)


def _proj_gates_kernel(x_ref, g_ref, wm_ref, wt_ref, wgt_ref, om_ref, ot_ref, gt_ref, h_ref, *, n_main):
    j = pl.program_id(1)

    @pl.when(j == 0)
    def _():
        h = _rms(x_ref[...], g_ref[...]).astype(BF16)
        h_ref[...] = h
        gt_ref[...] = _dot_nt(wgt_ref[...], h)

    @pl.when(j < n_main)
    def _():
        om_ref[...] = _dot(h_ref[...], wm_ref[...])

    @pl.when(j == n_main)
    def _():
        ot_ref[...] = _dot(h_ref[...], wt_ref[...])


def _proj(x, gain, gain_layer, w, layer, col_tile):
    m, d = x.shape
    n = w.shape[-1]
    return pl.pallas_call(
        _proj_kernel,
        grid=(m // ROW_TILE, n // col_tile),
        in_specs=[
            pl.BlockSpec((ROW_TILE, d), lambda i, j: (i, 0)),
            pl.BlockSpec((None, 1, d), lambda i, j: (gain_layer, 0, 0)),
            pl.BlockSpec((None, d, col_tile), lambda i, j: (layer, 0, j)),
        ],
        out_specs=pl.BlockSpec((ROW_TILE, col_tile), lambda i, j: (i, j)),
        out_shape=jax.ShapeDtypeStruct((m, n), F32),
        scratch_shapes=[pltpu.VMEM((ROW_TILE, d), BF16)],
        compiler_params=_params(2),
        name="proj",
    )(x, gain, w)


def _proj_gates(x, gain, w_main, w_tail, wgt, layer, main_tile):
    m, d = x.shape
    n_main = w_main.shape[-1]
    n_tail = w_tail.shape[-1]
    main_steps = n_main // main_tile
    main = lambda i, j: (i, jnp.minimum(j, main_steps - 1))
    return pl.pallas_call(
        functools.partial(_proj_gates_kernel, n_main=main_steps),
        grid=(m // ROW_TILE, main_steps + 1),
        in_specs=[
            pl.BlockSpec((ROW_TILE, d), lambda i, j: (i, 0)),
            pl.BlockSpec((None, 1, d), lambda i, j: (layer, 0, 0)),
            pl.BlockSpec((None, d, main_tile), lambda i, j: (layer, 0, jnp.minimum(j, main_steps - 1))),
            pl.BlockSpec((None, d, n_tail), lambda i, j: (layer, 0, 0)),
            pl.BlockSpec((None, GATE_ROWS, d), lambda i, j: (layer, 0, 0)),
        ],
        out_specs=(pl.BlockSpec((ROW_TILE, main_tile), main),
                   pl.BlockSpec((ROW_TILE, n_tail), lambda i, j: (i, 0)),
                   pl.BlockSpec((GATE_ROWS, ROW_TILE), lambda i, j: (0, i))),
        out_shape=(jax.ShapeDtypeStruct((m, n_main), F32), jax.ShapeDtypeStruct((m, n_tail), F32),
                   jax.ShapeDtypeStruct((GATE_ROWS, m), F32)),
        scratch_shapes=[pltpu.VMEM((ROW_TILE, d), BF16)],
        compiler_params=_params(2),
        name="proj_gates",
    )(x, gain, w_main, w_tail, wgt)


def _kv_kernel(x_ref, g_ref, w_ref, kp_ref, vp_ref, ks_ref, vs_ref, o16_ref,
               *, n_pad, n_prompt, n_seq, rows):
    i = pl.program_id(0)

    @pl.when(i < n_pad)
    def _():
        o16_ref[...] = jnp.zeros(o16_ref.shape, BF16)

    @pl.when(i >= n_pad)
    def _():
        h = _rms(x_ref[...], g_ref[...]).astype(BF16)
        heads = [slice(hd * SB_HD, (hd + 1) * SB_HD) for hd in range(SB_HEADS)]
        for j, (p_ref, s_ref) in enumerate(((kp_ref, ks_ref), (vp_ref, vs_ref))):
            y = _dot(h, w_ref[j])
            o16_ref[j] = y.astype(BF16)

            @pl.when(i < n_pad + n_prompt)
            def _():
                for hd, sl in enumerate(heads):
                    p_ref[hd] = y[:, sl]

            @pl.when(i >= n_pad + n_prompt)
            def _():
                for b in range(n_seq):
                    for hd, sl in enumerate(heads):
                        s_ref[b, hd] = y[b * rows:(b + 1) * rows, sl]


def _kv_proj(x, gain, w, seq, n_seq, rows):
    m, d = x.shape
    n = w.shape[-1]
    tile = n_seq * rows
    n_prompt = seq // tile
    n_pad = SB_PAD // tile
    assert n_prompt * tile == seq and seq + tile == m and n_pad * tile == SB_PAD
    row_tile = lambda i: jnp.clip(i - n_pad, 0, n_prompt - 1)
    prompt_spec = pl.BlockSpec((SB_HEADS, tile, SB_HD), lambda i: (0, row_tile(i), 0))
    sample_spec = pl.BlockSpec((n_seq, SB_HEADS, rows, SB_HD), lambda i: (0, 0, 0, 0))
    prompt_shape = jax.ShapeDtypeStruct((SB_HEADS, seq, SB_HD), F32)
    sample_shape = jax.ShapeDtypeStruct((n_seq, SB_HEADS, rows, SB_HD), F32)
    return pl.pallas_call(
        functools.partial(_kv_kernel, n_pad=n_pad, n_prompt=n_prompt, n_seq=n_seq, rows=rows),
        grid=(n_pad + n_prompt + 1,),
        in_specs=[
            pl.BlockSpec((tile, d), lambda i: (jnp.maximum(i - n_pad, 0), 0)),
            pl.BlockSpec((1, d), lambda i: (0, 0)),
            pl.BlockSpec((2, d, n), lambda i: (0, 0, 0)),
        ],
        out_specs=(prompt_spec, prompt_spec, sample_spec, sample_spec,
                   pl.BlockSpec((2, tile, n), lambda i: (0, i, 0))),
        out_shape=(prompt_shape, prompt_shape, sample_shape, sample_shape,
                   jax.ShapeDtypeStruct((2, SB_PAD + m, n), BF16)),
        compiler_params=_params(1),
        name="kv_proj",
    )(x, gain, w)


def _out_proj_kernel(x_ref, tokp_ref, toks_ref, mop_ref, mos_ref, wt_ref, wm_ref, o_ref, w16_ref,
                     *, n_prompt):
    i = pl.program_id(0)
    kt = wt_ref.shape[0]

    @pl.when(i == 0)
    def _():
        w16_ref[:kt] = wt_ref[...].astype(BF16)
        w16_ref[kt:] = wm_ref[...].astype(BF16)

    @pl.when(i < n_prompt)
    def _():
        o_ref[...] = (x_ref[...] + _dot(tokp_ref[...], w16_ref[:kt])
                      + _dot(mop_ref[...], w16_ref[kt:]))

    @pl.when(i >= n_prompt)
    def _():
        o_ref[...] = (x_ref[...] + _dot(toks_ref[...], w16_ref[:kt])
                      + _dot(mos_ref[...], w16_ref[kt:]))


def _out_proj(x, tok_p, tok_s, mo_p, mo_s, w, layer):
    m, d = x.shape
    kt = tok_p.shape[1]
    km = mo_p.shape[1]
    tile = tok_s.shape[0]
    n_prompt = tok_p.shape[0] // tile
    assert n_prompt * tile == tok_p.shape[0] and (n_prompt + 1) * tile == m
    prompt = lambda i: (jnp.minimum(i, n_prompt - 1), 0)
    return pl.pallas_call(
        functools.partial(_out_proj_kernel, n_prompt=n_prompt),
        grid=(n_prompt + 1,),
        in_specs=[
            pl.BlockSpec((tile, d), lambda i: (i, 0)),
            pl.BlockSpec((tile, kt), prompt),
            pl.BlockSpec((tile, kt), lambda i: (0, 0)),
            pl.BlockSpec((tile, km), prompt),
            pl.BlockSpec((tile, km), lambda i: (0, 0)),
            pl.BlockSpec((None, kt, d), lambda i: (layer, 0, 0)),
            pl.BlockSpec((None, km, d), lambda i: (layer, kt // km, 0)),
        ],
        out_specs=pl.BlockSpec((tile, d), lambda i: (i, 0)),
        out_shape=jax.ShapeDtypeStruct((m, d), F32),
        scratch_shapes=[pltpu.VMEM((kt + km, d), BF16)],
        compiler_params=_params(1),
        name="out_proj",
    )(x, tok_p, tok_s, mo_p, mo_s, w, w)


def _mem_kv_kernel(mem_ref, g_ref, wk_ref, wv_ref, kg_ref, mk_ref, mv_ref):
    h = _rms(mem_ref[...], g_ref[...]).astype(BF16)
    k = _dot(h, wk_ref[...].astype(BF16))
    v = _dot(h, wv_ref[...].astype(BF16))
    kg = kg_ref[...]
    for hd in range(MEM_HEADS):
        sl = slice(hd * MEM_HD, (hd + 1) * MEM_HD)
        mk_ref[hd] = _rms(k[:, sl], kg)
        mv_ref[hd] = v[:, sl]


def _mem_kv(mem, gain, wk, wv, k_gain):
    depth, d, w = wk.shape
    n = mem.shape[0]
    out_spec = pl.BlockSpec((None, MEM_HEADS, n, MEM_HD), lambda l: (l, 0, 0, 0))
    out_shape = jax.ShapeDtypeStruct((depth, MEM_HEADS, n, MEM_HD), F32)
    return pl.pallas_call(
        _mem_kv_kernel,
        grid=(depth,),
        in_specs=[
            pl.BlockSpec((n, d), lambda l: (0, 0)),
            pl.BlockSpec((None, 1, d), lambda l: (l, 0, 0)),
            pl.BlockSpec((None, d, w), lambda l: (l, 0, 0)),
            pl.BlockSpec((None, d, w), lambda l: (l, 0, 0)),
            pl.BlockSpec((None, 1, MEM_HD), lambda l: (l, 0, 0)),
        ],
        out_specs=(out_spec, out_spec),
        out_shape=(out_shape, out_shape),
        compiler_params=_params(1),
        name="mem_kv",
    )(mem, gain, wk, wv, k_gain)


def _mem_attn_heads(mq_ref, qg_ref, keys, values, masks, o_ref):
    heads = range(MEM_HEADS)
    cols = [slice(hd * MEM_HD, (hd + 1) * MEM_HD) for hd in heads]
    qg = qg_ref[...]
    qn = [_rms(mq_ref[:, cols[hd]], qg).astype(BF16) for hd in heads]
    s = [_dot_nt(qn[hd], keys[hd]) * (MEM_HD ** -0.5) for hd in heads]
    if masks is not None:
        s = [jnp.where(masks[hd], s[hd], -jnp.inf) for hd in heads]
    e = [jnp.exp(s[hd] - jnp.max(s[hd], axis=-1, keepdims=True)) for hd in heads]
    p = [(e[hd] / jnp.sum(e[hd], axis=-1, keepdims=True)).astype(BF16) for hd in heads]
    out = [_dot(p[hd], values[hd]) for hd in heads]
    for hd in heads:
        o_ref[:, cols[hd]] = out[hd].astype(BF16)


def _mem_attn_prompt_kernel(mq_ref, qg_ref, mk_ref, mv_ref, o_ref):
    _mem_attn_heads(mq_ref, qg_ref, [mk_ref[hd].astype(BF16) for hd in range(MEM_HEADS)],
                    [mv_ref[hd].astype(BF16) for hd in range(MEM_HEADS)], None, o_ref)


def _mem_attn_sample_kernel(mq_ref, qg_ref, mk_ref, mv_ref, o_ref):
    k16 = mk_ref[...].astype(BF16)
    v16 = mv_ref[...].astype(BF16)
    row_head = lax.broadcasted_iota(jnp.int32, (mq_ref.shape[0], k16.shape[0]), 1) % MEM_HEADS
    _mem_attn_heads(mq_ref, qg_ref, [k16] * MEM_HEADS, [v16] * MEM_HEADS,
                    [row_head == hd for hd in range(MEM_HEADS)], o_ref)


def _mem_attn(proj, col_block, row_block0, rows, n_tiles, q_gain, layer, mk, mv, kv_index0, per_tile_kv):
    w = MEM_HEADS * MEM_HD
    if per_tile_kv:
        body = _mem_attn_sample_kernel
        kv_spec = pl.BlockSpec((None,) + mk.shape[1:], lambda b: (kv_index0 + b, 0, 0))
    else:
        body = _mem_attn_prompt_kernel
        kv_spec = pl.BlockSpec((None,) + mk.shape[1:], lambda b: (kv_index0, 0, 0, 0))
    return pl.pallas_call(
        body,
        grid=(n_tiles,),
        in_specs=[
            pl.BlockSpec((rows, w), lambda b: (row_block0 + b, col_block)),
            pl.BlockSpec((None, 1, MEM_HD), lambda b: (layer, 0, 0)),
            kv_spec,
            kv_spec,
        ],
        out_specs=pl.BlockSpec((rows, w), lambda b: (b, 0)),
        out_shape=jax.ShapeDtypeStruct((n_tiles * rows, w), BF16),
        compiler_params=_params(1),
        name="mem_attn",
    )(proj, q_gain, mk, mv)


def _mlstm_kernel(q_ref, k_ref, v_ref, og_ref, gc_ref, gr_ref, bc_ref, br_ref, hn_ref,
                  c0_ref, n0_ref, m0_ref, tok_ref, c_ref, n_ref, m_ref, *, chunk):
    L = chunk

    @pl.when(pl.program_id(1) == 0)
    def _():
        c_ref[...] = c0_ref[...]
        n_ref[...] = n0_ref[...]
        m_ref[...] = m0_ref[...]

    row = lax.broadcasted_iota(jnp.int32, (L, L), 0)
    col = lax.broadcasted_iota(jnp.int32, (L, L), 1)
    causal = col <= row
    tri = jnp.where(causal, 1.0, 0.0).astype(BF16)

    pre_c = gc_ref[...] + bc_ref[...]
    pre_r = gr_ref[...] + br_ref[...]
    lf_c = _log_sigmoid(pre_c)
    lf_r = _log_sigmoid(pre_r)
    cum_c = sum(_dot(tri, part) for part in _split3(lf_c))
    cum_r = sum(_dot_nt(part, tri) for part in _split3(lf_r))

    scale = A_DQK ** -0.5
    heads = range(A_HEADS)
    qk = [slice(h * A_DQK, (h + 1) * A_DQK) for h in heads]
    vv = [slice(h * A_DV, (h + 1) * A_DV) for h in heads]
    q = [q_ref[:, qk[h]] * scale for h in heads]
    q16 = [q[h].astype(BF16) for h in heads]
    k16 = [k_ref[:, qk[h]].astype(BF16) for h in heads]
    c_old = [c_ref[h] for h in heads]
    n_old = [n_ref[h:h + 1, :] for h in heads]
    m_old = [m_ref[h:h + 1, 0:1] for h in heads]
    i_c = [pre_c[:, h:h + 1] for h in heads]
    b_c = [cum_c[:, A_HEADS + h:A_HEADS + h + 1] for h in heads]
    i_r = [pre_r[h:h + 1, :] for h in heads]
    b_r = [cum_r[A_HEADS + h:A_HEADS + h + 1, :] for h in heads]

    qk_t = [_dot_nt(q16[h], k16[h]) for h in heads]
    q_c = [_dot_nt(q16[h], c_old[h].astype(BF16)) for h in heads]

    s, w_st, m_t = [], [], []
    for h in heads:
        d = jnp.where(causal, b_c[h] - b_r[h] + i_r[h], -jnp.inf)
        inter = b_c[h] + m_old[h]
        m_t.append(jnp.maximum(inter, jnp.max(d, axis=-1, keepdims=True)))
        w_st.append(jnp.exp(inter - m_t[h]))
        s.append(qk_t[h] * jnp.exp(d - m_t[h]))
    s_v = [_dot(s[h].astype(BF16), v_ref[:, vv[h]].astype(BF16)) for h in heads]

    for h in heads:
        num = s_v[h] + w_st[h] * q_c[h]
        den = (jnp.sum(s[h], axis=-1, keepdims=True)
               + w_st[h] * jnp.sum(q[h] * n_old[h], axis=-1, keepdims=True))
        hh = num * (1.0 / jnp.maximum(jnp.abs(den), jnp.exp(-m_t[h])))
        out = _rms(hh, hn_ref[:, vv[h]]) * jax.nn.sigmoid(og_ref[:, vv[h]])
        tok_ref[:, vv[h]] = out.astype(BF16)

    decay, w_k = [], []
    for h in heads:
        b_end = b_c[h][L - 1:L, :]
        g = b_end - b_c[h] + i_c[h]
        m_new = jnp.maximum(b_end + m_old[h], jnp.max(g, axis=0, keepdims=True))
        w_k.append(jnp.exp(g - m_new))
        decay.append(jnp.exp(b_end + m_old[h] - m_new))
        m_ref[h:h + 1, :] = jnp.broadcast_to(m_new, (1, LANES))
    vw_k = [lax.dot_general((v_ref[:, vv[h]] * w_k[h]).astype(BF16), k16[h], TN_DIMS,
                            preferred_element_type=F32) for h in heads]
    for h in heads:
        c_ref[h] = decay[h] * c_old[h] + vw_k[h]
        n_ref[h:h + 1, :] = (decay[h] * n_old[h]
                             + jnp.sum(w_k[h] * k_ref[:, qk[h]], axis=0, keepdims=True))


def _mlstm(proj, tail, gates_r, bias_c, bias_r, head_norm, c0, n0, m0, row_block0, chunk, n_chunks):
    n_seq = c0.shape[0]
    aq = A_HEADS * A_DQK
    av = A_HEADS * A_DV
    gate_block = MEM_HEADS * MEM_HD // LANES

    def rows(b, c):
        return row_block0 + b * n_chunks + c

    state = lambda b, c: (b, 0, 0)
    return pl.pallas_call(
        functools.partial(_mlstm_kernel, chunk=chunk),
        grid=(n_seq, n_chunks),
        in_specs=[
            pl.BlockSpec((chunk, aq), lambda b, c: (rows(b, c), 0)),
            pl.BlockSpec((chunk, aq), lambda b, c: (rows(b, c), 1)),
            pl.BlockSpec((chunk, av), lambda b, c: (rows(b, c), 1)),
            pl.BlockSpec((chunk, av), lambda b, c: (rows(b, c), 2)),
            pl.BlockSpec((chunk, LANES), lambda b, c: (rows(b, c), gate_block)),
            pl.BlockSpec((None, GATE_ROWS, chunk), lambda b, c: (b, 0, c)),
            pl.BlockSpec((1, LANES), lambda b, c: (0, 0)),
            pl.BlockSpec((GATE_ROWS, 1), lambda b, c: (0, 0)),
            pl.BlockSpec((1, av), lambda b, c: (0, 0)),
            pl.BlockSpec((None, A_HEADS, A_DV, A_DQK), lambda b, c: (b, 0, 0, 0)),
            pl.BlockSpec((None, SUBLANES, A_DQK), state),
            pl.BlockSpec((None, SUBLANES, LANES), state),
        ],
        out_specs=(
            pl.BlockSpec((chunk, av), lambda b, c: (b * n_chunks + c, 0)),
            pl.BlockSpec((None, A_HEADS, A_DV, A_DQK), lambda b, c: (b, 0, 0, 0)),
            pl.BlockSpec((None, SUBLANES, A_DQK), state),
            pl.BlockSpec((None, SUBLANES, LANES), state),
        ),
        out_shape=(
            jax.ShapeDtypeStruct((n_seq * n_chunks * chunk, av), BF16),
            jax.ShapeDtypeStruct((n_seq, A_HEADS, A_DV, A_DQK), F32),
            jax.ShapeDtypeStruct((n_seq, SUBLANES, A_DQK), F32),
            jax.ShapeDtypeStruct((n_seq, SUBLANES, LANES), F32),
        ),
        compiler_params=_params(2),
        name="mlstm",
    )(proj, proj, proj, proj, tail, gates_r, bias_c, bias_r, head_norm, c0, n0, m0)


def _sb_scores(q16, k16, valid):
    z = _dot_nt(q16, k16) * (SB_HD ** -0.5)
    sp = jnp.maximum(z, 0.0) + jnp.log(1.0 + jnp.exp(-jnp.abs(z)))
    if valid is not None:
        sp = jnp.where(valid, sp, 0.0)
    return z, sp


def _sb_newer(sp, upper):
    s1 = sp.astype(BF16)
    s2 = (sp - s1.astype(F32)).astype(BF16)
    return _dot(s1, upper) + _dot(s2, upper)


def _sb_weights(z, sp, newer, valid, r_prev):
    a = jnp.exp(z - sp - newer + r_prev)
    if valid is not None:
        a = jnp.where(valid, a, 0.0)
    return a.astype(BF16)


def _sb_tiles(qs, ks, vs, upper, valid, r_prevs):
    scores = [_sb_scores(q, k, valid) for q, k in zip(qs, ks)]
    newer = [_sb_newer(sp, upper) for _, sp in scores]
    outs = [_dot(_sb_weights(z, sp, nw, valid, r), v)
            for (z, sp), nw, r, v in zip(scores, newer, r_prevs, vs)]
    sums = [r - jnp.sum(sp, axis=-1, keepdims=True) for (_, sp), r in zip(scores, r_prevs)]
    return outs, sums


def _upper(n):
    row = lax.broadcasted_iota(jnp.int32, (n, n), 0)
    col = lax.broadcasted_iota(jnp.int32, (n, n), 1)
    return jnp.where(row > col, 1.0, 0.0).astype(BF16)


def _sb_prompt_kernel(q_ref, k_ref, v_ref, o_ref, q16_ref, acc_ref, r_ref):
    t = SB_TILE
    base = pl.program_id(1) * SB_CHAINS + SB_PAD // SB_TILE
    upper = _upper(t)
    row = lax.broadcasted_iota(jnp.int32, (t, t), 0)
    col = lax.broadcasted_iota(jnp.int32, (t, t), 1)
    q16_ref[...] = q_ref[...].astype(BF16)

    def walk(j, diagonal):
        rows = [slice(c * t, (c + 1) * t) for c in range(SB_CHAINS)]
        starts = [pl.multiple_of((base + c - j) * t, t) for c in range(SB_CHAINS)]
        outs, sums = _sb_tiles(
            [q16_ref[r, :] for r in rows],
            [k_ref[pl.ds(s, t), :] for s in starts],
            [v_ref[pl.ds(s, t), :] for s in starts],
            upper, col < row if diagonal else None,
            [jnp.zeros((t, 1), F32) if diagonal else r_ref[r, :] for r in rows])
        r_max = None
        for r, out, total in zip(rows, outs, sums):
            if diagonal:
                acc_ref[r, :] = out
            else:
                acc_ref[r, :] += out
            r_ref[r, :] = total
            r_c = jnp.max(total)
            r_max = r_c if r_max is None else jnp.maximum(r_max, r_c)
        return r_max

    def cond(carry):
        j, r_max = carry
        return jnp.logical_and(j <= base, r_max > EXP_ZERO_BELOW)

    def body(carry):
        j, _ = carry
        return j + 1, walk(j, False)

    lax.while_loop(cond, body, (jnp.int32(1), walk(0, True)))
    o_ref[...] = acc_ref[...].astype(BF16)


def _sb_prompt(proj, kv16, seq):
    step = SB_CHAINS * SB_TILE
    return pl.pallas_call(
        _sb_prompt_kernel,
        grid=(SB_HEADS, seq // step),
        in_specs=[
            pl.BlockSpec((step, SB_HD), lambda h, i: (i, h)),
            pl.BlockSpec((None, SB_PAD + seq, SB_HD), lambda h, i: (0, 0, h)),
            pl.BlockSpec((None, SB_PAD + seq, SB_HD), lambda h, i: (1, 0, h)),
        ],
        out_specs=pl.BlockSpec((step, SB_HD), lambda h, i: (i, h)),
        out_shape=jax.ShapeDtypeStruct((seq, SB_HEADS * SB_HD), BF16),
        scratch_shapes=[pltpu.VMEM((step, SB_HD), BF16), pltpu.VMEM((step, SB_HD), F32),
                        pltpu.VMEM((step, 1), F32)],
        compiler_params=_params(2),
        name="sb_prompt",
    )(proj, kv16, kv16)


def _sb_sample_kernel(q_ref, kn_ref, vn_ref, pk_hbm, pv_hbm, o_ref, kbuf, vbuf, sem, acc_ref, r_ref,
                      *, rows, n_past):
    b = pl.program_id(0)
    t = SB_PAST_TILE

    def copies(tile_index):
        start = pl.multiple_of(tile_index * t, t)
        return (
            pltpu.make_async_copy(pk_hbm.at[b, :, pl.ds(start, t), :], kbuf, sem.at[0]),
            pltpu.make_async_copy(pv_hbm.at[b, :, pl.ds(start, t), :], vbuf, sem.at[1]),
        )

    def fetch(tile_index):
        for cp in copies(tile_index):
            cp.start()

    def wait(tile_index):
        for cp in copies(tile_index):
            cp.wait()

    heads = [slice(h * SB_HD, (h + 1) * SB_HD) for h in range(SB_HEADS)]

    def past_tile():
        outs, sums = _sb_tiles(
            [q_ref[:, sl].astype(BF16) for sl in heads],
            [kbuf[h].astype(BF16) for h in range(SB_HEADS)],
            [vbuf[h].astype(BF16) for h in range(SB_HEADS)],
            _upper(t), None, [r_ref[h] for h in range(SB_HEADS)])
        r_max = None
        for h, (out, total) in enumerate(zip(outs, sums)):
            acc_ref[:, heads[h]] += out
            r_ref[h] = total
            r_h = jnp.max(total)
            r_max = r_h if r_max is None else jnp.maximum(r_max, r_h)
        return r_max

    fetch(n_past - 1)
    row = lax.broadcasted_iota(jnp.int32, (rows, rows), 0)
    col = lax.broadcasted_iota(jnp.int32, (rows, rows), 1)
    outs, sums = _sb_tiles(
        [q_ref[:, sl].astype(BF16) for sl in heads], [kn_ref[:, sl] for sl in heads],
        [vn_ref[:, sl] for sl in heads], _upper(rows), col < row,
        [jnp.zeros((rows, 1), F32)] * SB_HEADS)
    for h, (out, total) in enumerate(zip(outs, sums)):
        acc_ref[:, heads[h]] = out
        r_ref[h] = total
    wait(n_past - 1)
    r_max = past_tile()

    def cond(carry):
        tile_index, r_max = carry
        return jnp.logical_and(tile_index >= 0, r_max > EXP_ZERO_BELOW)

    def body(carry):
        tile_index, _ = carry
        fetch(tile_index)
        wait(tile_index)
        return tile_index - 1, past_tile()

    lax.while_loop(cond, body, (jnp.int32(n_past - 2), r_max))
    o_ref[...] = acc_ref[...].astype(BF16)


def _sb_sample(proj, kv16, past_k, past_v, row_block0, rows):
    n_seq, heads, past, hd = past_k.shape
    w = heads * hd
    new_block0 = row_block0 + SB_PAD // rows
    return pl.pallas_call(
        functools.partial(_sb_sample_kernel, rows=rows, n_past=past // SB_PAST_TILE),
        grid=(n_seq,),
        in_specs=[
            pl.BlockSpec((rows, w), lambda b: (row_block0 + b, 0)),
            pl.BlockSpec((None, rows, w), lambda b: (0, new_block0 + b, 0)),
            pl.BlockSpec((None, rows, w), lambda b: (1, new_block0 + b, 0)),
            pl.BlockSpec(memory_space=pl.ANY),
            pl.BlockSpec(memory_space=pl.ANY),
        ],
        out_specs=pl.BlockSpec((rows, w), lambda b: (b, 0)),
        out_shape=jax.ShapeDtypeStruct((n_seq * rows, w), BF16),
        scratch_shapes=[
            pltpu.VMEM((heads, SB_PAST_TILE, hd), F32),
            pltpu.VMEM((heads, SB_PAST_TILE, hd), F32),
            pltpu.SemaphoreType.DMA((2,)),
            pltpu.VMEM((rows, w), F32),
            pltpu.VMEM((heads, rows, 1), F32),
        ],
        compiler_params=_params(1),
        name="sb_sample",
    )(proj, kv16, kv16, past_k, past_v)


def _pad_rows(a, n):
    return jnp.pad(a, [(0, 0)] * (a.ndim - 2) + [(0, n - a.shape[-2]), (0, 0)])


def kernel(x_prompt, x_sample, state_mlstm_C, state_mlstm_n, state_mlstm_m, cache_sb_k, cache_sb_v,
           cache_mem_k, cache_mem_v, mem_prompt, ffn1_norm, ffn1_w_gate, ffn1_w_up, ffn1_w_down,
           ffn2_norm, ffn2_w_gate, ffn2_w_up, ffn2_w_down, mix_norm, a_w_in, a_b_i, a_b_f, a_head_norm,
           b_w_in, w_out, mem_norm, mem_w_k, mem_w_v, mem_q_norm, mem_k_norm, kv_norm, sb_w_k, sb_w_v):
    n_pb, seq, d = x_prompt.shape
    n_sb, dec_seq, _ = x_sample.shape
    assert n_pb == 1
    depth = ffn1_norm.shape[0]
    n_a = a_w_in.shape[0]
    n_slots = mem_prompt.shape[1]
    aq = A_HEADS * A_DQK
    av = A_HEADS * A_DV
    mem_w = MEM_HEADS * MEM_HD
    sb_w = SB_HEADS * SB_HD
    n_sample = n_sb * dec_seq
    sample_block0 = seq // dec_seq

    gains = lambda g: g.reshape(g.shape[0], 1, g.shape[-1])
    ffn1 = (gains(ffn1_norm), ffn1_w_gate, ffn1_w_up, ffn1_w_down)
    ffn2 = (gains(ffn2_norm), ffn2_w_gate, ffn2_w_up, ffn2_w_down)
    mix_gain = gains(mix_norm)
    n_main = 2 * aq + 2 * av
    w_gates = a_w_in[:, :, n_main:n_main + 2 * A_HEADS]
    a_main = a_w_in[:, :, :n_main].astype(BF16)
    a_tail = jnp.concatenate(
        [a_w_in[:, :, n_main + 2 * A_HEADS:], w_gates,
         jnp.zeros((n_a, d, LANES - 2 * A_HEADS), F32)], axis=-1).astype(BF16)
    a_wgt = _pad_rows(jnp.swapaxes(w_gates, 1, 2), GATE_ROWS).astype(BF16)
    gate_bias = jnp.concatenate([a_b_i, a_b_f], axis=-1)
    b_w = b_w_in.astype(BF16)
    w_kv = jnp.stack([sb_w_k, sb_w_v]).astype(BF16)
    q_gain = mem_q_norm.reshape(depth, 1, MEM_HD)

    mk_p, mv_p = _mem_kv(mem_prompt[0], gains(mem_norm), mem_w_k, mem_w_v,
                         mem_k_norm.reshape(depth, 1, MEM_HD))
    mk_s = cache_mem_k.reshape(depth * n_sb, n_slots * MEM_HEADS, MEM_HD)
    mv_s = cache_mem_v.reshape(depth * n_sb, n_slots * MEM_HEADS, MEM_HD)

    past_k = jnp.transpose(cache_sb_k, (0, 2, 1, 3))
    past_v = jnp.transpose(cache_sb_v, (0, 2, 1, 3))

    x = jnp.concatenate([x_prompt[0], x_sample.reshape(n_sample, d)], axis=0)
    c_p, n_p, m_p, c_s, n_s, m_s = [], [], [], [], [], []
    k_p = v_p = k_s = v_s = kv16 = None
    mem_tile = 1024
    for l in range(depth):
        x = _ffn(x, *ffn1, l)
        if l < n_a:
            proj, mq_src, gates_r = _proj_gates(x, mix_gain, a_main, a_tail, a_wgt, l, n_main // 3)
            bias_c = jnp.pad(gate_bias[l], (0, LANES - 2 * A_HEADS)).reshape(1, LANES)
            bias_r = jnp.pad(gate_bias[l], (0, GATE_ROWS - 2 * A_HEADS)).reshape(GATE_ROWS, 1)
            head_norm = a_head_norm[l].reshape(1, av)
            chunk_p = 128
            tok_p, c, n, m = _mlstm(
                proj, mq_src, gates_r[:, :seq].reshape(1, GATE_ROWS, seq), bias_c, bias_r, head_norm,
                jnp.zeros((1, A_HEADS, A_DV, A_DQK), F32), jnp.zeros((1, SUBLANES, A_DQK), F32),
                jnp.zeros((1, SUBLANES, LANES), F32), 0, chunk_p, seq // chunk_p)
            c_p.append(c); n_p.append(n[:, :A_HEADS]); m_p.append(m[:, :A_HEADS, 0])
            gates_s = gates_r[:, seq:].reshape(GATE_ROWS, n_sb, dec_seq).transpose(1, 0, 2)
            m0 = jnp.broadcast_to(state_mlstm_m[l][:, :, None], (n_sb, A_HEADS, LANES))
            tok_s, c, n, m = _mlstm(
                proj, mq_src, gates_s, bias_c, bias_r, head_norm, state_mlstm_C[l],
                _pad_rows(state_mlstm_n[l], SUBLANES), _pad_rows(m0, SUBLANES), sample_block0, dec_seq, 1)
            c_s.append(c); n_s.append(n[:, :A_HEADS]); m_s.append(m[:, :A_HEADS, 0])
            mq_block = 0
        else:
            proj = mq_src = _proj(x, mix_gain, l, b_w, l - n_a, b_w.shape[-1])
            tok_p = _sb_prompt(proj, kv16, seq)
            tok_s = _sb_sample(proj, kv16, past_k, past_v, sample_block0, dec_seq)
            mq_block = sb_w // mem_w
        mo_p = _mem_attn(mq_src, mq_block, 0, mem_tile, seq // mem_tile, q_gain, l, mk_p, mv_p, l, False)
        mo_s = _mem_attn(mq_src, mq_block, sample_block0, dec_seq, n_sb, q_gain, l, mk_s, mv_s, l * n_sb, True)
        x = _out_proj(x, tok_p, tok_s, mo_p, mo_s, w_out, l)
        x = _ffn(x, *ffn2, l)
        if l == n_a - 1:
            k_p, v_p, k_s, v_s, kv16 = _kv_proj(x, kv_norm.reshape(1, d), w_kv, seq, n_sb, dec_seq)

    y_prompt = x[:seq].reshape(1, seq, d)
    y_sample = x[seq:].reshape(n_sb, dec_seq, d)
    k_p, v_p = (jnp.transpose(a, (1, 0, 2))[None] for a in (k_p, v_p))
    k_s, v_s = (jnp.transpose(a, (0, 2, 1, 3)) for a in (k_s, v_s))
    mem_out = lambda a: jnp.transpose(a, (0, 2, 1, 3))[:, None]
    return (y_prompt, y_sample, jnp.stack(c_p), jnp.stack(n_p), jnp.stack(m_p), k_p, v_p,
            mem_out(mk_p), mem_out(mv_p),
            jnp.stack(c_s), jnp.stack(n_s), jnp.stack(m_s), k_s, v_s)
```

```python
import functools

import jax
import jax.numpy as jnp
from jax import lax
from jax.experimental import pallas as pl
from jax.experimental.pallas import tpu as pltpu

F32 = jnp.float32
BF16 = jnp.bfloat16

RMS_EPS = 1e-6
A_HEADS = 6
A_DQK = 128
A_DV = 256
SB_HEADS = 12
SB_HD = 128
MEM_HEADS = 4
MEM_HD = 128
LANES = 128
SUBLANES = 8
GATE_ROWS = 16

ROW_TILE = 768
FFN_ROW_TILE = 1056
FF_TILE = 512
FF_HEAD_TILE = 256
VMEM_LIMIT = 58 * 1024 * 1024

SB_TILE = 256
SB_CHAINS = 4
SB_PAST_TILE = 256
SB_PAD = 768

NT_DIMS = (((1,), (1,)), ((), ()))
TN_DIMS = (((0,), (0,)), ((), ()))
EXP_ZERO_BELOW = -104.0


def _params(n_axes):
    return pltpu.CompilerParams(
        dimension_semantics=("arbitrary",) * n_axes, vmem_limit_bytes=VMEM_LIMIT)


def _rms(x, g):
    ms = jnp.mean(x * x, axis=-1, keepdims=True)
    return x * lax.rsqrt(ms + RMS_EPS) * g


def _log_sigmoid(x):
    return jnp.minimum(x, 0.0) - jnp.log1p(jnp.exp(-jnp.abs(x)))


def _dot(a, b):
    return jnp.dot(a, b, preferred_element_type=F32)


def _dot_nt(a, b):
    return lax.dot_general(a, b, NT_DIMS, preferred_element_type=F32)


def _split3(x):
    x1 = x.astype(BF16)
    r = x - x1.astype(F32)
    x2 = r.astype(BF16)
    x3 = (r - x2.astype(F32)).astype(BF16)
    return x1, x2, x3


def _ffn_step(h_ref, wg, wu, wd, o_ref):
    h = h_ref[...]
    g = _dot(h, wg)
    u = _dot(h, wu)
    a = (g * jax.nn.sigmoid(g) * u * 0.5).astype(BF16)
    o_ref[...] += _dot(a, wd)


def _ffn_head_kernel(x_ref, g_ref, wg_ref, wu_ref, wd_ref, o_ref, wg16_ref, wu16_ref, wd16_ref, h_ref):
    @pl.when(pl.program_id(0) == 0)
    def _():
        x = x_ref[...]
        h_ref[...] = _rms(x, g_ref[...]).astype(BF16)
        o_ref[...] = x

    wg16_ref[...] = wg_ref[...].astype(BF16)
    wu16_ref[...] = wu_ref[...].astype(BF16)
    wd16_ref[...] = wd_ref[...].astype(BF16)
    _ffn_step(h_ref, wg16_ref[...], wu16_ref[...], wd16_ref[...], o_ref)


def _ffn_rest_kernel(x_ref, g_ref, wg_ref, wu_ref, wd_ref, o_ref, h_ref):
    @pl.when(pl.program_id(1) == 0)
    def _():
        x = x_ref[...]
        h_ref[...] = _rms(x, g_ref[...]).astype(BF16)
        o_ref[...] = x

    _ffn_step(h_ref, wg_ref[...], wu_ref[...], wd_ref[...], o_ref)


def _ffn(x, gain, wg, wu, wd, layer):
    m, d = x.shape
    f = wg.shape[-1]
    n_head = f // FF_HEAD_TILE
    w16 = lambda shape: jax.ShapeDtypeStruct(shape, BF16)
    x, wg16, wu16, wd16 = pl.pallas_call(
        _ffn_head_kernel,
        grid=(n_head,),
        in_specs=[
            pl.BlockSpec((FFN_ROW_TILE, d), lambda j: (0, 0)),
            pl.BlockSpec((None, 1, d), lambda j: (layer, 0, 0)),
            pl.BlockSpec((None, d, FF_HEAD_TILE), lambda j: (layer, 0, j)),
            pl.BlockSpec((None, d, FF_HEAD_TILE), lambda j: (layer, 0, j)),
            pl.BlockSpec((None, FF_HEAD_TILE, d), lambda j: (layer, j, 0)),
        ],
        out_specs=(
            pl.BlockSpec((FFN_ROW_TILE, d), lambda j: (0, 0)),
            pl.BlockSpec((d, FF_HEAD_TILE), lambda j: (0, j)),
            pl.BlockSpec((d, FF_HEAD_TILE), lambda j: (0, j)),
            pl.BlockSpec((FF_HEAD_TILE, d), lambda j: (j, 0)),
        ),
        out_shape=(jax.ShapeDtypeStruct((m, d), F32), w16((d, f)), w16((d, f)), w16((f, d))),
        scratch_shapes=[pltpu.VMEM((FFN_ROW_TILE, d), BF16)],
        input_output_aliases={0: 0},
        compiler_params=_params(1),
        name="ffn_head",
    )(x, gain, wg, wu, wd)
    return pl.pallas_call(
        _ffn_rest_kernel,
        grid=(m // FFN_ROW_TILE - 1, f // FF_TILE),
        in_specs=[
            pl.BlockSpec((FFN_ROW_TILE, d), lambda i, j: (i + 1, 0)),
            pl.BlockSpec((None, 1, d), lambda i, j: (layer, 0, 0)),
            pl.BlockSpec((d, FF_TILE), lambda i, j: (0, j)),
            pl.BlockSpec((d, FF_TILE), lambda i, j: (0, j)),
            pl.BlockSpec((FF_TILE, d), lambda i, j: (j, 0)),
        ],
        out_specs=pl.BlockSpec((FFN_ROW_TILE, d), lambda i, j: (i + 1, 0)),
        out_shape=jax.ShapeDtypeStruct((m, d), F32),
        scratch_shapes=[pltpu.VMEM((FFN_ROW_TILE, d), BF16)],
        input_output_aliases={0: 0},
        compiler_params=_params(2),
        name="ffn",
    )(x, gain, wg16, wu16, wd16)


def _proj_kernel(x_ref, g_ref, w_ref, o_ref, h_ref):
    @pl.when(pl.program_id(1) == 0)
    def _():
        h_ref[...] = _rms(x_ref[...], g_ref[...]).astype(BF16)

    o_ref[...] = _dot(h_ref[...], w_ref[...]).astype(o_ref.dtype)


def _proj_gates_kernel(x_ref, g_ref, w_ref, wgt_ref, o_ref, gt_ref, h_ref):
    @pl.when(pl.program_id(1) == 0)
    def _():
        h = _rms(x_ref[...], g_ref[...]).astype(BF16)
        h_ref[...] = h
        gt_ref[...] = _dot_nt(wgt_ref[...], h)

    o_ref[...] = _dot(h_ref[...], w_ref[...]).astype(o_ref.dtype)


def _proj(x, gain, gain_layer, w, layer, col_tile, wgt=None):
    m, d = x.shape
    n = w.shape[-1]
    in_specs = [
        pl.BlockSpec((ROW_TILE, d), lambda i, j: (i, 0)),
        pl.BlockSpec((None, 1, d), lambda i, j: (gain_layer, 0, 0)),
        pl.BlockSpec((None, d, col_tile), lambda i, j: (layer, 0, j)),
    ]
    out_spec = pl.BlockSpec((ROW_TILE, col_tile), lambda i, j: (i, j))
    out_shape = jax.ShapeDtypeStruct((m, n), F32)
    common = dict(
        grid=(m // ROW_TILE, n // col_tile),
        scratch_shapes=[pltpu.VMEM((ROW_TILE, d), BF16)],
        compiler_params=_params(2),
    )
    if wgt is None:
        return pl.pallas_call(
            _proj_kernel, in_specs=in_specs, out_specs=out_spec, out_shape=out_shape,
            name="proj", **common)(x, gain, w)
    return pl.pallas_call(
        _proj_gates_kernel,
        in_specs=in_specs + [pl.BlockSpec((None, GATE_ROWS, d), lambda i, j: (layer, 0, 0))],
        out_specs=(out_spec, pl.BlockSpec((GATE_ROWS, ROW_TILE), lambda i, j: (0, i))),
        out_shape=(out_shape, jax.ShapeDtypeStruct((GATE_ROWS, m), F32)),
        name="proj_gates", **common)(x, gain, w, wgt)


def _kv_kernel(x_ref, g_ref, w_ref, kp_ref, vp_ref, ks_ref, vs_ref, o16_ref,
               *, n_pad, n_prompt, n_seq, rows):
    i = pl.program_id(0)

    @pl.when(i < n_pad)
    def _():
        o16_ref[...] = jnp.zeros(o16_ref.shape, BF16)

    @pl.when(i >= n_pad)
    def _():
        h = _rms(x_ref[...], g_ref[...]).astype(BF16)
        heads = [slice(hd * SB_HD, (hd + 1) * SB_HD) for hd in range(SB_HEADS)]
        for j, (p_ref, s_ref) in enumerate(((kp_ref, ks_ref), (vp_ref, vs_ref))):
            y = _dot(h, w_ref[j])
            o16_ref[j] = y.astype(BF16)

            @pl.when(i < n_pad + n_prompt)
            def _():
                for hd, sl in enumerate(heads):
                    p_ref[hd] = y[:, sl]

            @pl.when(i >= n_pad + n_prompt)
            def _():
                for b in range(n_seq):
                    for hd, sl in enumerate(heads):
                        s_ref[b, hd] = y[b * rows:(b + 1) * rows, sl]


def _kv_proj(x, gain, w, seq, n_seq, rows):
    m, d = x.shape
    n = w.shape[-1]
    tile = n_seq * rows
    n_prompt = seq // tile
    n_pad = SB_PAD // tile
    assert n_prompt * tile == seq and seq + tile == m and n_pad * tile == SB_PAD
    row_tile = lambda i: jnp.clip(i - n_pad, 0, n_prompt - 1)
    prompt_spec = pl.BlockSpec((SB_HEADS, tile, SB_HD), lambda i: (0, row_tile(i), 0))
    sample_spec = pl.BlockSpec((n_seq, SB_HEADS, rows, SB_HD), lambda i: (0, 0, 0, 0))
    prompt_shape = jax.ShapeDtypeStruct((SB_HEADS, seq, SB_HD), F32)
    sample_shape = jax.ShapeDtypeStruct((n_seq, SB_HEADS, rows, SB_HD), F32)
    return pl.pallas_call(
        functools.partial(_kv_kernel, n_pad=n_pad, n_prompt=n_prompt, n_seq=n_seq, rows=rows),
        grid=(n_pad + n_prompt + 1,),
        in_specs=[
            pl.BlockSpec((tile, d), lambda i: (jnp.maximum(i - n_pad, 0), 0)),
            pl.BlockSpec((1, d), lambda i: (0, 0)),
            pl.BlockSpec((2, d, n), lambda i: (0, 0, 0)),
        ],
        out_specs=(prompt_spec, prompt_spec, sample_spec, sample_spec,
                   pl.BlockSpec((2, tile, n), lambda i: (0, i, 0))),
        out_shape=(prompt_shape, prompt_shape, sample_shape, sample_shape,
                   jax.ShapeDtypeStruct((2, SB_PAD + m, n), BF16)),
        compiler_params=_params(1),
        name="kv_proj",
    )(x, gain, w)


def _out_proj_kernel(x_ref, tokp_ref, toks_ref, mop_ref, mos_ref, wt_ref, wm_ref, o_ref, w16_ref,
                     *, n_prompt):
    i = pl.program_id(0)
    kt = wt_ref.shape[0]

    @pl.when(i == 0)
    def _():
        w16_ref[:kt] = wt_ref[...].astype(BF16)
        w16_ref[kt:] = wm_ref[...].astype(BF16)

    @pl.when(i < n_prompt)
    def _():
        o_ref[...] = (x_ref[...] + _dot(tokp_ref[...], w16_ref[:kt])
                      + _dot(mop_ref[...], w16_ref[kt:]))

    @pl.when(i >= n_prompt)
    def _():
        o_ref[...] = (x_ref[...] + _dot(toks_ref[...], w16_ref[:kt])
                      + _dot(mos_ref[...], w16_ref[kt:]))


def _out_proj(x, tok_p, tok_s, mo_p, mo_s, w, layer):
    m, d = x.shape
    kt = tok_p.shape[1]
    km = mo_p.shape[1]
    tile = tok_s.shape[0]
    n_prompt = tok_p.shape[0] // tile
    assert n_prompt * tile == tok_p.shape[0] and (n_prompt + 1) * tile == m
    prompt = lambda i: (jnp.minimum(i, n_prompt - 1), 0)
    return pl.pallas_call(
        functools.partial(_out_proj_kernel, n_prompt=n_prompt),
        grid=(n_prompt + 1,),
        in_specs=[
            pl.BlockSpec((tile, d), lambda i: (i, 0)),
            pl.BlockSpec((tile, kt), prompt),
            pl.BlockSpec((tile, kt), lambda i: (0, 0)),
            pl.BlockSpec((tile, km), prompt),
            pl.BlockSpec((tile, km), lambda i: (0, 0)),
            pl.BlockSpec((None, kt, d), lambda i: (layer, 0, 0)),
            pl.BlockSpec((None, km, d), lambda i: (layer, kt // km, 0)),
        ],
        out_specs=pl.BlockSpec((tile, d), lambda i: (i, 0)),
        out_shape=jax.ShapeDtypeStruct((m, d), F32),
        scratch_shapes=[pltpu.VMEM((kt + km, d), BF16)],
        compiler_params=_params(1),
        name="out_proj",
    )(x, tok_p, tok_s, mo_p, mo_s, w, w)


def _mem_kv_kernel(mem_ref, g_ref, wk_ref, wv_ref, kg_ref, mk_ref, mv_ref):
    h = _rms(mem_ref[...], g_ref[...]).astype(BF16)
    k = _dot(h, wk_ref[...].astype(BF16))
    v = _dot(h, wv_ref[...].astype(BF16))
    kg = kg_ref[...]
    for hd in range(MEM_HEADS):
        sl = slice(hd * MEM_HD, (hd + 1) * MEM_HD)
        mk_ref[hd] = _rms(k[:, sl], kg)
        mv_ref[hd] = v[:, sl]


def _mem_kv(mem, gain, wk, wv, k_gain):
    depth, d, w = wk.shape
    n = mem.shape[0]
    out_spec = pl.BlockSpec((None, MEM_HEADS, n, MEM_HD), lambda l: (l, 0, 0, 0))
    out_shape = jax.ShapeDtypeStruct((depth, MEM_HEADS, n, MEM_HD), F32)
    return pl.pallas_call(
        _mem_kv_kernel,
        grid=(depth,),
        in_specs=[
            pl.BlockSpec((n, d), lambda l: (0, 0)),
            pl.BlockSpec((None, 1, d), lambda l: (l, 0, 0)),
            pl.BlockSpec((None, d, w), lambda l: (l, 0, 0)),
            pl.BlockSpec((None, d, w), lambda l: (l, 0, 0)),
            pl.BlockSpec((None, 1, MEM_HD), lambda l: (l, 0, 0)),
        ],
        out_specs=(out_spec, out_spec),
        out_shape=(out_shape, out_shape),
        compiler_params=_params(1),
        name="mem_kv",
    )(mem, gain, wk, wv, k_gain)


def _mem_attn_heads(mq_ref, qg_ref, keys, values, masks, o_ref):
    heads = range(MEM_HEADS)
    cols = [slice(hd * MEM_HD, (hd + 1) * MEM_HD) for hd in heads]
    qg = qg_ref[...]
    qn = [_rms(mq_ref[:, cols[hd]], qg).astype(BF16) for hd in heads]
    s = [_dot_nt(qn[hd], keys[hd]) * (MEM_HD ** -0.5) for hd in heads]
    if masks is not None:
        s = [jnp.where(masks[hd], s[hd], -jnp.inf) for hd in heads]
    e = [jnp.exp(s[hd] - jnp.max(s[hd], axis=-1, keepdims=True)) for hd in heads]
    p = [(e[hd] / jnp.sum(e[hd], axis=-1, keepdims=True)).astype(BF16) for hd in heads]
    out = [_dot(p[hd], values[hd]) for hd in heads]
    for hd in heads:
        o_ref[:, cols[hd]] = out[hd].astype(BF16)


def _mem_attn_prompt_kernel(mq_ref, qg_ref, mk_ref, mv_ref, o_ref):
    _mem_attn_heads(mq_ref, qg_ref, [mk_ref[hd].astype(BF16) for hd in range(MEM_HEADS)],
                    [mv_ref[hd].astype(BF16) for hd in range(MEM_HEADS)], None, o_ref)


def _mem_attn_sample_kernel(mq_ref, qg_ref, mk_ref, mv_ref, o_ref):
    k16 = mk_ref[...].astype(BF16)
    v16 = mv_ref[...].astype(BF16)
    row_head = lax.broadcasted_iota(jnp.int32, (mq_ref.shape[0], k16.shape[0]), 1) % MEM_HEADS
    _mem_attn_heads(mq_ref, qg_ref, [k16] * MEM_HEADS, [v16] * MEM_HEADS,
                    [row_head == hd for hd in range(MEM_HEADS)], o_ref)


def _mem_attn(proj, col_block, row_block0, rows, n_tiles, q_gain, layer, mk, mv, kv_index0, per_tile_kv):
    w = MEM_HEADS * MEM_HD
    if per_tile_kv:
        body = _mem_attn_sample_kernel
        kv_spec = pl.BlockSpec((None,) + mk.shape[1:], lambda b: (kv_index0 + b, 0, 0))
    else:
        body = _mem_attn_prompt_kernel
        kv_spec = pl.BlockSpec((None,) + mk.shape[1:], lambda b: (kv_index0, 0, 0, 0))
    return pl.pallas_call(
        body,
        grid=(n_tiles,),
        in_specs=[
            pl.BlockSpec((rows, w), lambda b: (row_block0 + b, col_block)),
            pl.BlockSpec((None, 1, MEM_HD), lambda b: (layer, 0, 0)),
            kv_spec,
            kv_spec,
        ],
        out_specs=pl.BlockSpec((rows, w), lambda b: (b, 0)),
        out_shape=jax.ShapeDtypeStruct((n_tiles * rows, w), BF16),
        compiler_params=_params(1),
        name="mem_attn",
    )(proj, q_gain, mk, mv)


def _mlstm_kernel(q_ref, k_ref, v_ref, og_ref, gc_ref, gr_ref, bc_ref, br_ref, hn_ref,
                  c0_ref, n0_ref, m0_ref, tok_ref, c_ref, n_ref, m_ref, *, chunk):
    L = chunk

    @pl.when(pl.program_id(1) == 0)
    def _():
        c_ref[...] = c0_ref[...]
        n_ref[...] = n0_ref[...]
        m_ref[...] = m0_ref[...]

    row = lax.broadcasted_iota(jnp.int32, (L, L), 0)
    col = lax.broadcasted_iota(jnp.int32, (L, L), 1)
    causal = col <= row
    tri = jnp.where(causal, 1.0, 0.0).astype(BF16)

    pre_c = gc_ref[...] + bc_ref[...]
    pre_r = gr_ref[...] + br_ref[...]
    lf_c = _log_sigmoid(pre_c)
    lf_r = _log_sigmoid(pre_r)
    cum_c = sum(_dot(tri, part) for part in _split3(lf_c))
    cum_r = sum(_dot_nt(part, tri) for part in _split3(lf_r))

    scale = A_DQK ** -0.5
    heads = range(A_HEADS)
    qk = [slice(h * A_DQK, (h + 1) * A_DQK) for h in heads]
    vv = [slice(h * A_DV, (h + 1) * A_DV) for h in heads]
    q = [q_ref[:, qk[h]] * scale for h in heads]
    q16 = [q[h].astype(BF16) for h in heads]
    k16 = [k_ref[:, qk[h]].astype(BF16) for h in heads]
    c_old = [c_ref[h] for h in heads]
    n_old = [n_ref[h:h + 1, :] for h in heads]
    m_old = [m_ref[h:h + 1, 0:1] for h in heads]
    i_c = [pre_c[:, h:h + 1] for h in heads]
    b_c = [cum_c[:, A_HEADS + h:A_HEADS + h + 1] for h in heads]
    i_r = [pre_r[h:h + 1, :] for h in heads]
    b_r = [cum_r[A_HEADS + h:A_HEADS + h + 1, :] for h in heads]

    qk_t = [_dot_nt(q16[h], k16[h]) for h in heads]
    q_c = [_dot_nt(q16[h], c_old[h].astype(BF16)) for h in heads]

    s, w_st, m_t = [], [], []
    for h in heads:
        d = jnp.where(causal, b_c[h] - b_r[h] + i_r[h], -jnp.inf)
        inter = b_c[h] + m_old[h]
        m_t.append(jnp.maximum(inter, jnp.max(d, axis=-1, keepdims=True)))
        w_st.append(jnp.exp(inter - m_t[h]))
        s.append(qk_t[h] * jnp.exp(d - m_t[h]))
    s_v = [_dot(s[h].astype(BF16), v_ref[:, vv[h]].astype(BF16)) for h in heads]

    for h in heads:
        num = s_v[h] + w_st[h] * q_c[h]
        den = (jnp.sum(s[h], axis=-1, keepdims=True)
               + w_st[h] * jnp.sum(q[h] * n_old[h], axis=-1, keepdims=True))
        hh = num * (1.0 / jnp.maximum(jnp.abs(den), jnp.exp(-m_t[h])))
        out = _rms(hh, hn_ref[:, vv[h]]) * jax.nn.sigmoid(og_ref[:, vv[h]])
        tok_ref[:, vv[h]] = out.astype(BF16)

    decay, w_k = [], []
    for h in heads:
        b_end = b_c[h][L - 1:L, :]
        g = b_end - b_c[h] + i_c[h]
        m_new = jnp.maximum(b_end + m_old[h], jnp.max(g, axis=0, keepdims=True))
        w_k.append(jnp.exp(g - m_new))
        decay.append(jnp.exp(b_end + m_old[h] - m_new))
        m_ref[h:h + 1, :] = jnp.broadcast_to(m_new, (1, LANES))
    vw_k = [lax.dot_general((v_ref[:, vv[h]] * w_k[h]).astype(BF16), k16[h], TN_DIMS,
                            preferred_element_type=F32) for h in heads]
    for h in heads:
        c_ref[h] = decay[h] * c_old[h] + vw_k[h]
        n_ref[h:h + 1, :] = (decay[h] * n_old[h]
                             + jnp.sum(w_k[h] * k_ref[:, qk[h]], axis=0, keepdims=True))


def _mlstm(proj, gates_r, bias_c, bias_r, head_norm, c0, n0, m0, row_block0, chunk, n_chunks):
    n_seq = c0.shape[0]
    aq = A_HEADS * A_DQK
    av = A_HEADS * A_DV
    gate_block = (2 * aq + 2 * av + MEM_HEADS * MEM_HD) // LANES

    def rows(b, c):
        return row_block0 + b * n_chunks + c

    state = lambda b, c: (b, 0, 0)
    return pl.pallas_call(
        functools.partial(_mlstm_kernel, chunk=chunk),
        grid=(n_seq, n_chunks),
        in_specs=[
            pl.BlockSpec((chunk, aq), lambda b, c: (rows(b, c), 0)),
            pl.BlockSpec((chunk, aq), lambda b, c: (rows(b, c), 1)),
            pl.BlockSpec((chunk, av), lambda b, c: (rows(b, c), 1)),
            pl.BlockSpec((chunk, av), lambda b, c: (rows(b, c), 2)),
            pl.BlockSpec((chunk, LANES), lambda b, c: (rows(b, c), gate_block)),
            pl.BlockSpec((None, GATE_ROWS, chunk), lambda b, c: (b, 0, c)),
            pl.BlockSpec((1, LANES), lambda b, c: (0, 0)),
            pl.BlockSpec((GATE_ROWS, 1), lambda b, c: (0, 0)),
            pl.BlockSpec((1, av), lambda b, c: (0, 0)),
            pl.BlockSpec((None, A_HEADS, A_DV, A_DQK), lambda b, c: (b, 0, 0, 0)),
            pl.BlockSpec((None, SUBLANES, A_DQK), state),
            pl.BlockSpec((None, SUBLANES, LANES), state),
        ],
        out_specs=(
            pl.BlockSpec((chunk, av), lambda b, c: (b * n_chunks + c, 0)),
            pl.BlockSpec((None, A_HEADS, A_DV, A_DQK), lambda b, c: (b, 0, 0, 0)),
            pl.BlockSpec((None, SUBLANES, A_DQK), state),
            pl.BlockSpec((None, SUBLANES, LANES), state),
        ),
        out_shape=(
            jax.ShapeDtypeStruct((n_seq * n_chunks * chunk, av), BF16),
            jax.ShapeDtypeStruct((n_seq, A_HEADS, A_DV, A_DQK), F32),
            jax.ShapeDtypeStruct((n_seq, SUBLANES, A_DQK), F32),
            jax.ShapeDtypeStruct((n_seq, SUBLANES, LANES), F32),
        ),
        compiler_params=_params(2),
        name="mlstm",
    )(proj, proj, proj, proj, proj, gates_r, bias_c, bias_r, head_norm, c0, n0, m0)


def _sb_scores(q16, k16, valid):
    z = _dot_nt(q16, k16) * (SB_HD ** -0.5)
    sp = jnp.maximum(z, 0.0) + jnp.log(1.0 + jnp.exp(-jnp.abs(z)))
    if valid is not None:
        sp = jnp.where(valid, sp, 0.0)
    return z, sp


def _sb_newer(sp, upper):
    s1 = sp.astype(BF16)
    s2 = (sp - s1.astype(F32)).astype(BF16)
    return _dot(s1, upper) + _dot(s2, upper)


def _sb_weights(z, sp, newer, valid, r_prev):
    a = jnp.exp(z - sp - newer + r_prev)
    if valid is not None:
        a = jnp.where(valid, a, 0.0)
    return a.astype(BF16)


def _sb_tiles(qs, ks, vs, upper, valid, r_prevs):
    scores = [_sb_scores(q, k, valid) for q, k in zip(qs, ks)]
    newer = [_sb_newer(sp, upper) for _, sp in scores]
    outs = [_dot(_sb_weights(z, sp, nw, valid, r), v)
            for (z, sp), nw, r, v in zip(scores, newer, r_prevs, vs)]
    sums = [r - jnp.sum(sp, axis=-1, keepdims=True) for (_, sp), r in zip(scores, r_prevs)]
    return outs, sums


def _upper(n):
    row = lax.broadcasted_iota(jnp.int32, (n, n), 0)
    col = lax.broadcasted_iota(jnp.int32, (n, n), 1)
    return jnp.where(row > col, 1.0, 0.0).astype(BF16)


def _sb_prompt_kernel(q_ref, k_ref, v_ref, o_ref, q16_ref, acc_ref, r_ref):
    t = SB_TILE
    base = pl.program_id(1) * SB_CHAINS + SB_PAD // SB_TILE
    upper = _upper(t)
    row = lax.broadcasted_iota(jnp.int32, (t, t), 0)
    col = lax.broadcasted_iota(jnp.int32, (t, t), 1)
    q16_ref[...] = q_ref[...].astype(BF16)

    def walk(j, diagonal):
        rows = [slice(c * t, (c + 1) * t) for c in range(SB_CHAINS)]
        starts = [pl.multiple_of((base + c - j) * t, t) for c in range(SB_CHAINS)]
        outs, sums = _sb_tiles(
            [q16_ref[r, :] for r in rows],
            [k_ref[pl.ds(s, t), :] for s in starts],
            [v_ref[pl.ds(s, t), :] for s in starts],
            upper, col < row if diagonal else None,
            [jnp.zeros((t, 1), F32) if diagonal else r_ref[r, :] for r in rows])
        r_max = None
        for r, out, total in zip(rows, outs, sums):
            if diagonal:
                acc_ref[r, :] = out
            else:
                acc_ref[r, :] += out
            r_ref[r, :] = total
            r_c = jnp.max(total)
            r_max = r_c if r_max is None else jnp.maximum(r_max, r_c)
        return r_max

    def cond(carry):
        j, r_max = carry
        return jnp.logical_and(j <= base, r_max > EXP_ZERO_BELOW)

    def body(carry):
        j, _ = carry
        return j + 1, walk(j, False)

    lax.while_loop(cond, body, (jnp.int32(1), walk(0, True)))
    o_ref[...] = acc_ref[...].astype(BF16)


def _sb_prompt(proj, kv16, seq):
    step = SB_CHAINS * SB_TILE
    return pl.pallas_call(
        _sb_prompt_kernel,
        grid=(SB_HEADS, seq // step),
        in_specs=[
            pl.BlockSpec((step, SB_HD), lambda h, i: (i, h)),
            pl.BlockSpec((None, SB_PAD + seq, SB_HD), lambda h, i: (0, 0, h)),
            pl.BlockSpec((None, SB_PAD + seq, SB_HD), lambda h, i: (1, 0, h)),
        ],
        out_specs=pl.BlockSpec((step, SB_HD), lambda h, i: (i, h)),
        out_shape=jax.ShapeDtypeStruct((seq, SB_HEADS * SB_HD), BF16),
        scratch_shapes=[pltpu.VMEM((step, SB_HD), BF16), pltpu.VMEM((step, SB_HD), F32),
                        pltpu.VMEM((step, 1), F32)],
        compiler_params=_params(2),
        name="sb_prompt",
    )(proj, kv16, kv16)


def _sb_sample_kernel(q_ref, kn_ref, vn_ref, pk_hbm, pv_hbm, o_ref, kbuf, vbuf, sem, acc_ref, r_ref,
                      *, rows, n_past):
    b = pl.program_id(0)
    t = SB_PAST_TILE

    def copies(tile_index):
        start = pl.multiple_of(tile_index * t, t)
        return (
            pltpu.make_async_copy(pk_hbm.at[b, :, pl.ds(start, t), :], kbuf, sem.at[0]),
            pltpu.make_async_copy(pv_hbm.at[b, :, pl.ds(start, t), :], vbuf, sem.at[1]),
        )

    def fetch(tile_index):
        for cp in copies(tile_index):
            cp.start()

    def wait(tile_index):
        for cp in copies(tile_index):
            cp.wait()

    heads = [slice(h * SB_HD, (h + 1) * SB_HD) for h in range(SB_HEADS)]

    def past_tile():
        outs, sums = _sb_tiles(
            [q_ref[:, sl].astype(BF16) for sl in heads],
            [kbuf[h].astype(BF16) for h in range(SB_HEADS)],
            [vbuf[h].astype(BF16) for h in range(SB_HEADS)],
            _upper(t), None, [r_ref[h] for h in range(SB_HEADS)])
        r_max = None
        for h, (out, total) in enumerate(zip(outs, sums)):
            acc_ref[:, heads[h]] += out
            r_ref[h] = total
            r_h = jnp.max(total)
            r_max = r_h if r_max is None else jnp.maximum(r_max, r_h)
        return r_max

    fetch(n_past - 1)
    row = lax.broadcasted_iota(jnp.int32, (rows, rows), 0)
    col = lax.broadcasted_iota(jnp.int32, (rows, rows), 1)
    outs, sums = _sb_tiles(
        [q_ref[:, sl].astype(BF16) for sl in heads], [kn_ref[:, sl] for sl in heads],
        [vn_ref[:, sl] for sl in heads], _upper(rows), col < row,
        [jnp.zeros((rows, 1), F32)] * SB_HEADS)
    for h, (out, total) in enumerate(zip(outs, sums)):
        acc_ref[:, heads[h]] = out
        r_ref[h] = total
    wait(n_past - 1)
    r_max = past_tile()

    def cond(carry):
        tile_index, r_max = carry
        return jnp.logical_and(tile_index >= 0, r_max > EXP_ZERO_BELOW)

    def body(carry):
        tile_index, _ = carry
        fetch(tile_index)
        wait(tile_index)
        return tile_index - 1, past_tile()

    lax.while_loop(cond, body, (jnp.int32(n_past - 2), r_max))
    o_ref[...] = acc_ref[...].astype(BF16)


def _sb_sample(proj, kv16, past_k, past_v, row_block0, rows):
    n_seq, heads, past, hd = past_k.shape
    w = heads * hd
    new_block0 = row_block0 + SB_PAD // rows
    return pl.pallas_call(
        functools.partial(_sb_sample_kernel, rows=rows, n_past=past // SB_PAST_TILE),
        grid=(n_seq,),
        in_specs=[
            pl.BlockSpec((rows, w), lambda b: (row_block0 + b, 0)),
            pl.BlockSpec((None, rows, w), lambda b: (0, new_block0 + b, 0)),
            pl.BlockSpec((None, rows, w), lambda b: (1, new_block0 + b, 0)),
            pl.BlockSpec(memory_space=pl.ANY),
            pl.BlockSpec(memory_space=pl.ANY),
        ],
        out_specs=pl.BlockSpec((rows, w), lambda b: (b, 0)),
        out_shape=jax.ShapeDtypeStruct((n_seq * rows, w), BF16),
        scratch_shapes=[
            pltpu.VMEM((heads, SB_PAST_TILE, hd), F32),
            pltpu.VMEM((heads, SB_PAST_TILE, hd), F32),
            pltpu.SemaphoreType.DMA((2,)),
            pltpu.VMEM((rows, w), F32),
            pltpu.VMEM((heads, rows, 1), F32),
        ],
        compiler_params=_params(1),
        name="sb_sample",
    )(proj, kv16, kv16, past_k, past_v)


def _pad_rows(a, n):
    return jnp.pad(a, [(0, 0)] * (a.ndim - 2) + [(0, n - a.shape[-2]), (0, 0)])


def kernel(x_prompt, x_sample, state_mlstm_C, state_mlstm_n, state_mlstm_m, cache_sb_k, cache_sb_v,
           cache_mem_k, cache_mem_v, mem_prompt, ffn1_norm, ffn1_w_gate, ffn1_w_up, ffn1_w_down,
           ffn2_norm, ffn2_w_gate, ffn2_w_up, ffn2_w_down, mix_norm, a_w_in, a_b_i, a_b_f, a_head_norm,
           b_w_in, w_out, mem_norm, mem_w_k, mem_w_v, mem_q_norm, mem_k_norm, kv_norm, sb_w_k, sb_w_v):
    n_pb, seq, d = x_prompt.shape
    n_sb, dec_seq, _ = x_sample.shape
    assert n_pb == 1
    depth = ffn1_norm.shape[0]
    n_a = a_w_in.shape[0]
    n_slots = mem_prompt.shape[1]
    aq = A_HEADS * A_DQK
    av = A_HEADS * A_DV
    mem_w = MEM_HEADS * MEM_HD
    sb_w = SB_HEADS * SB_HD
    n_sample = n_sb * dec_seq
    sample_block0 = seq // dec_seq

    gains = lambda g: g.reshape(g.shape[0], 1, g.shape[-1])
    ffn1 = (gains(ffn1_norm), ffn1_w_gate, ffn1_w_up, ffn1_w_down)
    ffn2 = (gains(ffn2_norm), ffn2_w_gate, ffn2_w_up, ffn2_w_down)
    mix_gain = gains(mix_norm)
    n_main = 2 * aq + 2 * av
    w_gates = a_w_in[:, :, n_main:n_main + 2 * A_HEADS]
    a_cols = n_main + mem_w + LANES
    a_tile = 1792
    a_pad = -a_cols % a_tile
    a_w = jnp.concatenate(
        [a_w_in[:, :, :n_main], a_w_in[:, :, n_main + 2 * A_HEADS:], w_gates,
         jnp.zeros((n_a, d, LANES - 2 * A_HEADS + a_pad), F32)], axis=-1).astype(BF16)
    a_wgt = _pad_rows(jnp.swapaxes(w_gates, 1, 2), GATE_ROWS).astype(BF16)
    gate_bias = jnp.concatenate([a_b_i, a_b_f], axis=-1)
    b_w = b_w_in.astype(BF16)
    w_kv = jnp.stack([sb_w_k, sb_w_v]).astype(BF16)
    q_gain = mem_q_norm.reshape(depth, 1, MEM_HD)

    mk_p, mv_p = _mem_kv(mem_prompt[0], gains(mem_norm), mem_w_k, mem_w_v,
                         mem_k_norm.reshape(depth, 1, MEM_HD))
    mk_s = cache_mem_k.reshape(depth * n_sb, n_slots * MEM_HEADS, MEM_HD)
    mv_s = cache_mem_v.reshape(depth * n_sb, n_slots * MEM_HEADS, MEM_HD)

    past_k = jnp.transpose(cache_sb_k, (0, 2, 1, 3))
    past_v = jnp.transpose(cache_sb_v, (0, 2, 1, 3))

    x = jnp.concatenate([x_prompt[0], x_sample.reshape(n_sample, d)], axis=0)
    c_p, n_p, m_p, c_s, n_s, m_s = [], [], [], [], [], []
    k_p = v_p = k_s = v_s = kv16 = None
    mem_tile = 1024
    for l in range(depth):
        x = _ffn(x, *ffn1, l)
        if l < n_a:
            proj, gates_r = _proj(x, mix_gain, l, a_w, l, a_tile, wgt=a_wgt)
            bias_c = jnp.pad(gate_bias[l], (0, LANES - 2 * A_HEADS)).reshape(1, LANES)
            bias_r = jnp.pad(gate_bias[l], (0, GATE_ROWS - 2 * A_HEADS)).reshape(GATE_ROWS, 1)
            head_norm = a_head_norm[l].reshape(1, av)
            chunk_p = 128
            tok_p, c, n, m = _mlstm(
                proj, gates_r[:, :seq].reshape(1, GATE_ROWS, seq), bias_c, bias_r, head_norm,
                jnp.zeros((1, A_HEADS, A_DV, A_DQK), F32), jnp.zeros((1, SUBLANES, A_DQK), F32),
                jnp.zeros((1, SUBLANES, LANES), F32), 0, chunk_p, seq // chunk_p)
            c_p.append(c); n_p.append(n[:, :A_HEADS]); m_p.append(m[:, :A_HEADS, 0])
            gates_s = gates_r[:, seq:].reshape(GATE_ROWS, n_sb, dec_seq).transpose(1, 0, 2)
            m0 = jnp.broadcast_to(state_mlstm_m[l][:, :, None], (n_sb, A_HEADS, LANES))
            tok_s, c, n, m = _mlstm(
                proj, gates_s, bias_c, bias_r, head_norm, state_mlstm_C[l],
                _pad_rows(state_mlstm_n[l], SUBLANES), _pad_rows(m0, SUBLANES), sample_block0, dec_seq, 1)
            c_s.append(c); n_s.append(n[:, :A_HEADS]); m_s.append(m[:, :A_HEADS, 0])
            mq_block = n_main // mem_w
        else:
            proj = _proj(x, mix_gain, l, b_w, l - n_a, b_w.shape[-1])
            tok_p = _sb_prompt(proj, kv16, seq)
            tok_s = _sb_sample(proj, kv16, past_k, past_v, sample_block0, dec_seq)
            mq_block = sb_w // mem_w
        mo_p = _mem_attn(proj, mq_block, 0, mem_tile, seq // mem_tile, q_gain, l, mk_p, mv_p, l, False)
        mo_s = _mem_attn(proj, mq_block, sample_block0, dec_seq, n_sb, q_gain, l, mk_s, mv_s, l * n_sb, True)
        x = _out_proj(x, tok_p, tok_s, mo_p, mo_s, w_out, l)
        x = _ffn(x, *ffn2, l)
        if l == n_a - 1:
            k_p, v_p, k_s, v_s, kv16 = _kv_proj(x, kv_norm.reshape(1, d), w_kv, seq, n_sb, dec_seq)

    y_prompt = x[:seq].reshape(1, seq, d)
    y_sample = x[seq:].reshape(n_sb, dec_seq, d)
    k_p, v_p = (jnp.transpose(a, (1, 0, 2))[None] for a in (k_p, v_p))
    k_s, v_s = (jnp.transpose(a, (0, 2, 1, 3)) for a in (k_s, v_s))
    mem_out = lambda a: jnp.transpose(a, (0, 2, 1, 3))[:, None]
    return (y_prompt, y_sample, jnp.stack(c_p), jnp.stack(n_p), jnp.stack(m_p), k_p, v_p,
            mem_out(mk_p), mem_out(mv_p),
            jnp.stack(c_s), jnp.stack(n_s), jnp.stack(m_s), k_s, v_s)
```

```python
import functools

import jax
import jax.numpy as jnp
from jax import lax
from jax.experimental import pallas as pl
from jax.experimental.pallas import tpu as pltpu

F32 = jnp.float32
BF16 = jnp.bfloat16

RMS_EPS = 1e-6
A_HEADS = 6
A_DQK = 128
A_DV = 256
SB_HEADS = 12
SB_HD = 128
MEM_HEADS = 4
MEM_HD = 128
LANES = 128
SUBLANES = 8
GATE_ROWS = 16

ROW_TILE = 768
FFN_ROW_TILE = 1056
FF_TILE = 512
FF_HEAD_TILE = 256
VMEM_LIMIT = 58 * 1024 * 1024

SB_TILE = 256
SB_CHAINS = 4
SB_PAST_TILE = 256
SB_PAD = 768

NT_DIMS = (((1,), (1,)), ((), ()))
TN_DIMS = (((0,), (0,)), ((), ()))
EXP_ZERO_BELOW = -104.0


def _params(n_axes):
    return pltpu.CompilerParams(
        dimension_semantics=("arbitrary",) * n_axes, vmem_limit_bytes=VMEM_LIMIT)


def _rms(x, g):
    ms = jnp.mean(x * x, axis=-1, keepdims=True)
    return x * lax.rsqrt(ms + RMS_EPS) * g


def _log_sigmoid(x):
    return jnp.minimum(x, 0.0) - jnp.log1p(jnp.exp(-jnp.abs(x)))


def _dot(a, b):
    return jnp.dot(a, b, preferred_element_type=F32)


def _dot_nt(a, b):
    return lax.dot_general(a, b, NT_DIMS, preferred_element_type=F32)


def _split3(x):
    x1 = x.astype(BF16)
    r = x - x1.astype(F32)
    x2 = r.astype(BF16)
    x3 = (r - x2.astype(F32)).astype(BF16)
    return x1, x2, x3


def _ffn_step(h_ref, wg, wu, wd, o_ref):
    h = h_ref[...]
    g = _dot(h, wg)
    u = _dot(h, wu)
    a = (g * jax.nn.sigmoid(g) * u * 0.5).astype(BF16)
    o_ref[...] += _dot(a, wd)


def _ffn_head_kernel(x_ref, g_ref, wg_ref, wu_ref, wd_ref, o_ref, wg16_ref, wu16_ref, wd16_ref, h_ref):
    @pl.when(pl.program_id(0) == 0)
    def _():
        x = x_ref[...]
        h_ref[...] = _rms(x, g_ref[...]).astype(BF16)
        o_ref[...] = x

    wg16_ref[...] = wg_ref[...].astype(BF16)
    wu16_ref[...] = wu_ref[...].astype(BF16)
    wd16_ref[...] = wd_ref[...].astype(BF16)
    _ffn_step(h_ref, wg16_ref[...], wu16_ref[...], wd16_ref[...], o_ref)


def _ffn_rest_kernel(x_ref, g_ref, wg_ref, wu_ref, wd_ref, o_ref, h_ref):
    @pl.when(pl.program_id(1) == 0)
    def _():
        x = x_ref[...]
        h_ref[...] = _rms(x, g_ref[...]).astype(BF16)
        o_ref[...] = x

    _ffn_step(h_ref, wg_ref[...], wu_ref[...], wd_ref[...], o_ref)


def _ffn(x, gain, wg, wu, wd, layer):
    m, d = x.shape
    f = wg.shape[-1]
    n_head = f // FF_HEAD_TILE
    w16 = lambda shape: jax.ShapeDtypeStruct(shape, BF16)
    x, wg16, wu16, wd16 = pl.pallas_call(
        _ffn_head_kernel,
        grid=(n_head,),
        in_specs=[
            pl.BlockSpec((FFN_ROW_TILE, d), lambda j: (0, 0)),
            pl.BlockSpec((None, 1, d), lambda j: (layer, 0, 0)),
            pl.BlockSpec((None, d, FF_HEAD_TILE), lambda j: (layer, 0, j)),
            pl.BlockSpec((None, d, FF_HEAD_TILE), lambda j: (layer, 0, j)),
            pl.BlockSpec((None, FF_HEAD_TILE, d), lambda j: (layer, j, 0)),
        ],
        out_specs=(
            pl.BlockSpec((FFN_ROW_TILE, d), lambda j: (0, 0)),
            pl.BlockSpec((d, FF_HEAD_TILE), lambda j: (0, j)),
            pl.BlockSpec((d, FF_HEAD_TILE), lambda j: (0, j)),
            pl.BlockSpec((FF_HEAD_TILE, d), lambda j: (j, 0)),
        ),
        out_shape=(jax.ShapeDtypeStruct((m, d), F32), w16((d, f)), w16((d, f)), w16((f, d))),
        scratch_shapes=[pltpu.VMEM((FFN_ROW_TILE, d), BF16)],
        input_output_aliases={0: 0},
        compiler_params=_params(1),
        name="ffn_head",
    )(x, gain, wg, wu, wd)
    return pl.pallas_call(
        _ffn_rest_kernel,
        grid=(m // FFN_ROW_TILE - 1, f // FF_TILE),
        in_specs=[
            pl.BlockSpec((FFN_ROW_TILE, d), lambda i, j: (i + 1, 0)),
            pl.BlockSpec((None, 1, d), lambda i, j: (layer, 0, 0)),
            pl.BlockSpec((d, FF_TILE), lambda i, j: (0, j)),
            pl.BlockSpec((d, FF_TILE), lambda i, j: (0, j)),
            pl.BlockSpec((FF_TILE, d), lambda i, j: (j, 0)),
        ],
        out_specs=pl.BlockSpec((FFN_ROW_TILE, d), lambda i, j: (i + 1, 0)),
        out_shape=jax.ShapeDtypeStruct((m, d), F32),
        scratch_shapes=[pltpu.VMEM((FFN_ROW_TILE, d), BF16)],
        input_output_aliases={0: 0},
        compiler_params=_params(2),
        name="ffn",
    )(x, gain, wg16, wu16, wd16)


def _proj_kernel(x_ref, g_ref, w_ref, o_ref, h_ref):
    @pl.when(pl.program_id(1) == 0)
    def _():
        h_ref[...] = _rms(x_ref[...], g_ref[...]).astype(BF16)

    o_ref[...] = _dot(h_ref[...], w_ref[...]).astype(o_ref.dtype)


def _proj_gates_kernel(x_ref, g_ref, w_ref, wgt_ref, o_ref, gt_ref, h_ref):
    @pl.when(pl.program_id(1) == 0)
    def _():
        h = _rms(x_ref[...], g_ref[...]).astype(BF16)
        h_ref[...] = h
        gt_ref[...] = _dot_nt(wgt_ref[...], h)

    o_ref[...] = _dot(h_ref[...], w_ref[...]).astype(o_ref.dtype)


def _proj(x, gain, gain_layer, w, layer, col_tile, wgt=None):
    m, d = x.shape
    n = w.shape[-1]
    in_specs = [
        pl.BlockSpec((ROW_TILE, d), lambda i, j: (i, 0)),
        pl.BlockSpec((None, 1, d), lambda i, j: (gain_layer, 0, 0)),
        pl.BlockSpec((None, d, col_tile), lambda i, j: (layer, 0, j)),
    ]
    out_spec = pl.BlockSpec((ROW_TILE, col_tile), lambda i, j: (i, j))
    out_shape = jax.ShapeDtypeStruct((m, n), F32)
    common = dict(
        grid=(m // ROW_TILE, n // col_tile),
        scratch_shapes=[pltpu.VMEM((ROW_TILE, d), BF16)],
        compiler_params=_params(2),
    )
    if wgt is None:
        return pl.pallas_call(
            _proj_kernel, in_specs=in_specs, out_specs=out_spec, out_shape=out_shape,
            name="proj", **common)(x, gain, w)
    return pl.pallas_call(
        _proj_gates_kernel,
        in_specs=in_specs + [pl.BlockSpec((None, GATE_ROWS, d), lambda i, j: (layer, 0, 0))],
        out_specs=(out_spec, pl.BlockSpec((GATE_ROWS, ROW_TILE), lambda i, j: (0, i))),
        out_shape=(out_shape, jax.ShapeDtypeStruct((GATE_ROWS, m), F32)),
        name="proj_gates", **common)(x, gain, w, wgt)


def _kv_kernel(x_ref, g_ref, w_ref, kp_ref, vp_ref, ks_ref, vs_ref, o16_ref,
               *, n_pad, n_prompt, n_seq, rows):
    i = pl.program_id(0)

    @pl.when(i < n_pad)
    def _():
        o16_ref[...] = jnp.zeros(o16_ref.shape, BF16)

    @pl.when(i >= n_pad)
    def _():
        h = _rms(x_ref[...], g_ref[...]).astype(BF16)
        heads = [slice(hd * SB_HD, (hd + 1) * SB_HD) for hd in range(SB_HEADS)]
        for j, (p_ref, s_ref) in enumerate(((kp_ref, ks_ref), (vp_ref, vs_ref))):
            y = _dot(h, w_ref[j])
            o16_ref[j] = y.astype(BF16)

            @pl.when(i < n_pad + n_prompt)
            def _():
                for hd, sl in enumerate(heads):
                    p_ref[hd] = y[:, sl]

            @pl.when(i >= n_pad + n_prompt)
            def _():
                for b in range(n_seq):
                    for hd, sl in enumerate(heads):
                        s_ref[b, hd] = y[b * rows:(b + 1) * rows, sl]


def _kv_proj(x, gain, w, seq, n_seq, rows):
    m, d = x.shape
    n = w.shape[-1]
    tile = n_seq * rows
    n_prompt = seq // tile
    n_pad = SB_PAD // tile
    assert n_prompt * tile == seq and seq + tile == m and n_pad * tile == SB_PAD
    row_tile = lambda i: jnp.clip(i - n_pad, 0, n_prompt - 1)
    prompt_spec = pl.BlockSpec((SB_HEADS, tile, SB_HD), lambda i: (0, row_tile(i), 0))
    sample_spec = pl.BlockSpec((n_seq, SB_HEADS, rows, SB_HD), lambda i: (0, 0, 0, 0))
    prompt_shape = jax.ShapeDtypeStruct((SB_HEADS, seq, SB_HD), F32)
    sample_shape = jax.ShapeDtypeStruct((n_seq, SB_HEADS, rows, SB_HD), F32)
    return pl.pallas_call(
        functools.partial(_kv_kernel, n_pad=n_pad, n_prompt=n_prompt, n_seq=n_seq, rows=rows),
        grid=(n_pad + n_prompt + 1,),
        in_specs=[
            pl.BlockSpec((tile, d), lambda i: (jnp.maximum(i - n_pad, 0), 0)),
            pl.BlockSpec((1, d), lambda i: (0, 0)),
            pl.BlockSpec((2, d, n), lambda i: (0, 0, 0)),
        ],
        out_specs=(prompt_spec, prompt_spec, sample_spec, sample_spec,
                   pl.BlockSpec((2, tile, n), lambda i: (0, i, 0))),
        out_shape=(prompt_shape, prompt_shape, sample_shape, sample_shape,
                   jax.ShapeDtypeStruct((2, SB_PAD + m, n), BF16)),
        compiler_params=_params(1),
        name="kv_proj",
    )(x, gain, w)


def _out_proj_kernel(x_ref, tokp_ref, toks_ref, mop_ref, mos_ref, wt_ref, wm_ref, o_ref, w16_ref,
                     *, n_prompt):
    i = pl.program_id(0)
    kt = wt_ref.shape[0]

    @pl.when(i == 0)
    def _():
        w16_ref[:kt] = wt_ref[...].astype(BF16)
        w16_ref[kt:] = wm_ref[...].astype(BF16)

    @pl.when(i < n_prompt)
    def _():
        o_ref[...] = (x_ref[...] + _dot(tokp_ref[...], w16_ref[:kt])
                      + _dot(mop_ref[...], w16_ref[kt:]))

    @pl.when(i >= n_prompt)
    def _():
        o_ref[...] = (x_ref[...] + _dot(toks_ref[...], w16_ref[:kt])
                      + _dot(mos_ref[...], w16_ref[kt:]))


def _out_proj(x, tok_p, tok_s, mo_p, mo_s, w, layer):
    m, d = x.shape
    kt = tok_p.shape[1]
    km = mo_p.shape[1]
    tile = tok_s.shape[0]
    n_prompt = tok_p.shape[0] // tile
    assert n_prompt * tile == tok_p.shape[0] and (n_prompt + 1) * tile == m
    prompt = lambda i: (jnp.minimum(i, n_prompt - 1), 0)
    return pl.pallas_call(
        functools.partial(_out_proj_kernel, n_prompt=n_prompt),
        grid=(n_prompt + 1,),
        in_specs=[
            pl.BlockSpec((tile, d), lambda i: (i, 0)),
            pl.BlockSpec((tile, kt), prompt),
            pl.BlockSpec((tile, kt), lambda i: (0, 0)),
            pl.BlockSpec((tile, km), prompt),
            pl.BlockSpec((tile, km), lambda i: (0, 0)),
            pl.BlockSpec((None, kt, d), lambda i: (layer, 0, 0)),
            pl.BlockSpec((None, km, d), lambda i: (layer, kt // km, 0)),
        ],
        out_specs=pl.BlockSpec((tile, d), lambda i: (i, 0)),
        out_shape=jax.ShapeDtypeStruct((m, d), F32),
        scratch_shapes=[pltpu.VMEM((kt + km, d), BF16)],
        compiler_params=_params(1),
        name="out_proj",
    )(x, tok_p, tok_s, mo_p, mo_s, w, w)


def _mem_kv_kernel(mem_ref, g_ref, wk_ref, wv_ref, kg_ref, mk_ref, mv_ref):
    h = _rms(mem_ref[...], g_ref[...]).astype(BF16)
    k = _dot(h, wk_ref[...].astype(BF16))
    v = _dot(h, wv_ref[...].astype(BF16))
    kg = kg_ref[...]
    for hd in range(MEM_HEADS):
        sl = slice(hd * MEM_HD, (hd + 1) * MEM_HD)
        mk_ref[hd] = _rms(k[:, sl], kg)
        mv_ref[hd] = v[:, sl]


def _mem_kv(mem, gain, wk, wv, k_gain):
    depth, d, w = wk.shape
    n = mem.shape[0]
    out_spec = pl.BlockSpec((None, MEM_HEADS, n, MEM_HD), lambda l: (l, 0, 0, 0))
    out_shape = jax.ShapeDtypeStruct((depth, MEM_HEADS, n, MEM_HD), F32)
    return pl.pallas_call(
        _mem_kv_kernel,
        grid=(depth,),
        in_specs=[
            pl.BlockSpec((n, d), lambda l: (0, 0)),
            pl.BlockSpec((None, 1, d), lambda l: (l, 0, 0)),
            pl.BlockSpec((None, d, w), lambda l: (l, 0, 0)),
            pl.BlockSpec((None, d, w), lambda l: (l, 0, 0)),
            pl.BlockSpec((None, 1, MEM_HD), lambda l: (l, 0, 0)),
        ],
        out_specs=(out_spec, out_spec),
        out_shape=(out_shape, out_shape),
        compiler_params=_params(1),
        name="mem_kv",
    )(mem, gain, wk, wv, k_gain)


def _mem_attn_heads(mq_ref, qg_ref, keys, values, masks, o_ref):
    heads = range(MEM_HEADS)
    cols = [slice(hd * MEM_HD, (hd + 1) * MEM_HD) for hd in heads]
    qg = qg_ref[...]
    qn = [_rms(mq_ref[:, cols[hd]], qg).astype(BF16) for hd in heads]
    s = [_dot_nt(qn[hd], keys[hd]) * (MEM_HD ** -0.5) for hd in heads]
    if masks is not None:
        s = [jnp.where(masks[hd], s[hd], -jnp.inf) for hd in heads]
    e = [jnp.exp(s[hd] - jnp.max(s[hd], axis=-1, keepdims=True)) for hd in heads]
    p = [(e[hd] / jnp.sum(e[hd], axis=-1, keepdims=True)).astype(BF16) for hd in heads]
    out = [_dot(p[hd], values[hd]) for hd in heads]
    for hd in heads:
        o_ref[:, cols[hd]] = out[hd].astype(BF16)


def _mem_attn_prompt_kernel(mq_ref, qg_ref, mk_ref, mv_ref, o_ref):
    _mem_attn_heads(mq_ref, qg_ref, [mk_ref[hd].astype(BF16) for hd in range(MEM_HEADS)],
                    [mv_ref[hd].astype(BF16) for hd in range(MEM_HEADS)], None, o_ref)


def _mem_attn_sample_kernel(mq_ref, qg_ref, mk_ref, mv_ref, o_ref):
    k16 = mk_ref[...].astype(BF16)
    v16 = mv_ref[...].astype(BF16)
    row_head = lax.broadcasted_iota(jnp.int32, (mq_ref.shape[0], k16.shape[0]), 1) % MEM_HEADS
    _mem_attn_heads(mq_ref, qg_ref, [k16] * MEM_HEADS, [v16] * MEM_HEADS,
                    [row_head == hd for hd in range(MEM_HEADS)], o_ref)


def _mem_attn(proj, col_block, row_block0, rows, n_tiles, q_gain, layer, mk, mv, kv_index0, per_tile_kv):
    w = MEM_HEADS * MEM_HD
    if per_tile_kv:
        body = _mem_attn_sample_kernel
        kv_spec = pl.BlockSpec((None,) + mk.shape[1:], lambda b: (kv_index0 + b, 0, 0))
    else:
        body = _mem_attn_prompt_kernel
        kv_spec = pl.BlockSpec((None,) + mk.shape[1:], lambda b: (kv_index0, 0, 0, 0))
    return pl.pallas_call(
        body,
        grid=(n_tiles,),
        in_specs=[
            pl.BlockSpec((rows, w), lambda b: (row_block0 + b, col_block)),
            pl.BlockSpec((None, 1, MEM_HD), lambda b: (layer, 0, 0)),
            kv_spec,
            kv_spec,
        ],
        out_specs=pl.BlockSpec((rows, w), lambda b: (b, 0)),
        out_shape=jax.ShapeDtypeStruct((n_tiles * rows, w), BF16),
        compiler_params=_params(1),
        name="mem_attn",
    )(proj, q_gain, mk, mv)


def _mlstm_kernel(q_ref, k_ref, v_ref, og_ref, gc_ref, gr_ref, bc_ref, br_ref, hn_ref,
                  c0_ref, n0_ref, m0_ref, tok_ref, c_ref, n_ref, m_ref, *, chunk):
    L = chunk

    @pl.when(pl.program_id(1) == 0)
    def _():
        c_ref[...] = c0_ref[...]
        n_ref[...] = n0_ref[...]
        m_ref[...] = m0_ref[...]

    row = lax.broadcasted_iota(jnp.int32, (L, L), 0)
    col = lax.broadcasted_iota(jnp.int32, (L, L), 1)
    causal = col <= row
    tri = jnp.where(causal, 1.0, 0.0).astype(BF16)

    pre_c = gc_ref[...] + bc_ref[...]
    pre_r = gr_ref[...] + br_ref[...]
    lf_c = _log_sigmoid(pre_c)
    lf_r = _log_sigmoid(pre_r)
    cum_c = sum(_dot(tri, part) for part in _split3(lf_c))
    cum_r = sum(_dot_nt(part, tri) for part in _split3(lf_r))

    scale = A_DQK ** -0.5
    heads = range(A_HEADS)
    qk = [slice(h * A_DQK, (h + 1) * A_DQK) for h in heads]
    vv = [slice(h * A_DV, (h + 1) * A_DV) for h in heads]
    q = [q_ref[:, qk[h]] * scale for h in heads]
    q16 = [q[h].astype(BF16) for h in heads]
    k16 = [k_ref[:, qk[h]].astype(BF16) for h in heads]
    c_old = [c_ref[h] for h in heads]
    n_old = [n_ref[h:h + 1, :] for h in heads]
    m_old = [m_ref[h:h + 1, 0:1] for h in heads]
    i_c = [pre_c[:, h:h + 1] for h in heads]
    b_c = [cum_c[:, A_HEADS + h:A_HEADS + h + 1] for h in heads]
    i_r = [pre_r[h:h + 1, :] for h in heads]
    b_r = [cum_r[A_HEADS + h:A_HEADS + h + 1, :] for h in heads]

    qk_t = [_dot_nt(q16[h], k16[h]) for h in heads]
    q_c = [_dot_nt(q16[h], c_old[h].astype(BF16)) for h in heads]

    s, w_st, m_t = [], [], []
    for h in heads:
        d = jnp.where(causal, b_c[h] - b_r[h] + i_r[h], -jnp.inf)
        inter = b_c[h] + m_old[h]
        m_t.append(jnp.maximum(inter, jnp.max(d, axis=-1, keepdims=True)))
        w_st.append(jnp.exp(inter - m_t[h]))
        s.append(qk_t[h] * jnp.exp(d - m_t[h]))
    s_v = [_dot(s[h].astype(BF16), v_ref[:, vv[h]].astype(BF16)) for h in heads]

    for h in heads:
        num = s_v[h] + w_st[h] * q_c[h]
        den = (jnp.sum(s[h], axis=-1, keepdims=True)
               + w_st[h] * jnp.sum(q[h] * n_old[h], axis=-1, keepdims=True))
        hh = num * (1.0 / jnp.maximum(jnp.abs(den), jnp.exp(-m_t[h])))
        out = _rms(hh, hn_ref[:, vv[h]]) * jax.nn.sigmoid(og_ref[:, vv[h]])
        tok_ref[:, vv[h]] = out.astype(BF16)

    decay, w_k = [], []
    for h in heads:
        b_end = b_c[h][L - 1:L, :]
        g = b_end - b_c[h] + i_c[h]
        m_new = jnp.maximum(b_end + m_old[h], jnp.max(g, axis=0, keepdims=True))
        w_k.append(jnp.exp(g - m_new))
        decay.append(jnp.exp(b_end + m_old[h] - m_new))
        m_ref[h:h + 1, :] = jnp.broadcast_to(m_new, (1, LANES))
    vw_k = [lax.dot_general((v_ref[:, vv[h]] * w_k[h]).astype(BF16), k16[h], TN_DIMS,
                            preferred_element_type=F32) for h in heads]
    for h in heads:
        c_ref[h] = decay[h] * c_old[h] + vw_k[h]
        n_ref[h:h + 1, :] = (decay[h] * n_old[h]
                             + jnp.sum(w_k[h] * k_ref[:, qk[h]], axis=0, keepdims=True))


def _mlstm(proj, gates_r, bias_c, bias_r, head_norm, c0, n0, m0, row_block0, chunk, n_chunks):
    n_seq = c0.shape[0]
    aq = A_HEADS * A_DQK
    av = A_HEADS * A_DV
    gate_block = (2 * aq + 2 * av + MEM_HEADS * MEM_HD) // LANES

    def rows(b, c):
        return row_block0 + b * n_chunks + c

    state = lambda b, c: (b, 0, 0)
    return pl.pallas_call(
        functools.partial(_mlstm_kernel, chunk=chunk),
        grid=(n_seq, n_chunks),
        in_specs=[
            pl.BlockSpec((chunk, aq), lambda b, c: (rows(b, c), 0)),
            pl.BlockSpec((chunk, aq), lambda b, c: (rows(b, c), 1)),
            pl.BlockSpec((chunk, av), lambda b, c: (rows(b, c), 1)),
            pl.BlockSpec((chunk, av), lambda b, c: (rows(b, c), 2)),
            pl.BlockSpec((chunk, LANES), lambda b, c: (rows(b, c), gate_block)),
            pl.BlockSpec((None, GATE_ROWS, chunk), lambda b, c: (b, 0, c)),
            pl.BlockSpec((1, LANES), lambda b, c: (0, 0)),
            pl.BlockSpec((GATE_ROWS, 1), lambda b, c: (0, 0)),
            pl.BlockSpec((1, av), lambda b, c: (0, 0)),
            pl.BlockSpec((None, A_HEADS, A_DV, A_DQK), lambda b, c: (b, 0, 0, 0)),
            pl.BlockSpec((None, SUBLANES, A_DQK), state),
            pl.BlockSpec((None, SUBLANES, LANES), state),
        ],
        out_specs=(
            pl.BlockSpec((chunk, av), lambda b, c: (b * n_chunks + c, 0)),
            pl.BlockSpec((None, A_HEADS, A_DV, A_DQK), lambda b, c: (b, 0, 0, 0)),
            pl.BlockSpec((None, SUBLANES, A_DQK), state),
            pl.BlockSpec((None, SUBLANES, LANES), state),
        ),
        out_shape=(
            jax.ShapeDtypeStruct((n_seq * n_chunks * chunk, av), BF16),
            jax.ShapeDtypeStruct((n_seq, A_HEADS, A_DV, A_DQK), F32),
            jax.ShapeDtypeStruct((n_seq, SUBLANES, A_DQK), F32),
            jax.ShapeDtypeStruct((n_seq, SUBLANES, LANES), F32),
        ),
        compiler_params=_params(2),
        name="mlstm",
    )(proj, proj, proj, proj, proj, gates_r, bias_c, bias_r, head_norm, c0, n0, m0)


def _sb_scores(q16, k16, valid):
    z = _dot_nt(q16, k16) * (SB_HD ** -0.5)
    sp = jnp.maximum(z, 0.0) + jnp.log(1.0 + jnp.exp(-jnp.abs(z)))
    if valid is not None:
        sp = jnp.where(valid, sp, 0.0)
    return z, sp


def _sb_newer(sp, upper):
    s1 = sp.astype(BF16)
    s2 = (sp - s1.astype(F32)).astype(BF16)
    return _dot(s1, upper) + _dot(s2, upper)


def _sb_weights(z, sp, newer, valid, r_prev):
    a = jnp.exp(z - sp - newer + r_prev)
    if valid is not None:
        a = jnp.where(valid, a, 0.0)
    return a.astype(BF16)


def _sb_tiles(qs, ks, vs, upper, valid, r_prevs):
    scores = [_sb_scores(q, k, valid) for q, k in zip(qs, ks)]
    newer = [_sb_newer(sp, upper) for _, sp in scores]
    outs = [_dot(_sb_weights(z, sp, nw, valid, r), v)
            for (z, sp), nw, r, v in zip(scores, newer, r_prevs, vs)]
    sums = [r - jnp.sum(sp, axis=-1, keepdims=True) for (_, sp), r in zip(scores, r_prevs)]
    return outs, sums


def _upper(n):
    row = lax.broadcasted_iota(jnp.int32, (n, n), 0)
    col = lax.broadcasted_iota(jnp.int32, (n, n), 1)
    return jnp.where(row > col, 1.0, 0.0).astype(BF16)


def _sb_prompt_kernel(q_ref, k_ref, v_ref, o_ref, q16_ref, acc_ref, r_ref):
    t = SB_TILE
    base = pl.program_id(1) * SB_CHAINS + SB_PAD // SB_TILE
    upper = _upper(t)
    row = lax.broadcasted_iota(jnp.int32, (t, t), 0)
    col = lax.broadcasted_iota(jnp.int32, (t, t), 1)
    q16_ref[...] = q_ref[...].astype(BF16)

    def walk(j, diagonal):
        rows = [slice(c * t, (c + 1) * t) for c in range(SB_CHAINS)]
        starts = [pl.multiple_of((base + c - j) * t, t) for c in range(SB_CHAINS)]
        outs, sums = _sb_tiles(
            [q16_ref[r, :] for r in rows],
            [k_ref[pl.ds(s, t), :] for s in starts],
            [v_ref[pl.ds(s, t), :] for s in starts],
            upper, col < row if diagonal else None,
            [jnp.zeros((t, 1), F32) if diagonal else r_ref[r, :] for r in rows])
        r_max = None
        for r, out, total in zip(rows, outs, sums):
            if diagonal:
                acc_ref[r, :] = out
            else:
                acc_ref[r, :] += out
            r_ref[r, :] = total
            r_c = jnp.max(total)
            r_max = r_c if r_max is None else jnp.maximum(r_max, r_c)
        return r_max

    def cond(carry):
        j, r_max = carry
        return jnp.logical_and(j <= base, r_max > EXP_ZERO_BELOW)

    def body(carry):
        j, _ = carry
        return j + 1, walk(j, False)

    lax.while_loop(cond, body, (jnp.int32(1), walk(0, True)))
    o_ref[...] = acc_ref[...].astype(BF16)


def _sb_prompt(proj, kv16, seq):
    step = SB_CHAINS * SB_TILE
    return pl.pallas_call(
        _sb_prompt_kernel,
        grid=(SB_HEADS, seq // step),
        in_specs=[
            pl.BlockSpec((step, SB_HD), lambda h, i: (i, h)),
            pl.BlockSpec((None, SB_PAD + seq, SB_HD), lambda h, i: (0, 0, h)),
            pl.BlockSpec((None, SB_PAD + seq, SB_HD), lambda h, i: (1, 0, h)),
        ],
        out_specs=pl.BlockSpec((step, SB_HD), lambda h, i: (i, h)),
        out_shape=jax.ShapeDtypeStruct((seq, SB_HEADS * SB_HD), BF16),
        scratch_shapes=[pltpu.VMEM((step, SB_HD), BF16), pltpu.VMEM((step, SB_HD), F32),
                        pltpu.VMEM((step, 1), F32)],
        compiler_params=_params(2),
        name="sb_prompt",
    )(proj, kv16, kv16)


def _sb_sample_kernel(q_ref, kn_ref, vn_ref, pk_hbm, pv_hbm, o_ref, kbuf, vbuf, sem, acc_ref, r_ref,
                      *, rows, n_past):
    b = pl.program_id(0)
    t = SB_PAST_TILE

    def copies(tile_index):
        start = pl.multiple_of(tile_index * t, t)
        return (
            pltpu.make_async_copy(pk_hbm.at[b, :, pl.ds(start, t), :], kbuf, sem.at[0]),
            pltpu.make_async_copy(pv_hbm.at[b, :, pl.ds(start, t), :], vbuf, sem.at[1]),
        )

    def fetch(tile_index):
        for cp in copies(tile_index):
            cp.start()

    def wait(tile_index):
        for cp in copies(tile_index):
            cp.wait()

    heads = [slice(h * SB_HD, (h + 1) * SB_HD) for h in range(SB_HEADS)]

    def past_tile():
        outs, sums = _sb_tiles(
            [q_ref[:, sl].astype(BF16) for sl in heads],
            [kbuf[h].astype(BF16) for h in range(SB_HEADS)],
            [vbuf[h].astype(BF16) for h in range(SB_HEADS)],
            _upper(t), None, [r_ref[h] for h in range(SB_HEADS)])
        r_max = None
        for h, (out, total) in enumerate(zip(outs, sums)):
            acc_ref[:, heads[h]] += out
            r_ref[h] = total
            r_h = jnp.max(total)
            r_max = r_h if r_max is None else jnp.maximum(r_max, r_h)
        return r_max

    fetch(n_past - 1)
    row = lax.broadcasted_iota(jnp.int32, (rows, rows), 0)
    col = lax.broadcasted_iota(jnp.int32, (rows, rows), 1)
    outs, sums = _sb_tiles(
        [q_ref[:, sl].astype(BF16) for sl in heads], [kn_ref[:, sl] for sl in heads],
        [vn_ref[:, sl] for sl in heads], _upper(rows), col < row,
        [jnp.zeros((rows, 1), F32)] * SB_HEADS)
    for h, (out, total) in enumerate(zip(outs, sums)):
        acc_ref[:, heads[h]] = out
        r_ref[h] = total
    wait(n_past - 1)
    r_max = past_tile()

    def cond(carry):
        tile_index, r_max = carry
        return jnp.logical_and(tile_index >= 0, r_max > EXP_ZERO_BELOW)

    def body(carry):
        tile_index, _ = carry
        fetch(tile_index)
        wait(tile_index)
        return tile_index - 1, past_tile()

    lax.while_loop(cond, body, (jnp.int32(n_past - 2), r_max))
    o_ref[...] = acc_ref[...].astype(BF16)


def _sb_sample(proj, kv16, past_k, past_v, row_block0, rows):
    n_seq, heads, past, hd = past_k.shape
    w = heads * hd
    new_block0 = row_block0 + SB_PAD // rows
    return pl.pallas_call(
        functools.partial(_sb_sample_kernel, rows=rows, n_past=past // SB_PAST_TILE),
        grid=(n_seq,),
        in_specs=[
            pl.BlockSpec((rows, w), lambda b: (row_block0 + b, 0)),
            pl.BlockSpec((None, rows, w), lambda b: (0, new_block0 + b, 0)),
            pl.BlockSpec((None, rows, w), lambda b: (1, new_block0 + b, 0)),
            pl.BlockSpec(memory_space=pl.ANY),
            pl.BlockSpec(memory_space=pl.ANY),
        ],
        out_specs=pl.BlockSpec((rows, w), lambda b: (b, 0)),
        out_shape=jax.ShapeDtypeStruct((n_seq * rows, w), BF16),
        scratch_shapes=[
            pltpu.VMEM((heads, SB_PAST_TILE, hd), F32),
            pltpu.VMEM((heads, SB_PAST_TILE, hd), F32),
            pltpu.SemaphoreType.DMA((2,)),
            pltpu.VMEM((rows, w), F32),
            pltpu.VMEM((heads, rows, 1), F32),
        ],
        compiler_params=_params(1),
        name="sb_sample",
    )(proj, kv16, kv16, past_k, past_v)


def _pad_rows(a, n):
    return jnp.pad(a, [(0, 0)] * (a.ndim - 2) + [(0, n - a.shape[-2]), (0, 0)])


def kernel(x_prompt, x_sample, state_mlstm_C, state_mlstm_n, state_mlstm_m, cache_sb_k, cache_sb_v,
           cache_mem_k, cache_mem_v, mem_prompt, ffn1_norm, ffn1_w_gate, ffn1_w_up, ffn1_w_down,
           ffn2_norm, ffn2_w_gate, ffn2_w_up, ffn2_w_down, mix_norm, a_w_in, a_b_i, a_b_f, a_head_norm,
           b_w_in, w_out, mem_norm, mem_w_k, mem_w_v, mem_q_norm, mem_k_norm, kv_norm, sb_w_k, sb_w_v):
    n_pb, seq, d = x_prompt.shape
    n_sb, dec_seq, _ = x_sample.shape
    assert n_pb == 1
    depth = ffn1_norm.shape[0]
    n_a = a_w_in.shape[0]
    n_slots = mem_prompt.shape[1]
    aq = A_HEADS * A_DQK
    av = A_HEADS * A_DV
    mem_w = MEM_HEADS * MEM_HD
    sb_w = SB_HEADS * SB_HD
    n_sample = n_sb * dec_seq
    sample_block0 = seq // dec_seq

    gains = lambda g: g.reshape(g.shape[0], 1, g.shape[-1])
    ffn1 = (gains(ffn1_norm), ffn1_w_gate, ffn1_w_up, ffn1_w_down)
    ffn2 = (gains(ffn2_norm), ffn2_w_gate, ffn2_w_up, ffn2_w_down)
    mix_gain = gains(mix_norm)
    n_main = 2 * aq + 2 * av
    w_gates = a_w_in[:, :, n_main:n_main + 2 * A_HEADS]
    a_cols = n_main + mem_w + LANES
    a_tile = 1792
    a_pad = -a_cols % a_tile
    a_w = jnp.concatenate(
        [a_w_in[:, :, :n_main], a_w_in[:, :, n_main + 2 * A_HEADS:], w_gates,
         jnp.zeros((n_a, d, LANES - 2 * A_HEADS + a_pad), F32)], axis=-1).astype(BF16)
    a_wgt = _pad_rows(jnp.swapaxes(w_gates, 1, 2), GATE_ROWS).astype(BF16)
    gate_bias = jnp.concatenate([a_b_i, a_b_f], axis=-1)
    b_w = b_w_in.astype(BF16)
    w_kv = jnp.stack([sb_w_k, sb_w_v]).astype(BF16)
    q_gain = mem_q_norm.reshape(depth, 1, MEM_HD)

    mk_p, mv_p = _mem_kv(mem_prompt[0], gains(mem_norm), mem_w_k, mem_w_v,
                         mem_k_norm.reshape(depth, 1, MEM_HD))
    mk_s = cache_mem_k.reshape(depth * n_sb, n_slots * MEM_HEADS, MEM_HD)
    mv_s = cache_mem_v.reshape(depth * n_sb, n_slots * MEM_HEADS, MEM_HD)

    past_k = jnp.transpose(cache_sb_k, (0, 2, 1, 3))
    past_v = jnp.transpose(cache_sb_v, (0, 2, 1, 3))

    x = jnp.concatenate([x_prompt[0], x_sample.reshape(n_sample, d)], axis=0)
    c_p, n_p, m_p, c_s, n_s, m_s = [], [], [], [], [], []
    k_p = v_p = k_s = v_s = kv16 = None
    mem_tile = 1024
    for l in range(depth):
        x = _ffn(x, *ffn1, l)
        if l < n_a:
            proj, gates_r = _proj(x, mix_gain, l, a_w, l, a_tile, wgt=a_wgt)
            bias_c = jnp.pad(gate_bias[l], (0, LANES - 2 * A_HEADS)).reshape(1, LANES)
            bias_r = jnp.pad(gate_bias[l], (0, GATE_ROWS - 2 * A_HEADS)).reshape(GATE_ROWS, 1)
            head_norm = a_head_norm[l].reshape(1, av)
            chunk_p = 256
            tok_p, c, n, m = _mlstm(
                proj, gates_r[:, :seq].reshape(1, GATE_ROWS, seq), bias_c, bias_r, head_norm,
                jnp.zeros((1, A_HEADS, A_DV, A_DQK), F32), jnp.zeros((1, SUBLANES, A_DQK), F32),
                jnp.zeros((1, SUBLANES, LANES), F32), 0, chunk_p, seq // chunk_p)
            c_p.append(c); n_p.append(n[:, :A_HEADS]); m_p.append(m[:, :A_HEADS, 0])
            gates_s = gates_r[:, seq:].reshape(GATE_ROWS, n_sb, dec_seq).transpose(1, 0, 2)
            m0 = jnp.broadcast_to(state_mlstm_m[l][:, :, None], (n_sb, A_HEADS, LANES))
            tok_s, c, n, m = _mlstm(
                proj, gates_s, bias_c, bias_r, head_norm, state_mlstm_C[l],
                _pad_rows(state_mlstm_n[l], SUBLANES), _pad_rows(m0, SUBLANES), sample_block0, dec_seq, 1)
            c_s.append(c); n_s.append(n[:, :A_HEADS]); m_s.append(m[:, :A_HEADS, 0])
            mq_block = n_main // mem_w
        else:
            proj = _proj(x, mix_gain, l, b_w, l - n_a, b_w.shape[-1])
            tok_p = _sb_prompt(proj, kv16, seq)
            tok_s = _sb_sample(proj, kv16, past_k, past_v, sample_block0, dec_seq)
            mq_block = sb_w // mem_w
        mo_p = _mem_attn(proj, mq_block, 0, mem_tile, seq // mem_tile, q_gain, l, mk_p, mv_p, l, False)
        mo_s = _mem_attn(proj, mq_block, sample_block0, dec_seq, n_sb, q_gain, l, mk_s, mv_s, l * n_sb, True)
        x = _out_proj(x, tok_p, tok_s, mo_p, mo_s, w_out, l)
        x = _ffn(x, *ffn2, l)
        if l == n_a - 1:
            k_p, v_p, k_s, v_s, kv16 = _kv_proj(x, kv_norm.reshape(1, d), w_kv, seq, n_sb, dec_seq)

    y_prompt = x[:seq].reshape(1, seq, d)
    y_sample = x[seq:].reshape(n_sb, dec_seq, d)
    k_p, v_p = (jnp.transpose(a, (1, 0, 2))[None] for a in (k_p, v_p))
    k_s, v_s = (jnp.transpose(a, (0, 2, 1, 3)) for a in (k_s, v_s))
    mem_out = lambda a: jnp.transpose(a, (0, 2, 1, 3))[:, None]
    return (y_prompt, y_sample, jnp.stack(c_p), jnp.stack(n_p), jnp.stack(m_p), k_p, v_p,
            mem_out(mk_p), mem_out(mv_p),
            jnp.stack(c_s), jnp.stack(n_s), jnp.stack(m_s), k_s, v_s)
```

```python
import functools

import jax
import jax.numpy as jnp
from jax import lax
from jax.experimental import pallas as pl
from jax.experimental.pallas import tpu as pltpu

F32 = jnp.float32
BF16 = jnp.bfloat16

RMS_EPS = 1e-6
A_HEADS = 6
A_DQK = 128
A_DV = 256
SB_HEADS = 12
SB_HD = 128
MEM_HEADS = 4
MEM_HD = 128
LANES = 128
SUBLANES = 8
GATE_ROWS = 16

ROW_TILE = 768
FFN_ROW_TILE = 1056
FF_TILE = 512
FF_HEAD_TILE = 256
VMEM_LIMIT = 58 * 1024 * 1024

SB_TILE = 256
SB_CHAINS = 4
SB_PAST_TILE = 256
SB_PAD = 768

NT_DIMS = (((1,), (1,)), ((), ()))
TN_DIMS = (((0,), (0,)), ((), ()))
EXP_ZERO_BELOW = -104.0


def _params(n_axes):
    return pltpu.CompilerParams(
        dimension_semantics=("arbitrary",) * n_axes, vmem_limit_bytes=VMEM_LIMIT)


def _rms(x, g):
    ms = jnp.mean(x * x, axis=-1, keepdims=True)
    return x * lax.rsqrt(ms + RMS_EPS) * g


def _log_sigmoid(x):
    return jnp.minimum(x, 0.0) - jnp.log1p(jnp.exp(-jnp.abs(x)))


def _dot(a, b):
    return jnp.dot(a, b, preferred_element_type=F32)


def _dot_nt(a, b):
    return lax.dot_general(a, b, NT_DIMS, preferred_element_type=F32)


def _split3(x):
    x1 = x.astype(BF16)
    r = x - x1.astype(F32)
    x2 = r.astype(BF16)
    x3 = (r - x2.astype(F32)).astype(BF16)
    return x1, x2, x3


def _ffn_step(h_ref, wg, wu, wd, o_ref):
    h = h_ref[...]
    g = jnp.concatenate([_dot(h, w) for w in wg], axis=1)
    u = jnp.concatenate([_dot(h, w) for w in wu], axis=1)
    a = (g * jax.nn.sigmoid(g) * u * 0.5).astype(BF16)
    o_ref[...] += _dot(a, wd)


def _ffn_head_kernel(x_ref, g_ref, wg_ref, wu_ref, wd_ref, o_ref, wg16_ref, wu16_ref, wd16_ref, h_ref):
    @pl.when(pl.program_id(0) == 0)
    def _():
        x = x_ref[...]
        h_ref[...] = _rms(x, g_ref[...]).astype(BF16)
        o_ref[...] = x

    wg16_ref[...] = wg_ref[...].astype(BF16)
    wu16_ref[...] = wu_ref[...].astype(BF16)
    wd16_ref[...] = wd_ref[...].astype(BF16)
    _ffn_step(h_ref, [wg16_ref[...]], [wu16_ref[...]], wd16_ref[...], o_ref)


def _ffn_rest_kernel(x_ref, g_ref, wg_ref, wu_ref, wd_ref, o_ref, h_ref):
    @pl.when(pl.program_id(1) == 0)
    def _():
        x = x_ref[...]
        h_ref[...] = _rms(x, g_ref[...]).astype(BF16)
        o_ref[...] = x

    sub = range(wg_ref.shape[0])
    _ffn_step(h_ref, [wg_ref[t] for t in sub], [wu_ref[t] for t in sub], wd_ref[...], o_ref)


def _ffn(x, gain, wg, wu, wd, layer):
    m, d = x.shape
    f = wg.shape[-1]
    n_head = f // FF_HEAD_TILE
    w16 = lambda shape: jax.ShapeDtypeStruct(shape, BF16)
    x, wg16, wu16, wd16 = pl.pallas_call(
        _ffn_head_kernel,
        grid=(n_head,),
        in_specs=[
            pl.BlockSpec((FFN_ROW_TILE, d), lambda j: (0, 0)),
            pl.BlockSpec((None, 1, d), lambda j: (layer, 0, 0)),
            pl.BlockSpec((None, d, FF_HEAD_TILE), lambda j: (layer, 0, j)),
            pl.BlockSpec((None, d, FF_HEAD_TILE), lambda j: (layer, 0, j)),
            pl.BlockSpec((None, FF_HEAD_TILE, d), lambda j: (layer, j, 0)),
        ],
        out_specs=(
            pl.BlockSpec((FFN_ROW_TILE, d), lambda j: (0, 0)),
            pl.BlockSpec((None, d, FF_HEAD_TILE), lambda j: (j, 0, 0)),
            pl.BlockSpec((None, d, FF_HEAD_TILE), lambda j: (j, 0, 0)),
            pl.BlockSpec((FF_HEAD_TILE, d), lambda j: (j, 0)),
        ),
        out_shape=(jax.ShapeDtypeStruct((m, d), F32), w16((n_head, d, FF_HEAD_TILE)),
                   w16((n_head, d, FF_HEAD_TILE)), w16((f, d))),
        scratch_shapes=[pltpu.VMEM((FFN_ROW_TILE, d), BF16)],
        input_output_aliases={0: 0},
        compiler_params=_params(1),
        name="ffn_head",
    )(x, gain, wg, wu, wd)
    return pl.pallas_call(
        _ffn_rest_kernel,
        grid=(m // FFN_ROW_TILE - 1, f // FF_TILE),
        in_specs=[
            pl.BlockSpec((FFN_ROW_TILE, d), lambda i, j: (i + 1, 0)),
            pl.BlockSpec((None, 1, d), lambda i, j: (layer, 0, 0)),
            pl.BlockSpec((FF_TILE // FF_HEAD_TILE, d, FF_HEAD_TILE), lambda i, j: (j, 0, 0)),
            pl.BlockSpec((FF_TILE // FF_HEAD_TILE, d, FF_HEAD_TILE), lambda i, j: (j, 0, 0)),
            pl.BlockSpec((FF_TILE, d), lambda i, j: (j, 0)),
        ],
        out_specs=pl.BlockSpec((FFN_ROW_TILE, d), lambda i, j: (i + 1, 0)),
        out_shape=jax.ShapeDtypeStruct((m, d), F32),
        scratch_shapes=[pltpu.VMEM((FFN_ROW_TILE, d), BF16)],
        input_output_aliases={0: 0},
        compiler_params=_params(2),
        name="ffn",
    )(x, gain, wg16, wu16, wd16)


def _proj_kernel(x_ref, g_ref, w_ref, o_ref, h_ref):
    @pl.when(pl.program_id(1) == 0)
    def _():
        h_ref[...] = _rms(x_ref[...], g_ref[...]).astype(BF16)

    o_ref[...] = _dot(h_ref[...], w_ref[...]).astype(o_ref.dtype)


def _proj_gates_kernel(x_ref, g_ref, w_ref, wgt_ref, o_ref, gt_ref, h_ref):
    @pl.when(pl.program_id(1) == 0)
    def _():
        h = _rms(x_ref[...], g_ref[...]).astype(BF16)
        h_ref[...] = h
        gt_ref[...] = _dot_nt(wgt_ref[...], h)

    o_ref[...] = _dot(h_ref[...], w_ref[...]).astype(o_ref.dtype)


def _proj(x, gain, gain_layer, w, layer, col_tile, wgt=None):
    m, d = x.shape
    n = w.shape[-1]
    in_specs = [
        pl.BlockSpec((ROW_TILE, d), lambda i, j: (i, 0)),
        pl.BlockSpec((None, 1, d), lambda i, j: (gain_layer, 0, 0)),
        pl.BlockSpec((None, d, col_tile), lambda i, j: (layer, 0, j)),
    ]
    out_spec = pl.BlockSpec((ROW_TILE, col_tile), lambda i, j: (i, j))
    out_shape = jax.ShapeDtypeStruct((m, n), F32)
    common = dict(
        grid=(m // ROW_TILE, n // col_tile),
        scratch_shapes=[pltpu.VMEM((ROW_TILE, d), BF16)],
        compiler_params=_params(2),
    )
    if wgt is None:
        return pl.pallas_call(
            _proj_kernel, in_specs=in_specs, out_specs=out_spec, out_shape=out_shape,
            name="proj", **common)(x, gain, w)
    return pl.pallas_call(
        _proj_gates_kernel,
        in_specs=in_specs + [pl.BlockSpec((None, GATE_ROWS, d), lambda i, j: (layer, 0, 0))],
        out_specs=(out_spec, pl.BlockSpec((GATE_ROWS, ROW_TILE), lambda i, j: (0, i))),
        out_shape=(out_shape, jax.ShapeDtypeStruct((GATE_ROWS, m), F32)),
        name="proj_gates", **common)(x, gain, w, wgt)


def _kv_kernel(x_ref, g_ref, w_ref, kp_ref, vp_ref, ks_ref, vs_ref, o16_ref,
               *, n_pad, n_prompt, n_seq, rows):
    i = pl.program_id(0)

    @pl.when(i < n_pad)
    def _():
        o16_ref[...] = jnp.zeros(o16_ref.shape, BF16)

    @pl.when(i >= n_pad)
    def _():
        h = _rms(x_ref[...], g_ref[...]).astype(BF16)
        heads = [slice(hd * SB_HD, (hd + 1) * SB_HD) for hd in range(SB_HEADS)]
        for j, (p_ref, s_ref) in enumerate(((kp_ref, ks_ref), (vp_ref, vs_ref))):
            y = _dot(h, w_ref[j])
            o16_ref[j] = y.astype(BF16)

            @pl.when(i < n_pad + n_prompt)
            def _():
                for hd, sl in enumerate(heads):
                    p_ref[hd] = y[:, sl]

            @pl.when(i >= n_pad + n_prompt)
            def _():
                for b in range(n_seq):
                    for hd, sl in enumerate(heads):
                        s_ref[b, hd] = y[b * rows:(b + 1) * rows, sl]


def _kv_proj(x, gain, w, seq, n_seq, rows):
    m, d = x.shape
    n = w.shape[-1]
    tile = n_seq * rows
    n_prompt = seq // tile
    n_pad = SB_PAD // tile
    assert n_prompt * tile == seq and seq + tile == m and n_pad * tile == SB_PAD
    row_tile = lambda i: jnp.clip(i - n_pad, 0, n_prompt - 1)
    prompt_spec = pl.BlockSpec((SB_HEADS, tile, SB_HD), lambda i: (0, row_tile(i), 0))
    sample_spec = pl.BlockSpec((n_seq, SB_HEADS, rows, SB_HD), lambda i: (0, 0, 0, 0))
    prompt_shape = jax.ShapeDtypeStruct((SB_HEADS, seq, SB_HD), F32)
    sample_shape = jax.ShapeDtypeStruct((n_seq, SB_HEADS, rows, SB_HD), F32)
    return pl.pallas_call(
        functools.partial(_kv_kernel, n_pad=n_pad, n_prompt=n_prompt, n_seq=n_seq, rows=rows),
        grid=(n_pad + n_prompt + 1,),
        in_specs=[
            pl.BlockSpec((tile, d), lambda i: (jnp.maximum(i - n_pad, 0), 0)),
            pl.BlockSpec((1, d), lambda i: (0, 0)),
            pl.BlockSpec((2, d, n), lambda i: (0, 0, 0)),
        ],
        out_specs=(prompt_spec, prompt_spec, sample_spec, sample_spec,
                   pl.BlockSpec((2, tile, n), lambda i: (0, i, 0))),
        out_shape=(prompt_shape, prompt_shape, sample_shape, sample_shape,
                   jax.ShapeDtypeStruct((2, SB_PAD + m, n), BF16)),
        compiler_params=_params(1),
        name="kv_proj",
    )(x, gain, w)


def _out_proj_kernel(x_ref, tokp_ref, toks_ref, mop_ref, mos_ref, wt_ref, wm_ref, o_ref, w16_ref,
                     *, n_prompt):
    i = pl.program_id(0)
    kt = wt_ref.shape[0]

    @pl.when(i == 0)
    def _():
        w16_ref[:kt] = wt_ref[...].astype(BF16)
        w16_ref[kt:] = wm_ref[...].astype(BF16)

    @pl.when(i < n_prompt)
    def _():
        o_ref[...] = (x_ref[...] + _dot(tokp_ref[...], w16_ref[:kt])
                      + _dot(mop_ref[...], w16_ref[kt:]))

    @pl.when(i >= n_prompt)
    def _():
        o_ref[...] = (x_ref[...] + _dot(toks_ref[...], w16_ref[:kt])
                      + _dot(mos_ref[...], w16_ref[kt:]))


def _out_proj(x, tok_p, tok_s, mo_p, mo_s, w, layer):
    m, d = x.shape
    kt = tok_p.shape[1]
    km = mo_p.shape[1]
    tile = tok_s.shape[0]
    n_prompt = tok_p.shape[0] // tile
    assert n_prompt * tile == tok_p.shape[0] and (n_prompt + 1) * tile == m
    prompt = lambda i: (jnp.minimum(i, n_prompt - 1), 0)
    return pl.pallas_call(
        functools.partial(_out_proj_kernel, n_prompt=n_prompt),
        grid=(n_prompt + 1,),
        in_specs=[
            pl.BlockSpec((tile, d), lambda i: (i, 0)),
            pl.BlockSpec((tile, kt), prompt),
            pl.BlockSpec((tile, kt), lambda i: (0, 0)),
            pl.BlockSpec((tile, km), prompt),
            pl.BlockSpec((tile, km), lambda i: (0, 0)),
            pl.BlockSpec((None, kt, d), lambda i: (layer, 0, 0)),
            pl.BlockSpec((None, km, d), lambda i: (layer, kt // km, 0)),
        ],
        out_specs=pl.BlockSpec((tile, d), lambda i: (i, 0)),
        out_shape=jax.ShapeDtypeStruct((m, d), F32),
        scratch_shapes=[pltpu.VMEM((kt + km, d), BF16)],
        compiler_params=_params(1),
        name="out_proj",
    )(x, tok_p, tok_s, mo_p, mo_s, w, w)


def _mem_kv_kernel(mem_ref, g_ref, wk_ref, wv_ref, kg_ref, mk_ref, mv_ref):
    h = _rms(mem_ref[...], g_ref[...]).astype(BF16)
    k = _dot(h, wk_ref[...].astype(BF16))
    v = _dot(h, wv_ref[...].astype(BF16))
    kg = kg_ref[...]
    for hd in range(MEM_HEADS):
        sl = slice(hd * MEM_HD, (hd + 1) * MEM_HD)
        mk_ref[hd] = _rms(k[:, sl], kg)
        mv_ref[hd] = v[:, sl]


def _mem_kv(mem, gain, wk, wv, k_gain):
    depth, d, w = wk.shape
    n = mem.shape[0]
    out_spec = pl.BlockSpec((None, MEM_HEADS, n, MEM_HD), lambda l: (l, 0, 0, 0))
    out_shape = jax.ShapeDtypeStruct((depth, MEM_HEADS, n, MEM_HD), F32)
    return pl.pallas_call(
        _mem_kv_kernel,
        grid=(depth,),
        in_specs=[
            pl.BlockSpec((n, d), lambda l: (0, 0)),
            pl.BlockSpec((None, 1, d), lambda l: (l, 0, 0)),
            pl.BlockSpec((None, d, w), lambda l: (l, 0, 0)),
            pl.BlockSpec((None, d, w), lambda l: (l, 0, 0)),
            pl.BlockSpec((None, 1, MEM_HD), lambda l: (l, 0, 0)),
        ],
        out_specs=(out_spec, out_spec),
        out_shape=(out_shape, out_shape),
        compiler_params=_params(1),
        name="mem_kv",
    )(mem, gain, wk, wv, k_gain)


def _mem_attn_heads(mq_ref, qg_ref, keys, values, masks, o_ref):
    heads = range(MEM_HEADS)
    cols = [slice(hd * MEM_HD, (hd + 1) * MEM_HD) for hd in heads]
    qg = qg_ref[...]
    qn = [_rms(mq_ref[:, cols[hd]], qg).astype(BF16) for hd in heads]
    s = [_dot_nt(qn[hd], keys[hd]) * (MEM_HD ** -0.5) for hd in heads]
    if masks is not None:
        s = [jnp.where(masks[hd], s[hd], -jnp.inf) for hd in heads]
    e = [jnp.exp(s[hd] - jnp.max(s[hd], axis=-1, keepdims=True)) for hd in heads]
    p = [(e[hd] / jnp.sum(e[hd], axis=-1, keepdims=True)).astype(BF16) for hd in heads]
    out = [_dot(p[hd], values[hd]) for hd in heads]
    for hd in heads:
        o_ref[:, cols[hd]] = out[hd].astype(BF16)


def _mem_attn_prompt_kernel(mq_ref, qg_ref, mk_ref, mv_ref, o_ref):
    _mem_attn_heads(mq_ref, qg_ref, [mk_ref[hd].astype(BF16) for hd in range(MEM_HEADS)],
                    [mv_ref[hd].astype(BF16) for hd in range(MEM_HEADS)], None, o_ref)


def _mem_attn_sample_kernel(mq_ref, qg_ref, mk_ref, mv_ref, o_ref):
    k16 = mk_ref[...].astype(BF16)
    v16 = mv_ref[...].astype(BF16)
    row_head = lax.broadcasted_iota(jnp.int32, (mq_ref.shape[0], k16.shape[0]), 1) % MEM_HEADS
    _mem_attn_heads(mq_ref, qg_ref, [k16] * MEM_HEADS, [v16] * MEM_HEADS,
                    [row_head == hd for hd in range(MEM_HEADS)], o_ref)


def _mem_attn(proj, col_block, row_block0, rows, n_tiles, q_gain, layer, mk, mv, kv_index0, per_tile_kv):
    w = MEM_HEADS * MEM_HD
    if per_tile_kv:
        body = _mem_attn_sample_kernel
        kv_spec = pl.BlockSpec((None,) + mk.shape[1:], lambda b: (kv_index0 + b, 0, 0))
    else:
        body = _mem_attn_prompt_kernel
        kv_spec = pl.BlockSpec((None,) + mk.shape[1:], lambda b: (kv_index0, 0, 0, 0))
    return pl.pallas_call(
        body,
        grid=(n_tiles,),
        in_specs=[
            pl.BlockSpec((rows, w), lambda b: (row_block0 + b, col_block)),
            pl.BlockSpec((None, 1, MEM_HD), lambda b: (layer, 0, 0)),
            kv_spec,
            kv_spec,
        ],
        out_specs=pl.BlockSpec((rows, w), lambda b: (b, 0)),
        out_shape=jax.ShapeDtypeStruct((n_tiles * rows, w), BF16),
        compiler_params=_params(1),
        name="mem_attn",
    )(proj, q_gain, mk, mv)


def _mlstm_kernel(q_ref, k_ref, v_ref, og_ref, gc_ref, gr_ref, bc_ref, br_ref, hn_ref,
                  c0_ref, n0_ref, m0_ref, tok_ref, c_ref, n_ref, m_ref, *, chunk):
    L = chunk

    @pl.when(pl.program_id(1) == 0)
    def _():
        c_ref[...] = c0_ref[...]
        n_ref[...] = n0_ref[...]
        m_ref[...] = m0_ref[...]

    row = lax.broadcasted_iota(jnp.int32, (L, L), 0)
    col = lax.broadcasted_iota(jnp.int32, (L, L), 1)
    causal = col <= row
    tri = jnp.where(causal, 1.0, 0.0).astype(BF16)

    pre_c = gc_ref[...] + bc_ref[...]
    pre_r = gr_ref[...] + br_ref[...]
    lf_c = _log_sigmoid(pre_c)
    lf_r = _log_sigmoid(pre_r)
    cum_c = sum(_dot(tri, part) for part in _split3(lf_c))
    cum_r = sum(_dot_nt(part, tri) for part in _split3(lf_r))

    scale = A_DQK ** -0.5
    heads = range(A_HEADS)
    qk = [slice(h * A_DQK, (h + 1) * A_DQK) for h in heads]
    vv = [slice(h * A_DV, (h + 1) * A_DV) for h in heads]
    q = [q_ref[:, qk[h]] * scale for h in heads]
    q16 = [q[h].astype(BF16) for h in heads]
    k16 = [k_ref[:, qk[h]].astype(BF16) for h in heads]
    c_old = [c_ref[h] for h in heads]
    n_old = [n_ref[h:h + 1, :] for h in heads]
    m_old = [m_ref[h:h + 1, 0:1] for h in heads]
    i_c = [pre_c[:, h:h + 1] for h in heads]
    b_c = [cum_c[:, A_HEADS + h:A_HEADS + h + 1] for h in heads]
    i_r = [pre_r[h:h + 1, :] for h in heads]
    b_r = [cum_r[A_HEADS + h:A_HEADS + h + 1, :] for h in heads]

    qk_t = [_dot_nt(q16[h], k16[h]) for h in heads]
    q_c = [_dot_nt(q16[h], c_old[h].astype(BF16)) for h in heads]

    s, w_st, m_t = [], [], []
    for h in heads:
        d = jnp.where(causal, b_c[h] - b_r[h] + i_r[h], -jnp.inf)
        inter = b_c[h] + m_old[h]
        m_t.append(jnp.maximum(inter, jnp.max(d, axis=-1, keepdims=True)))
        w_st.append(jnp.exp(inter - m_t[h]))
        s.append(qk_t[h] * jnp.exp(d - m_t[h]))
    s_v = [_dot(s[h].astype(BF16), v_ref[:, vv[h]].astype(BF16)) for h in heads]

    for h in heads:
        num = s_v[h] + w_st[h] * q_c[h]
        den = (jnp.sum(s[h], axis=-1, keepdims=True)
               + w_st[h] * jnp.sum(q[h] * n_old[h], axis=-1, keepdims=True))
        hh = num * (1.0 / jnp.maximum(jnp.abs(den), jnp.exp(-m_t[h])))
        out = _rms(hh, hn_ref[:, vv[h]]) * jax.nn.sigmoid(og_ref[:, vv[h]])
        tok_ref[:, vv[h]] = out.astype(BF16)

    decay, w_k = [], []
    for h in heads:
        b_end = b_c[h][L - 1:L, :]
        g = b_end - b_c[h] + i_c[h]
        m_new = jnp.maximum(b_end + m_old[h], jnp.max(g, axis=0, keepdims=True))
        w_k.append(jnp.exp(g - m_new))
        decay.append(jnp.exp(b_end + m_old[h] - m_new))
        m_ref[h:h + 1, :] = jnp.broadcast_to(m_new, (1, LANES))
    vw_k = [lax.dot_general((v_ref[:, vv[h]] * w_k[h]).astype(BF16), k16[h], TN_DIMS,
                            preferred_element_type=F32) for h in heads]
    for h in heads:
        c_ref[h] = decay[h] * c_old[h] + vw_k[h]
        n_ref[h:h + 1, :] = (decay[h] * n_old[h]
                             + jnp.sum(w_k[h] * k_ref[:, qk[h]], axis=0, keepdims=True))


def _mlstm(proj, gates_r, bias_c, bias_r, head_norm, c0, n0, m0, row_block0, chunk, n_chunks):
    n_seq = c0.shape[0]
    aq = A_HEADS * A_DQK
    av = A_HEADS * A_DV
    gate_block = (2 * aq + 2 * av + MEM_HEADS * MEM_HD) // LANES

    def rows(b, c):
        return row_block0 + b * n_chunks + c

    state = lambda b, c: (b, 0, 0)
    return pl.pallas_call(
        functools.partial(_mlstm_kernel, chunk=chunk),
        grid=(n_seq, n_chunks),
        in_specs=[
            pl.BlockSpec((chunk, aq), lambda b, c: (rows(b, c), 0)),
            pl.BlockSpec((chunk, aq), lambda b, c: (rows(b, c), 1)),
            pl.BlockSpec((chunk, av), lambda b, c: (rows(b, c), 1)),
            pl.BlockSpec((chunk, av), lambda b, c: (rows(b, c), 2)),
            pl.BlockSpec((chunk, LANES), lambda b, c: (rows(b, c), gate_block)),
            pl.BlockSpec((None, GATE_ROWS, chunk), lambda b, c: (b, 0, c)),
            pl.BlockSpec((1, LANES), lambda b, c: (0, 0)),
            pl.BlockSpec((GATE_ROWS, 1), lambda b, c: (0, 0)),
            pl.BlockSpec((1, av), lambda b, c: (0, 0)),
            pl.BlockSpec((None, A_HEADS, A_DV, A_DQK), lambda b, c: (b, 0, 0, 0)),
            pl.BlockSpec((None, SUBLANES, A_DQK), state),
            pl.BlockSpec((None, SUBLANES, LANES), state),
        ],
        out_specs=(
            pl.BlockSpec((chunk, av), lambda b, c: (b * n_chunks + c, 0)),
            pl.BlockSpec((None, A_HEADS, A_DV, A_DQK), lambda b, c: (b, 0, 0, 0)),
            pl.BlockSpec((None, SUBLANES, A_DQK), state),
            pl.BlockSpec((None, SUBLANES, LANES), state),
        ),
        out_shape=(
            jax.ShapeDtypeStruct((n_seq * n_chunks * chunk, av), BF16),
            jax.ShapeDtypeStruct((n_seq, A_HEADS, A_DV, A_DQK), F32),
            jax.ShapeDtypeStruct((n_seq, SUBLANES, A_DQK), F32),
            jax.ShapeDtypeStruct((n_seq, SUBLANES, LANES), F32),
        ),
        compiler_params=_params(2),
        name="mlstm",
    )(proj, proj, proj, proj, proj, gates_r, bias_c, bias_r, head_norm, c0, n0, m0)


def _sb_scores(q16, k16, valid):
    z = _dot_nt(q16, k16) * (SB_HD ** -0.5)
    sp = jnp.maximum(z, 0.0) + jnp.log(1.0 + jnp.exp(-jnp.abs(z)))
    if valid is not None:
        sp = jnp.where(valid, sp, 0.0)
    return z, sp


def _sb_newer(sp, upper):
    s1 = sp.astype(BF16)
    s2 = (sp - s1.astype(F32)).astype(BF16)
    return _dot(s1, upper) + _dot(s2, upper)


def _sb_weights(z, sp, newer, valid, r_prev):
    a = jnp.exp(z - sp - newer + r_prev)
    if valid is not None:
        a = jnp.where(valid, a, 0.0)
    return a.astype(BF16)


def _sb_tiles(qs, ks, vs, upper, valid, r_prevs):
    scores = [_sb_scores(q, k, valid) for q, k in zip(qs, ks)]
    newer = [_sb_newer(sp, upper) for _, sp in scores]
    outs = [_dot(_sb_weights(z, sp, nw, valid, r), v)
            for (z, sp), nw, r, v in zip(scores, newer, r_prevs, vs)]
    sums = [r - jnp.sum(sp, axis=-1, keepdims=True) for (_, sp), r in zip(scores, r_prevs)]
    return outs, sums


def _upper(n):
    row = lax.broadcasted_iota(jnp.int32, (n, n), 0)
    col = lax.broadcasted_iota(jnp.int32, (n, n), 1)
    return jnp.where(row > col, 1.0, 0.0).astype(BF16)


def _sb_prompt_kernel(q_ref, k_ref, v_ref, o_ref, q16_ref, acc_ref, r_ref):
    t = SB_TILE
    base = pl.program_id(1) * SB_CHAINS + SB_PAD // SB_TILE
    upper = _upper(t)
    row = lax.broadcasted_iota(jnp.int32, (t, t), 0)
    col = lax.broadcasted_iota(jnp.int32, (t, t), 1)
    q16_ref[...] = q_ref[...].astype(BF16)

    def walk(j, diagonal):
        rows = [slice(c * t, (c + 1) * t) for c in range(SB_CHAINS)]
        starts = [pl.multiple_of((base + c - j) * t, t) for c in range(SB_CHAINS)]
        outs, sums = _sb_tiles(
            [q16_ref[r, :] for r in rows],
            [k_ref[pl.ds(s, t), :] for s in starts],
            [v_ref[pl.ds(s, t), :] for s in starts],
            upper, col < row if diagonal else None,
            [jnp.zeros((t, 1), F32) if diagonal else r_ref[r, :] for r in rows])
        r_max = None
        for r, out, total in zip(rows, outs, sums):
            if diagonal:
                acc_ref[r, :] = out
            else:
                acc_ref[r, :] += out
            r_ref[r, :] = total
            r_c = jnp.max(total)
            r_max = r_c if r_max is None else jnp.maximum(r_max, r_c)
        return r_max

    def cond(carry):
        j, r_max = carry
        return jnp.logical_and(j <= base, r_max > EXP_ZERO_BELOW)

    def body(carry):
        j, _ = carry
        return j + 1, walk(j, False)

    lax.while_loop(cond, body, (jnp.int32(1), walk(0, True)))
    o_ref[...] = acc_ref[...].astype(BF16)


def _sb_prompt(proj, kv16, seq):
    step = SB_CHAINS * SB_TILE
    return pl.pallas_call(
        _sb_prompt_kernel,
        grid=(SB_HEADS, seq // step),
        in_specs=[
            pl.BlockSpec((step, SB_HD), lambda h, i: (i, h)),
            pl.BlockSpec((None, SB_PAD + seq, SB_HD), lambda h, i: (0, 0, h)),
            pl.BlockSpec((None, SB_PAD + seq, SB_HD), lambda h, i: (1, 0, h)),
        ],
        out_specs=pl.BlockSpec((step, SB_HD), lambda h, i: (i, h)),
        out_shape=jax.ShapeDtypeStruct((seq, SB_HEADS * SB_HD), BF16),
        scratch_shapes=[pltpu.VMEM((step, SB_HD), BF16), pltpu.VMEM((step, SB_HD), F32),
                        pltpu.VMEM((step, 1), F32)],
        compiler_params=_params(2),
        name="sb_prompt",
    )(proj, kv16, kv16)


def _sb_sample_kernel(q_ref, kn_ref, vn_ref, pk_hbm, pv_hbm, o_ref, kbuf, vbuf, sem, acc_ref, r_ref,
                      *, rows, n_past):
    b = pl.program_id(0)
    t = SB_PAST_TILE

    def copies(tile_index):
        start = pl.multiple_of(tile_index * t, t)
        return (
            pltpu.make_async_copy(pk_hbm.at[b, :, pl.ds(start, t), :], kbuf, sem.at[0]),
            pltpu.make_async_copy(pv_hbm.at[b, :, pl.ds(start, t), :], vbuf, sem.at[1]),
        )

    def fetch(tile_index):
        for cp in copies(tile_index):
            cp.start()

    def wait(tile_index):
        for cp in copies(tile_index):
            cp.wait()

    heads = [slice(h * SB_HD, (h + 1) * SB_HD) for h in range(SB_HEADS)]

    def past_tile():
        outs, sums = _sb_tiles(
            [q_ref[:, sl].astype(BF16) for sl in heads],
            [kbuf[h].astype(BF16) for h in range(SB_HEADS)],
            [vbuf[h].astype(BF16) for h in range(SB_HEADS)],
            _upper(t), None, [r_ref[h] for h in range(SB_HEADS)])
        r_max = None
        for h, (out, total) in enumerate(zip(outs, sums)):
            acc_ref[:, heads[h]] += out
            r_ref[h] = total
            r_h = jnp.max(total)
            r_max = r_h if r_max is None else jnp.maximum(r_max, r_h)
        return r_max

    fetch(n_past - 1)
    row = lax.broadcasted_iota(jnp.int32, (rows, rows), 0)
    col = lax.broadcasted_iota(jnp.int32, (rows, rows), 1)
    outs, sums = _sb_tiles(
        [q_ref[:, sl].astype(BF16) for sl in heads], [kn_ref[:, sl] for sl in heads],
        [vn_ref[:, sl] for sl in heads], _upper(rows), col < row,
        [jnp.zeros((rows, 1), F32)] * SB_HEADS)
    for h, (out, total) in enumerate(zip(outs, sums)):
        acc_ref[:, heads[h]] = out
        r_ref[h] = total
    wait(n_past - 1)
    r_max = past_tile()

    def cond(carry):
        tile_index, r_max = carry
        return jnp.logical_and(tile_index >= 0, r_max > EXP_ZERO_BELOW)

    def body(carry):
        tile_index, _ = carry
        fetch(tile_index)
        wait(tile_index)
        return tile_index - 1, past_tile()

    lax.while_loop(cond, body, (jnp.int32(n_past - 2), r_max))
    o_ref[...] = acc_ref[...].astype(BF16)


def _sb_sample(proj, kv16, past_k, past_v, row_block0, rows):
    n_seq, heads, past, hd = past_k.shape
    w = heads * hd
    new_block0 = row_block0 + SB_PAD // rows
    return pl.pallas_call(
        functools.partial(_sb_sample_kernel, rows=rows, n_past=past // SB_PAST_TILE),
        grid=(n_seq,),
        in_specs=[
            pl.BlockSpec((rows, w), lambda b: (row_block0 + b, 0)),
            pl.BlockSpec((None, rows, w), lambda b: (0, new_block0 + b, 0)),
            pl.BlockSpec((None, rows, w), lambda b: (1, new_block0 + b, 0)),
            pl.BlockSpec(memory_space=pl.ANY),
            pl.BlockSpec(memory_space=pl.ANY),
        ],
        out_specs=pl.BlockSpec((rows, w), lambda b: (b, 0)),
        out_shape=jax.ShapeDtypeStruct((n_seq * rows, w), BF16),
        scratch_shapes=[
            pltpu.VMEM((heads, SB_PAST_TILE, hd), F32),
            pltpu.VMEM((heads, SB_PAST_TILE, hd), F32),
            pltpu.SemaphoreType.DMA((2,)),
            pltpu.VMEM((rows, w), F32),
            pltpu.VMEM((heads, rows, 1), F32),
        ],
        compiler_params=_params(1),
        name="sb_sample",
    )(proj, kv16, kv16, past_k, past_v)


def _pad_rows(a, n):
    return jnp.pad(a, [(0, 0)] * (a.ndim - 2) + [(0, n - a.shape[-2]), (0, 0)])


def kernel(x_prompt, x_sample, state_mlstm_C, state_mlstm_n, state_mlstm_m, cache_sb_k, cache_sb_v,
           cache_mem_k, cache_mem_v, mem_prompt, ffn1_norm, ffn1_w_gate, ffn1_w_up, ffn1_w_down,
           ffn2_norm, ffn2_w_gate, ffn2_w_up, ffn2_w_down, mix_norm, a_w_in, a_b_i, a_b_f, a_head_norm,
           b_w_in, w_out, mem_norm, mem_w_k, mem_w_v, mem_q_norm, mem_k_norm, kv_norm, sb_w_k, sb_w_v):
    n_pb, seq, d = x_prompt.shape
    n_sb, dec_seq, _ = x_sample.shape
    assert n_pb == 1
    depth = ffn1_norm.shape[0]
    n_a = a_w_in.shape[0]
    n_slots = mem_prompt.shape[1]
    aq = A_HEADS * A_DQK
    av = A_HEADS * A_DV
    mem_w = MEM_HEADS * MEM_HD
    sb_w = SB_HEADS * SB_HD
    n_sample = n_sb * dec_seq
    sample_block0 = seq // dec_seq

    gains = lambda g: g.reshape(g.shape[0], 1, g.shape[-1])
    ffn1 = (gains(ffn1_norm), ffn1_w_gate, ffn1_w_up, ffn1_w_down)
    ffn2 = (gains(ffn2_norm), ffn2_w_gate, ffn2_w_up, ffn2_w_down)
    mix_gain = gains(mix_norm)
    n_main = 2 * aq + 2 * av
    w_gates = a_w_in[:, :, n_main:n_main + 2 * A_HEADS]
    a_cols = n_main + mem_w + LANES
    a_tile = 1792
    a_pad = -a_cols % a_tile
    a_w = jnp.concatenate(
        [a_w_in[:, :, :n_main], a_w_in[:, :, n_main + 2 * A_HEADS:], w_gates,
         jnp.zeros((n_a, d, LANES - 2 * A_HEADS + a_pad), F32)], axis=-1).astype(BF16)
    a_wgt = _pad_rows(jnp.swapaxes(w_gates, 1, 2), GATE_ROWS).astype(BF16)
    gate_bias = jnp.concatenate([a_b_i, a_b_f], axis=-1)
    b_w = b_w_in.astype(BF16)
    w_kv = jnp.stack([sb_w_k, sb_w_v]).astype(BF16)
    q_gain = mem_q_norm.reshape(depth, 1, MEM_HD)

    mk_p, mv_p = _mem_kv(mem_prompt[0], gains(mem_norm), mem_w_k, mem_w_v,
                         mem_k_norm.reshape(depth, 1, MEM_HD))
    mk_s = cache_mem_k.reshape(depth * n_sb, n_slots * MEM_HEADS, MEM_HD)
    mv_s = cache_mem_v.reshape(depth * n_sb, n_slots * MEM_HEADS, MEM_HD)

    past_k = jnp.transpose(cache_sb_k, (0, 2, 1, 3))
    past_v = jnp.transpose(cache_sb_v, (0, 2, 1, 3))

    x = jnp.concatenate([x_prompt[0], x_sample.reshape(n_sample, d)], axis=0)
    c_p, n_p, m_p, c_s, n_s, m_s = [], [], [], [], [], []
    k_p = v_p = k_s = v_s = kv16 = None
    mem_tile = 1024
    for l in range(depth):
        x = _ffn(x, *ffn1, l)
        if l < n_a:
            proj, gates_r = _proj(x, mix_gain, l, a_w, l, a_tile, wgt=a_wgt)
            bias_c = jnp.pad(gate_bias[l], (0, LANES - 2 * A_HEADS)).reshape(1, LANES)
            bias_r = jnp.pad(gate_bias[l], (0, GATE_ROWS - 2 * A_HEADS)).reshape(GATE_ROWS, 1)
            head_norm = a_head_norm[l].reshape(1, av)
            chunk_p = 256
            tok_p, c, n, m = _mlstm(
                proj, gates_r[:, :seq].reshape(1, GATE_ROWS, seq), bias_c, bias_r, head_norm,
                jnp.zeros((1, A_HEADS, A_DV, A_DQK), F32), jnp.zeros((1, SUBLANES, A_DQK), F32),
                jnp.zeros((1, SUBLANES, LANES), F32), 0, chunk_p, seq // chunk_p)
            c_p.append(c); n_p.append(n[:, :A_HEADS]); m_p.append(m[:, :A_HEADS, 0])
            gates_s = gates_r[:, seq:].reshape(GATE_ROWS, n_sb, dec_seq).transpose(1, 0, 2)
            m0 = jnp.broadcast_to(state_mlstm_m[l][:, :, None], (n_sb, A_HEADS, LANES))
            tok_s, c, n, m = _mlstm(
                proj, gates_s, bias_c, bias_r, head_norm, state_mlstm_C[l],
                _pad_rows(state_mlstm_n[l], SUBLANES), _pad_rows(m0, SUBLANES), sample_block0, dec_seq, 1)
            c_s.append(c); n_s.append(n[:, :A_HEADS]); m_s.append(m[:, :A_HEADS, 0])
            mq_block = n_main // mem_w
        else:
            proj = _proj(x, mix_gain, l, b_w, l - n_a, b_w.shape[-1])
            tok_p = _sb_prompt(proj, kv16, seq)
            tok_s = _sb_sample(proj, kv16, past_k, past_v, sample_block0, dec_seq)
            mq_block = sb_w // mem_w
        mo_p = _mem_attn(proj, mq_block, 0, mem_tile, seq // mem_tile, q_gain, l, mk_p, mv_p, l, False)
        mo_s = _mem_attn(proj, mq_block, sample_block0, dec_seq, n_sb, q_gain, l, mk_s, mv_s, l * n_sb, True)
        x = _out_proj(x, tok_p, tok_s, mo_p, mo_s, w_out, l)
        x = _ffn(x, *ffn2, l)
        if l == n_a - 1:
            k_p, v_p, k_s, v_s, kv16 = _kv_proj(x, kv_norm.reshape(1, d), w_kv, seq, n_sb, dec_seq)

    y_prompt = x[:seq].reshape(1, seq, d)
    y_sample = x[seq:].reshape(n_sb, dec_seq, d)
    k_p, v_p = (jnp.transpose(a, (1, 0, 2))[None] for a in (k_p, v_p))
    k_s, v_s = (jnp.transpose(a, (0, 2, 1, 3)) for a in (k_s, v_s))
    mem_out = lambda a: jnp.transpose(a, (0, 2, 1, 3))[:, None]
    return (y_prompt, y_sample, jnp.stack(c_p), jnp.stack(n_p), jnp.stack(m_p), k_p, v_p,
            mem_out(mk_p), mem_out(mv_p),
            jnp.stack(c_s), jnp.stack(n_s), jnp.stack(m_s), k_s, v_s)
```

```python
import functools

import jax
import jax.numpy as jnp
from jax import lax
from jax.experimental import pallas as pl
from jax.experimental.pallas import tpu as pltpu

F32 = jnp.float32
BF16 = jnp.bfloat16

RMS_EPS = 1e-6
A_HEADS = 6
A_DQK = 128
A_DV = 256
SB_HEADS = 12
SB_HD = 128
MEM_HEADS = 4
MEM_HD = 128
LANES = 128
SUBLANES = 8
GATE_ROWS = 16

ROW_TILE = 768
FFN_ROW_TILE = 1056
FF_TILE = 512
FF_HEAD_TILE = 256
VMEM_LIMIT = 58 * 1024 * 1024

MEM_GROUP = 4

SB_TILE = 256
SB_CHAINS = 4
SB_PAST_TILE = 256
SB_PAD = 768

NT_DIMS = (((1,), (1,)), ((), ()))
TN_DIMS = (((0,), (0,)), ((), ()))
EXP_ZERO_BELOW = -104.0


def _params(n_axes):
    return pltpu.CompilerParams(
        dimension_semantics=("arbitrary",) * n_axes, vmem_limit_bytes=VMEM_LIMIT)


def _rms(x, g):
    ms = jnp.mean(x * x, axis=-1, keepdims=True)
    return x * lax.rsqrt(ms + RMS_EPS) * g


def _log_sigmoid(x):
    return jnp.minimum(x, 0.0) - jnp.log1p(jnp.exp(-jnp.abs(x)))


def _dot(a, b):
    return jnp.dot(a, b, preferred_element_type=F32)


def _dot_nt(a, b):
    return lax.dot_general(a, b, NT_DIMS, preferred_element_type=F32)


def _split3(x):
    x1 = x.astype(BF16)
    r = x - x1.astype(F32)
    x2 = r.astype(BF16)
    x3 = (r - x2.astype(F32)).astype(BF16)
    return x1, x2, x3


def _ffn_step(h_ref, wg, wu, wd, o_ref):
    h = h_ref[...]
    g = jnp.concatenate([_dot(h, w) for w in wg], axis=1)
    u = jnp.concatenate([_dot(h, w) for w in wu], axis=1)
    a = (g * jax.nn.sigmoid(g) * u * 0.5).astype(BF16)
    o_ref[...] += _dot(a, wd)


def _ffn_head_kernel(x_ref, g_ref, wg_ref, wu_ref, wd_ref, o_ref, wg16_ref, wu16_ref, wd16_ref, h_ref):
    @pl.when(pl.program_id(0) == 0)
    def _():
        x = x_ref[...]
        h_ref[...] = _rms(x, g_ref[...]).astype(BF16)
        o_ref[...] = x

    wg16_ref[...] = wg_ref[...].astype(BF16)
    wu16_ref[...] = wu_ref[...].astype(BF16)
    wd16_ref[...] = wd_ref[...].astype(BF16)
    _ffn_step(h_ref, [wg16_ref[...]], [wu16_ref[...]], wd16_ref[...], o_ref)


def _ffn_rest_kernel(x_ref, g_ref, wg_ref, wu_ref, wd_ref, o_ref, h_ref):
    @pl.when(pl.program_id(1) == 0)
    def _():
        x = x_ref[...]
        h_ref[...] = _rms(x, g_ref[...]).astype(BF16)
        o_ref[...] = x

    sub = range(wg_ref.shape[0])
    _ffn_step(h_ref, [wg_ref[t] for t in sub], [wu_ref[t] for t in sub], wd_ref[...], o_ref)


def _ffn(x, gain, wg, wu, wd, layer):
    m, d = x.shape
    f = wg.shape[-1]
    n_head = f // FF_HEAD_TILE
    w16 = lambda shape: jax.ShapeDtypeStruct(shape, BF16)
    x, wg16, wu16, wd16 = pl.pallas_call(
        _ffn_head_kernel,
        grid=(n_head,),
        in_specs=[
            pl.BlockSpec((FFN_ROW_TILE, d), lambda j: (0, 0)),
            pl.BlockSpec((None, 1, d), lambda j: (layer, 0, 0)),
            pl.BlockSpec((None, d, FF_HEAD_TILE), lambda j: (layer, 0, j)),
            pl.BlockSpec((None, d, FF_HEAD_TILE), lambda j: (layer, 0, j)),
            pl.BlockSpec((None, FF_HEAD_TILE, d), lambda j: (layer, j, 0)),
        ],
        out_specs=(
            pl.BlockSpec((FFN_ROW_TILE, d), lambda j: (0, 0)),
            pl.BlockSpec((None, d, FF_HEAD_TILE), lambda j: (j, 0, 0)),
            pl.BlockSpec((None, d, FF_HEAD_TILE), lambda j: (j, 0, 0)),
            pl.BlockSpec((FF_HEAD_TILE, d), lambda j: (j, 0)),
        ),
        out_shape=(jax.ShapeDtypeStruct((m, d), F32), w16((n_head, d, FF_HEAD_TILE)),
                   w16((n_head, d, FF_HEAD_TILE)), w16((f, d))),
        scratch_shapes=[pltpu.VMEM((FFN_ROW_TILE, d), BF16)],
        input_output_aliases={0: 0},
        compiler_params=_params(1),
        name="ffn_head",
    )(x, gain, wg, wu, wd)
    return pl.pallas_call(
        _ffn_rest_kernel,
        grid=(m // FFN_ROW_TILE - 1, f // FF_TILE),
        in_specs=[
            pl.BlockSpec((FFN_ROW_TILE, d), lambda i, j: (i + 1, 0)),
            pl.BlockSpec((None, 1, d), lambda i, j: (layer, 0, 0)),
            pl.BlockSpec((FF_TILE // FF_HEAD_TILE, d, FF_HEAD_TILE), lambda i, j: (j, 0, 0)),
            pl.BlockSpec((FF_TILE // FF_HEAD_TILE, d, FF_HEAD_TILE), lambda i, j: (j, 0, 0)),
            pl.BlockSpec((FF_TILE, d), lambda i, j: (j, 0)),
        ],
        out_specs=pl.BlockSpec((FFN_ROW_TILE, d), lambda i, j: (i + 1, 0)),
        out_shape=jax.ShapeDtypeStruct((m, d), F32),
        scratch_shapes=[pltpu.VMEM((FFN_ROW_TILE, d), BF16)],
        input_output_aliases={0: 0},
        compiler_params=_params(2),
        name="ffn",
    )(x, gain, wg16, wu16, wd16)


def _proj_kernel(x_ref, g_ref, w_ref, o_ref, h_ref):
    @pl.when(pl.program_id(1) == 0)
    def _():
        h_ref[...] = _rms(x_ref[...], g_ref[...]).astype(BF16)

    o_ref[...] = _dot(h_ref[...], w_ref[...]).astype(o_ref.dtype)


def _proj_gates_kernel(x_ref, g_ref, w_ref, wgt_ref, o_ref, gt_ref, h_ref):
    @pl.when(pl.program_id(1) == 0)
    def _():
        h = _rms(x_ref[...], g_ref[...]).astype(BF16)
        h_ref[...] = h
        gt_ref[...] = _dot_nt(wgt_ref[...], h)

    o_ref[...] = _dot(h_ref[...], w_ref[...]).astype(o_ref.dtype)


def _proj(x, gain, gain_layer, w, layer, col_tile, wgt=None):
    m, d = x.shape
    n = w.shape[-1]
    in_specs = [
        pl.BlockSpec((ROW_TILE, d), lambda i, j: (i, 0)),
        pl.BlockSpec((None, 1, d), lambda i, j: (gain_layer, 0, 0)),
        pl.BlockSpec((None, d, col_tile), lambda i, j: (layer, 0, j)),
    ]
    out_spec = pl.BlockSpec((ROW_TILE, col_tile), lambda i, j: (i, j))
    out_shape = jax.ShapeDtypeStruct((m, n), F32)
    common = dict(
        grid=(m // ROW_TILE, n // col_tile),
        scratch_shapes=[pltpu.VMEM((ROW_TILE, d), BF16)],
        compiler_params=_params(2),
    )
    if wgt is None:
        return pl.pallas_call(
            _proj_kernel, in_specs=in_specs, out_specs=out_spec, out_shape=out_shape,
            name="proj", **common)(x, gain, w)
    return pl.pallas_call(
        _proj_gates_kernel,
        in_specs=in_specs + [pl.BlockSpec((None, GATE_ROWS, d), lambda i, j: (layer, 0, 0))],
        out_specs=(out_spec, pl.BlockSpec((GATE_ROWS, ROW_TILE), lambda i, j: (0, i))),
        out_shape=(out_shape, jax.ShapeDtypeStruct((GATE_ROWS, m), F32)),
        name="proj_gates", **common)(x, gain, w, wgt)


def _kv_kernel(x_ref, g_ref, w_ref, kp_ref, vp_ref, ks_ref, vs_ref, o16_ref,
               *, n_pad, n_prompt, n_seq, rows):
    i = pl.program_id(0)

    @pl.when(i < n_pad)
    def _():
        o16_ref[...] = jnp.zeros(o16_ref.shape, BF16)

    @pl.when(i >= n_pad)
    def _():
        h = _rms(x_ref[...], g_ref[...]).astype(BF16)
        heads = [slice(hd * SB_HD, (hd + 1) * SB_HD) for hd in range(SB_HEADS)]
        for j, (p_ref, s_ref) in enumerate(((kp_ref, ks_ref), (vp_ref, vs_ref))):
            y = _dot(h, w_ref[j])
            o16_ref[j] = y.astype(BF16)

            @pl.when(i < n_pad + n_prompt)
            def _():
                for hd, sl in enumerate(heads):
                    p_ref[hd] = y[:, sl]

            @pl.when(i >= n_pad + n_prompt)
            def _():
                for b in range(n_seq):
                    for hd, sl in enumerate(heads):
                        s_ref[b, hd] = y[b * rows:(b + 1) * rows, sl]


def _kv_proj(x, gain, w, seq, n_seq, rows):
    m, d = x.shape
    n = w.shape[-1]
    tile = n_seq * rows
    n_prompt = seq // tile
    n_pad = SB_PAD // tile
    assert n_prompt * tile == seq and seq + tile == m and n_pad * tile == SB_PAD
    row_tile = lambda i: jnp.clip(i - n_pad, 0, n_prompt - 1)
    prompt_spec = pl.BlockSpec((SB_HEADS, tile, SB_HD), lambda i: (0, row_tile(i), 0))
    sample_spec = pl.BlockSpec((n_seq, SB_HEADS, rows, SB_HD), lambda i: (0, 0, 0, 0))
    prompt_shape = jax.ShapeDtypeStruct((SB_HEADS, seq, SB_HD), F32)
    sample_shape = jax.ShapeDtypeStruct((n_seq, SB_HEADS, rows, SB_HD), F32)
    return pl.pallas_call(
        functools.partial(_kv_kernel, n_pad=n_pad, n_prompt=n_prompt, n_seq=n_seq, rows=rows),
        grid=(n_pad + n_prompt + 1,),
        in_specs=[
            pl.BlockSpec((tile, d), lambda i: (jnp.maximum(i - n_pad, 0), 0)),
            pl.BlockSpec((1, d), lambda i: (0, 0)),
            pl.BlockSpec((2, d, n), lambda i: (0, 0, 0)),
        ],
        out_specs=(prompt_spec, prompt_spec, sample_spec, sample_spec,
                   pl.BlockSpec((2, tile, n), lambda i: (0, i, 0))),
        out_shape=(prompt_shape, prompt_shape, sample_shape, sample_shape,
                   jax.ShapeDtypeStruct((2, SB_PAD + m, n), BF16)),
        compiler_params=_params(1),
        name="kv_proj",
    )(x, gain, w)


def _out_proj_kernel(x_ref, tokp_ref, toks_ref, mop_ref, mos_ref, wt_ref, wm_ref, o_ref, w16_ref,
                     *, n_prompt):
    i = pl.program_id(0)
    kt = wt_ref.shape[0]

    @pl.when(i == 0)
    def _():
        w16_ref[:kt] = wt_ref[...].astype(BF16)
        w16_ref[kt:] = wm_ref[...].astype(BF16)

    @pl.when(i < n_prompt)
    def _():
        o_ref[...] = (x_ref[...] + _dot(tokp_ref[...], w16_ref[:kt])
                      + _dot(mop_ref[...], w16_ref[kt:]))

    @pl.when(i >= n_prompt)
    def _():
        o_ref[...] = (x_ref[...] + _dot(toks_ref[...], w16_ref[:kt])
                      + _dot(mos_ref[...], w16_ref[kt:]))


def _out_proj(x, tok_p, tok_s, mo_p, mo_s, w, layer):
    m, d = x.shape
    kt = tok_p.shape[1]
    km = mo_p.shape[1]
    tile = tok_s.shape[0]
    n_prompt = tok_p.shape[0] // tile
    assert n_prompt * tile == tok_p.shape[0] and (n_prompt + 1) * tile == m
    prompt = lambda i: (jnp.minimum(i, n_prompt - 1), 0)
    return pl.pallas_call(
        functools.partial(_out_proj_kernel, n_prompt=n_prompt),
        grid=(n_prompt + 1,),
        in_specs=[
            pl.BlockSpec((tile, d), lambda i: (i, 0)),
            pl.BlockSpec((tile, kt), prompt),
            pl.BlockSpec((tile, kt), lambda i: (0, 0)),
            pl.BlockSpec((tile, km), prompt),
            pl.BlockSpec((tile, km), lambda i: (0, 0)),
            pl.BlockSpec((None, kt, d), lambda i: (layer, 0, 0)),
            pl.BlockSpec((None, km, d), lambda i: (layer, kt // km, 0)),
        ],
        out_specs=pl.BlockSpec((tile, d), lambda i: (i, 0)),
        out_shape=jax.ShapeDtypeStruct((m, d), F32),
        scratch_shapes=[pltpu.VMEM((kt + km, d), BF16)],
        compiler_params=_params(1),
        name="out_proj",
    )(x, tok_p, tok_s, mo_p, mo_s, w, w)


def _mem_kv_kernel(mem_ref, g_ref, wk_ref, wv_ref, kg_ref, mk_ref, mv_ref):
    h = _rms(mem_ref[...], g_ref[...]).astype(BF16)
    k = _dot(h, wk_ref[...].astype(BF16))
    v = _dot(h, wv_ref[...].astype(BF16))
    kg = kg_ref[...]
    for hd in range(MEM_HEADS):
        sl = slice(hd * MEM_HD, (hd + 1) * MEM_HD)
        mk_ref[hd] = _rms(k[:, sl], kg)
        mv_ref[hd] = v[:, sl]


def _mem_kv(mem, gain, wk, wv, k_gain):
    depth, d, w = wk.shape
    n = mem.shape[0]
    out_spec = pl.BlockSpec((None, MEM_HEADS, n, MEM_HD), lambda l: (l, 0, 0, 0))
    out_shape = jax.ShapeDtypeStruct((depth, MEM_HEADS, n, MEM_HD), F32)
    return pl.pallas_call(
        _mem_kv_kernel,
        grid=(depth,),
        in_specs=[
            pl.BlockSpec((n, d), lambda l: (0, 0)),
            pl.BlockSpec((None, 1, d), lambda l: (l, 0, 0)),
            pl.BlockSpec((None, d, w), lambda l: (l, 0, 0)),
            pl.BlockSpec((None, d, w), lambda l: (l, 0, 0)),
            pl.BlockSpec((None, 1, MEM_HD), lambda l: (l, 0, 0)),
        ],
        out_specs=(out_spec, out_spec),
        out_shape=(out_shape, out_shape),
        compiler_params=_params(1),
        name="mem_kv",
    )(mem, gain, wk, wv, k_gain)


def _mem_attn_chains(mq_ref, qg_ref, chains, o_ref):
    qg = qg_ref[...]
    qn = [_rms(mq_ref[r, c], qg).astype(BF16) for r, c, _, _, _ in chains]
    s = [_dot_nt(q, k) * (MEM_HD ** -0.5) for q, (_, _, k, _, _) in zip(qn, chains)]
    s = [x if m is None else jnp.where(m, x, -jnp.inf) for x, (_, _, _, _, m) in zip(s, chains)]
    e = [jnp.exp(x - jnp.max(x, axis=-1, keepdims=True)) for x in s]
    p = [(x / jnp.sum(x, axis=-1, keepdims=True)).astype(BF16) for x in e]
    out = [_dot(x, v) for x, (_, _, _, v, _) in zip(p, chains)]
    for x, (r, c, _, _, _) in zip(out, chains):
        o_ref[r, c] = x.astype(BF16)


def _mem_attn_prompt_kernel(mq_ref, qg_ref, mk_ref, mv_ref, o_ref):
    _mem_attn_chains(mq_ref, qg_ref, [
        (slice(None), slice(hd * MEM_HD, (hd + 1) * MEM_HD), mk_ref[hd].astype(BF16),
         mv_ref[hd].astype(BF16), None) for hd in range(MEM_HEADS)], o_ref)


def _mem_attn_sample_kernel(mq_ref, qg_ref, mk_ref, mv_ref, o_ref):
    group, n_rows, _ = mk_ref.shape
    rows = mq_ref.shape[0] // group
    row_head = lax.broadcasted_iota(jnp.int32, (rows, n_rows), 1) % MEM_HEADS
    masks = [row_head == hd for hd in range(MEM_HEADS)]
    chains = []
    for g in range(group):
        k16 = mk_ref[g].astype(BF16)
        v16 = mv_ref[g].astype(BF16)
        chains += [(slice(g * rows, (g + 1) * rows), slice(hd * MEM_HD, (hd + 1) * MEM_HD),
                    k16, v16, masks[hd]) for hd in range(MEM_HEADS)]
    _mem_attn_chains(mq_ref, qg_ref, chains, o_ref)


def _mem_attn(proj, col_block, row_block0, rows, n_tiles, q_gain, layer, mk, mv, kv_index0, group):
    w = MEM_HEADS * MEM_HD
    if group:
        body = _mem_attn_sample_kernel
        kv_spec = pl.BlockSpec((group,) + mk.shape[1:], lambda b: (kv_index0 + b, 0, 0))
    else:
        body = _mem_attn_prompt_kernel
        kv_spec = pl.BlockSpec((None,) + mk.shape[1:], lambda b: (kv_index0, 0, 0, 0))
    return pl.pallas_call(
        body,
        grid=(n_tiles,),
        in_specs=[
            pl.BlockSpec((rows, w), lambda b: (row_block0 + b, col_block)),
            pl.BlockSpec((None, 1, MEM_HD), lambda b: (layer, 0, 0)),
            kv_spec,
            kv_spec,
        ],
        out_specs=pl.BlockSpec((rows, w), lambda b: (b, 0)),
        out_shape=jax.ShapeDtypeStruct((n_tiles * rows, w), BF16),
        compiler_params=_params(1),
        name="mem_attn",
    )(proj, q_gain, mk, mv)


def _mlstm_kernel(q_ref, k_ref, v_ref, og_ref, gc_ref, gr_ref, bc_ref, br_ref, hn_ref,
                  c0_ref, n0_ref, m0_ref, tok_ref, c_ref, n_ref, m_ref, *, chunk):
    L = chunk

    @pl.when(pl.program_id(1) == 0)
    def _():
        c_ref[...] = c0_ref[...]
        n_ref[...] = n0_ref[...]
        m_ref[...] = m0_ref[...]

    row = lax.broadcasted_iota(jnp.int32, (L, L), 0)
    col = lax.broadcasted_iota(jnp.int32, (L, L), 1)
    causal = col <= row
    tri = jnp.where(causal, 1.0, 0.0).astype(BF16)

    pre_c = gc_ref[...] + bc_ref[...]
    pre_r = gr_ref[...] + br_ref[...]
    lf_c = _log_sigmoid(pre_c)
    lf_r = _log_sigmoid(pre_r)
    cum_c = sum(_dot(tri, part) for part in _split3(lf_c))
    cum_r = sum(_dot_nt(part, tri) for part in _split3(lf_r))

    scale = A_DQK ** -0.5
    heads = range(A_HEADS)
    qk = [slice(h * A_DQK, (h + 1) * A_DQK) for h in heads]
    vv = [slice(h * A_DV, (h + 1) * A_DV) for h in heads]
    q = [q_ref[:, qk[h]] * scale for h in heads]
    q16 = [q[h].astype(BF16) for h in heads]
    k16 = [k_ref[:, qk[h]].astype(BF16) for h in heads]
    c_old = [c_ref[h] for h in heads]
    n_old = [n_ref[h:h + 1, :] for h in heads]
    m_old = [m_ref[h:h + 1, 0:1] for h in heads]
    i_c = [pre_c[:, h:h + 1] for h in heads]
    b_c = [cum_c[:, A_HEADS + h:A_HEADS + h + 1] for h in heads]
    i_r = [pre_r[h:h + 1, :] for h in heads]
    b_r = [cum_r[A_HEADS + h:A_HEADS + h + 1, :] for h in heads]

    qk_t = [_dot_nt(q16[h], k16[h]) for h in heads]
    q_c = [_dot_nt(q16[h], c_old[h].astype(BF16)) for h in heads]

    s, w_st, m_t = [], [], []
    for h in heads:
        d = jnp.where(causal, b_c[h] - b_r[h] + i_r[h], -jnp.inf)
        inter = b_c[h] + m_old[h]
        m_t.append(jnp.maximum(inter, jnp.max(d, axis=-1, keepdims=True)))
        w_st.append(jnp.exp(inter - m_t[h]))
        s.append(qk_t[h] * jnp.exp(d - m_t[h]))
    s_v = [_dot(s[h].astype(BF16), v_ref[:, vv[h]].astype(BF16)) for h in heads]

    for h in heads:
        num = s_v[h] + w_st[h] * q_c[h]
        den = (jnp.sum(s[h], axis=-1, keepdims=True)
               + w_st[h] * jnp.sum(q[h] * n_old[h], axis=-1, keepdims=True))
        hh = num * (1.0 / jnp.maximum(jnp.abs(den), jnp.exp(-m_t[h])))
        out = _rms(hh, hn_ref[:, vv[h]]) * jax.nn.sigmoid(og_ref[:, vv[h]])
        tok_ref[:, vv[h]] = out.astype(BF16)

    decay, w_k = [], []
    for h in heads:
        b_end = b_c[h][L - 1:L, :]
        g = b_end - b_c[h] + i_c[h]
        m_new = jnp.maximum(b_end + m_old[h], jnp.max(g, axis=0, keepdims=True))
        w_k.append(jnp.exp(g - m_new))
        decay.append(jnp.exp(b_end + m_old[h] - m_new))
        m_ref[h:h + 1, :] = jnp.broadcast_to(m_new, (1, LANES))
    vw_k = [lax.dot_general((v_ref[:, vv[h]] * w_k[h]).astype(BF16), k16[h], TN_DIMS,
                            preferred_element_type=F32) for h in heads]
    for h in heads:
        c_ref[h] = decay[h] * c_old[h] + vw_k[h]
        n_ref[h:h + 1, :] = (decay[h] * n_old[h]
                             + jnp.sum(w_k[h] * k_ref[:, qk[h]], axis=0, keepdims=True))


def _mlstm(proj, gates_r, bias_c, bias_r, head_norm, c0, n0, m0, row_block0, chunk, n_chunks):
    n_seq = c0.shape[0]
    aq = A_HEADS * A_DQK
    av = A_HEADS * A_DV
    gate_block = (2 * aq + 2 * av + MEM_HEADS * MEM_HD) // LANES

    def rows(b, c):
        return row_block0 + b * n_chunks + c

    state = lambda b, c: (b, 0, 0)
    return pl.pallas_call(
        functools.partial(_mlstm_kernel, chunk=chunk),
        grid=(n_seq, n_chunks),
        in_specs=[
            pl.BlockSpec((chunk, aq), lambda b, c: (rows(b, c), 0)),
            pl.BlockSpec((chunk, aq), lambda b, c: (rows(b, c), 1)),
            pl.BlockSpec((chunk, av), lambda b, c: (rows(b, c), 1)),
            pl.BlockSpec((chunk, av), lambda b, c: (rows(b, c), 2)),
            pl.BlockSpec((chunk, LANES), lambda b, c: (rows(b, c), gate_block)),
            pl.BlockSpec((None, GATE_ROWS, chunk), lambda b, c: (b, 0, c)),
            pl.BlockSpec((1, LANES), lambda b, c: (0, 0)),
            pl.BlockSpec((GATE_ROWS, 1), lambda b, c: (0, 0)),
            pl.BlockSpec((1, av), lambda b, c: (0, 0)),
            pl.BlockSpec((None, A_HEADS, A_DV, A_DQK), lambda b, c: (b, 0, 0, 0)),
            pl.BlockSpec((None, SUBLANES, A_DQK), state),
            pl.BlockSpec((None, SUBLANES, LANES), state),
        ],
        out_specs=(
            pl.BlockSpec((chunk, av), lambda b, c: (b * n_chunks + c, 0)),
            pl.BlockSpec((None, A_HEADS, A_DV, A_DQK), lambda b, c: (b, 0, 0, 0)),
            pl.BlockSpec((None, SUBLANES, A_DQK), state),
            pl.BlockSpec((None, SUBLANES, LANES), state),
        ),
        out_shape=(
            jax.ShapeDtypeStruct((n_seq * n_chunks * chunk, av), BF16),
            jax.ShapeDtypeStruct((n_seq, A_HEADS, A_DV, A_DQK), F32),
            jax.ShapeDtypeStruct((n_seq, SUBLANES, A_DQK), F32),
            jax.ShapeDtypeStruct((n_seq, SUBLANES, LANES), F32),
        ),
        compiler_params=_params(2),
        name="mlstm",
    )(proj, proj, proj, proj, proj, gates_r, bias_c, bias_r, head_norm, c0, n0, m0)


def _sb_scores(q16, k16, valid):
    z = _dot_nt(q16, k16) * (SB_HD ** -0.5)
    sp = jnp.maximum(z, 0.0) + jnp.log(1.0 + jnp.exp(-jnp.abs(z)))
    if valid is not None:
        sp = jnp.where(valid, sp, 0.0)
    return z, sp


def _sb_newer(sp, upper):
    s1 = sp.astype(BF16)
    s2 = (sp - s1.astype(F32)).astype(BF16)
    return _dot(s1, upper) + _dot(s2, upper)


def _sb_weights(z, sp, newer, valid, r_prev):
    a = jnp.exp(z - sp - newer + r_prev)
    if valid is not None:
        a = jnp.where(valid, a, 0.0)
    return a.astype(BF16)


def _sb_tiles(qs, ks, vs, upper, valid, r_prevs):
    scores = [_sb_scores(q, k, valid) for q, k in zip(qs, ks)]
    newer = [_sb_newer(sp, upper) for _, sp in scores]
    outs = [_dot(_sb_weights(z, sp, nw, valid, r), v)
            for (z, sp), nw, r, v in zip(scores, newer, r_prevs, vs)]
    sums = [r - jnp.sum(sp, axis=-1, keepdims=True) for (_, sp), r in zip(scores, r_prevs)]
    return outs, sums


def _upper(n):
    row = lax.broadcasted_iota(jnp.int32, (n, n), 0)
    col = lax.broadcasted_iota(jnp.int32, (n, n), 1)
    return jnp.where(row > col, 1.0, 0.0).astype(BF16)


def _sb_prompt_kernel(q_ref, k_ref, v_ref, o_ref, q16_ref, acc_ref, r_ref):
    t = SB_TILE
    base = pl.program_id(1) * SB_CHAINS + SB_PAD // SB_TILE
    upper = _upper(t)
    row = lax.broadcasted_iota(jnp.int32, (t, t), 0)
    col = lax.broadcasted_iota(jnp.int32, (t, t), 1)
    q16_ref[...] = q_ref[...].astype(BF16)

    def walk(j, diagonal):
        rows = [slice(c * t, (c + 1) * t) for c in range(SB_CHAINS)]
        starts = [pl.multiple_of((base + c - j) * t, t) for c in range(SB_CHAINS)]
        outs, sums = _sb_tiles(
            [q16_ref[r, :] for r in rows],
            [k_ref[pl.ds(s, t), :] for s in starts],
            [v_ref[pl.ds(s, t), :] for s in starts],
            upper, col < row if diagonal else None,
            [jnp.zeros((t, 1), F32) if diagonal else r_ref[r, :] for r in rows])
        r_max = None
        for r, out, total in zip(rows, outs, sums):
            if diagonal:
                acc_ref[r, :] = out
            else:
                acc_ref[r, :] += out
            r_ref[r, :] = total
            r_c = jnp.max(total)
            r_max = r_c if r_max is None else jnp.maximum(r_max, r_c)
        return r_max

    def cond(carry):
        j, r_max = carry
        return jnp.logical_and(j <= base, r_max > EXP_ZERO_BELOW)

    def body(carry):
        j, _ = carry
        return j + 1, walk(j, False)

    lax.while_loop(cond, body, (jnp.int32(1), walk(0, True)))
    o_ref[...] = acc_ref[...].astype(BF16)


def _sb_prompt(proj, kv16, seq):
    step = SB_CHAINS * SB_TILE
    return pl.pallas_call(
        _sb_prompt_kernel,
        grid=(SB_HEADS, seq // step),
        in_specs=[
            pl.BlockSpec((step, SB_HD), lambda h, i: (i, h)),
            pl.BlockSpec((None, SB_PAD + seq, SB_HD), lambda h, i: (0, 0, h)),
            pl.BlockSpec((None, SB_PAD + seq, SB_HD), lambda h, i: (1, 0, h)),
        ],
        out_specs=pl.BlockSpec((step, SB_HD), lambda h, i: (i, h)),
        out_shape=jax.ShapeDtypeStruct((seq, SB_HEADS * SB_HD), BF16),
        scratch_shapes=[pltpu.VMEM((step, SB_HD), BF16), pltpu.VMEM((step, SB_HD), F32),
                        pltpu.VMEM((step, 1), F32)],
        compiler_params=_params(2),
        name="sb_prompt",
    )(proj, kv16, kv16)


def _sb_sample_kernel(q_ref, kn_ref, vn_ref, pk_hbm, pv_hbm, o_ref, kbuf, vbuf, sem, acc_ref, r_ref,
                      *, rows, n_past):
    b = pl.program_id(0)
    t = SB_PAST_TILE

    def copies(tile_index):
        start = pl.multiple_of(tile_index * t, t)
        return (
            pltpu.make_async_copy(pk_hbm.at[b, :, pl.ds(start, t), :], kbuf, sem.at[0]),
            pltpu.make_async_copy(pv_hbm.at[b, :, pl.ds(start, t), :], vbuf, sem.at[1]),
        )

    def fetch(tile_index):
        for cp in copies(tile_index):
            cp.start()

    def wait(tile_index):
        for cp in copies(tile_index):
            cp.wait()

    heads = [slice(h * SB_HD, (h + 1) * SB_HD) for h in range(SB_HEADS)]

    def past_tile():
        outs, sums = _sb_tiles(
            [q_ref[:, sl].astype(BF16) for sl in heads],
            [kbuf[h].astype(BF16) for h in range(SB_HEADS)],
            [vbuf[h].astype(BF16) for h in range(SB_HEADS)],
            _upper(t), None, [r_ref[h] for h in range(SB_HEADS)])
        r_max = None
        for h, (out, total) in enumerate(zip(outs, sums)):
            acc_ref[:, heads[h]] += out
            r_ref[h] = total
            r_h = jnp.max(total)
            r_max = r_h if r_max is None else jnp.maximum(r_max, r_h)
        return r_max

    fetch(n_past - 1)
    row = lax.broadcasted_iota(jnp.int32, (rows, rows), 0)
    col = lax.broadcasted_iota(jnp.int32, (rows, rows), 1)
    outs, sums = _sb_tiles(
        [q_ref[:, sl].astype(BF16) for sl in heads], [kn_ref[:, sl] for sl in heads],
        [vn_ref[:, sl] for sl in heads], _upper(rows), col < row,
        [jnp.zeros((rows, 1), F32)] * SB_HEADS)
    for h, (out, total) in enumerate(zip(outs, sums)):
        acc_ref[:, heads[h]] = out
        r_ref[h] = total
    wait(n_past - 1)
    r_max = past_tile()

    def cond(carry):
        tile_index, r_max = carry
        return jnp.logical_and(tile_index >= 0, r_max > EXP_ZERO_BELOW)

    def body(carry):
        tile_index, _ = carry
        fetch(tile_index)
        wait(tile_index)
        return tile_index - 1, past_tile()

    lax.while_loop(cond, body, (jnp.int32(n_past - 2), r_max))
    o_ref[...] = acc_ref[...].astype(BF16)


def _sb_sample(proj, kv16, past_k, past_v, row_block0, rows):
    n_seq, heads, past, hd = past_k.shape
    w = heads * hd
    new_block0 = row_block0 + SB_PAD // rows
    return pl.pallas_call(
        functools.partial(_sb_sample_kernel, rows=rows, n_past=past // SB_PAST_TILE),
        grid=(n_seq,),
        in_specs=[
            pl.BlockSpec((rows, w), lambda b: (row_block0 + b, 0)),
            pl.BlockSpec((None, rows, w), lambda b: (0, new_block0 + b, 0)),
            pl.BlockSpec((None, rows, w), lambda b: (1, new_block0 + b, 0)),
            pl.BlockSpec(memory_space=pl.ANY),
            pl.BlockSpec(memory_space=pl.ANY),
        ],
        out_specs=pl.BlockSpec((rows, w), lambda b: (b, 0)),
        out_shape=jax.ShapeDtypeStruct((n_seq * rows, w), BF16),
        scratch_shapes=[
            pltpu.VMEM((heads, SB_PAST_TILE, hd), F32),
            pltpu.VMEM((heads, SB_PAST_TILE, hd), F32),
            pltpu.SemaphoreType.DMA((2,)),
            pltpu.VMEM((rows, w), F32),
            pltpu.VMEM((heads, rows, 1), F32),
        ],
        compiler_params=_params(1),
        name="sb_sample",
    )(proj, kv16, kv16, past_k, past_v)


def _pad_rows(a, n):
    return jnp.pad(a, [(0, 0)] * (a.ndim - 2) + [(0, n - a.shape[-2]), (0, 0)])


def kernel(x_prompt, x_sample, state_mlstm_C, state_mlstm_n, state_mlstm_m, cache_sb_k, cache_sb_v,
           cache_mem_k, cache_mem_v, mem_prompt, ffn1_norm, ffn1_w_gate, ffn1_w_up, ffn1_w_down,
           ffn2_norm, ffn2_w_gate, ffn2_w_up, ffn2_w_down, mix_norm, a_w_in, a_b_i, a_b_f, a_head_norm,
           b_w_in, w_out, mem_norm, mem_w_k, mem_w_v, mem_q_norm, mem_k_norm, kv_norm, sb_w_k, sb_w_v):
    n_pb, seq, d = x_prompt.shape
    n_sb, dec_seq, _ = x_sample.shape
    assert n_pb == 1
    depth = ffn1_norm.shape[0]
    n_a = a_w_in.shape[0]
    n_slots = mem_prompt.shape[1]
    aq = A_HEADS * A_DQK
    av = A_HEADS * A_DV
    mem_w = MEM_HEADS * MEM_HD
    sb_w = SB_HEADS * SB_HD
    n_sample = n_sb * dec_seq
    sample_block0 = seq // dec_seq

    gains = lambda g: g.reshape(g.shape[0], 1, g.shape[-1])
    ffn1 = (gains(ffn1_norm), ffn1_w_gate, ffn1_w_up, ffn1_w_down)
    ffn2 = (gains(ffn2_norm), ffn2_w_gate, ffn2_w_up, ffn2_w_down)
    mix_gain = gains(mix_norm)
    n_main = 2 * aq + 2 * av
    w_gates = a_w_in[:, :, n_main:n_main + 2 * A_HEADS]
    a_cols = n_main + mem_w + LANES
    a_tile = 1792
    a_pad = -a_cols % a_tile
    a_w = jnp.concatenate(
        [a_w_in[:, :, :n_main], a_w_in[:, :, n_main + 2 * A_HEADS:], w_gates,
         jnp.zeros((n_a, d, LANES - 2 * A_HEADS + a_pad), F32)], axis=-1).astype(BF16)
    a_wgt = _pad_rows(jnp.swapaxes(w_gates, 1, 2), GATE_ROWS).astype(BF16)
    gate_bias = jnp.concatenate([a_b_i, a_b_f], axis=-1)
    b_w = b_w_in.astype(BF16)
    w_kv = jnp.stack([sb_w_k, sb_w_v]).astype(BF16)
    q_gain = mem_q_norm.reshape(depth, 1, MEM_HD)

    mk_p, mv_p = _mem_kv(mem_prompt[0], gains(mem_norm), mem_w_k, mem_w_v,
                         mem_k_norm.reshape(depth, 1, MEM_HD))
    mk_s = cache_mem_k.reshape(depth * n_sb, n_slots * MEM_HEADS, MEM_HD)
    mv_s = cache_mem_v.reshape(depth * n_sb, n_slots * MEM_HEADS, MEM_HD)

    past_k = jnp.transpose(cache_sb_k, (0, 2, 1, 3))
    past_v = jnp.transpose(cache_sb_v, (0, 2, 1, 3))

    x = jnp.concatenate([x_prompt[0], x_sample.reshape(n_sample, d)], axis=0)
    c_p, n_p, m_p, c_s, n_s, m_s = [], [], [], [], [], []
    k_p = v_p = k_s = v_s = kv16 = None
    mem_tile = 1024
    for l in range(depth):
        x = _ffn(x, *ffn1, l)
        if l < n_a:
            proj, gates_r = _proj(x, mix_gain, l, a_w, l, a_tile, wgt=a_wgt)
            bias_c = jnp.pad(gate_bias[l], (0, LANES - 2 * A_HEADS)).reshape(1, LANES)
            bias_r = jnp.pad(gate_bias[l], (0, GATE_ROWS - 2 * A_HEADS)).reshape(GATE_ROWS, 1)
            head_norm = a_head_norm[l].reshape(1, av)
            chunk_p = 256
            tok_p, c, n, m = _mlstm(
                proj, gates_r[:, :seq].reshape(1, GATE_ROWS, seq), bias_c, bias_r, head_norm,
                jnp.zeros((1, A_HEADS, A_DV, A_DQK), F32), jnp.zeros((1, SUBLANES, A_DQK), F32),
                jnp.zeros((1, SUBLANES, LANES), F32), 0, chunk_p, seq // chunk_p)
            c_p.append(c); n_p.append(n[:, :A_HEADS]); m_p.append(m[:, :A_HEADS, 0])
            gates_s = gates_r[:, seq:].reshape(GATE_ROWS, n_sb, dec_seq).transpose(1, 0, 2)
            m0 = jnp.broadcast_to(state_mlstm_m[l][:, :, None], (n_sb, A_HEADS, LANES))
            tok_s, c, n, m = _mlstm(
                proj, gates_s, bias_c, bias_r, head_norm, state_mlstm_C[l],
                _pad_rows(state_mlstm_n[l], SUBLANES), _pad_rows(m0, SUBLANES), sample_block0, dec_seq, 1)
            c_s.append(c); n_s.append(n[:, :A_HEADS]); m_s.append(m[:, :A_HEADS, 0])
            mq_block = n_main // mem_w
        else:
            proj = _proj(x, mix_gain, l, b_w, l - n_a, b_w.shape[-1])
            tok_p = _sb_prompt(proj, kv16, seq)
            tok_s = _sb_sample(proj, kv16, past_k, past_v, sample_block0, dec_seq)
            mq_block = sb_w // mem_w
        mo_p = _mem_attn(proj, mq_block, 0, mem_tile, seq // mem_tile, q_gain, l, mk_p, mv_p, l, 0)
        mo_s = _mem_attn(proj, mq_block, sample_block0 // MEM_GROUP, dec_seq * MEM_GROUP, n_sb // MEM_GROUP,
                         q_gain, l, mk_s, mv_s, l * n_sb // MEM_GROUP, MEM_GROUP)
        x = _out_proj(x, tok_p, tok_s, mo_p, mo_s, w_out, l)
        x = _ffn(x, *ffn2, l)
        if l == n_a - 1:
            k_p, v_p, k_s, v_s, kv16 = _kv_proj(x, kv_norm.reshape(1, d), w_kv, seq, n_sb, dec_seq)

    y_prompt = x[:seq].reshape(1, seq, d)
    y_sample = x[seq:].reshape(n_sb, dec_seq, d)
    k_p, v_p = (jnp.transpose(a, (1, 0, 2))[None] for a in (k_p, v_p))
    k_s, v_s = (jnp.transpose(a, (0, 2, 1, 3)) for a in (k_s, v_s))
    mem_out = lambda a: jnp.transpose(a, (0, 2, 1, 3))[:, None]
    return (y_prompt, y_sample, jnp.stack(c_p), jnp.stack(n_p), jnp.stack(m_p), k_p, v_p,
            mem_out(mk_p), mem_out(mv_p),
            jnp.stack(c_s), jnp.stack(n_s), jnp.stack(m_s), k_s, v_s)
```

```python
import functools

import jax
import jax.numpy as jnp
from jax import lax
from jax.experimental import pallas as pl
from jax.experimental.pallas import tpu as pltpu

F32 = jnp.float32
BF16 = jnp.bfloat16

RMS_EPS = 1e-6
A_HEADS = 6
A_DQK = 128
A_DV = 256
SB_HEADS = 12
SB_HD = 128
MEM_HEADS = 4
MEM_HD = 128
LANES = 128
SUBLANES = 8
GATE_ROWS = 16

ROW_TILE = 768
FFN_ROW_TILE = 1056
FF_TILE = 512
FF_HEAD_TILE = 256
VMEM_LIMIT = 58 * 1024 * 1024

MEM_GROUP = 4

SB_TILE = 256
SB_CHAINS = 4
SB_PAST_TILE = 256
SB_PAD = 768

NT_DIMS = (((1,), (1,)), ((), ()))
TN_DIMS = (((0,), (0,)), ((), ()))
EXP_ZERO_BELOW = -104.0


def _params(n_axes):
    return pltpu.CompilerParams(
        dimension_semantics=("arbitrary",) * n_axes, vmem_limit_bytes=VMEM_LIMIT)


def _rms(x, g):
    ms = jnp.mean(x * x, axis=-1, keepdims=True)
    return x * lax.rsqrt(ms + RMS_EPS) * g


def _log_sigmoid(x):
    return jnp.minimum(x, 0.0) - jnp.log1p(jnp.exp(-jnp.abs(x)))


def _dot(a, b):
    return jnp.dot(a, b, preferred_element_type=F32)


def _dot_nt(a, b):
    return lax.dot_general(a, b, NT_DIMS, preferred_element_type=F32)


def _split3(x):
    x1 = x.astype(BF16)
    r = x - x1.astype(F32)
    x2 = r.astype(BF16)
    x3 = (r - x2.astype(F32)).astype(BF16)
    return x1, x2, x3


def _ffn_step(h_ref, wg, wu, wd, o_ref):
    h = h_ref[...]
    g = jnp.concatenate([_dot(h, w) for w in wg], axis=1)
    u = jnp.concatenate([_dot(h, w) for w in wu], axis=1)
    a = (g * jax.nn.sigmoid(g) * u * 0.5).astype(BF16)
    o_ref[...] += _dot(a, wd)


def _ffn_head_kernel(x_ref, g_ref, wg_ref, wu_ref, wd_ref, o_ref, wg16_ref, wu16_ref, wd16_ref, h_ref):
    @pl.when(pl.program_id(0) == 0)
    def _():
        x = x_ref[...]
        h_ref[...] = _rms(x, g_ref[...]).astype(BF16)
        o_ref[...] = x

    wg16_ref[...] = wg_ref[...].astype(BF16)
    wu16_ref[...] = wu_ref[...].astype(BF16)
    wd16_ref[...] = wd_ref[...].astype(BF16)
    _ffn_step(h_ref, [wg16_ref[...]], [wu16_ref[...]], wd16_ref[...], o_ref)


def _ffn_rest_kernel(x_ref, g_ref, wg_ref, wu_ref, wd_ref, o_ref, h_ref):
    @pl.when(pl.program_id(1) == 0)
    def _():
        x = x_ref[...]
        h_ref[...] = _rms(x, g_ref[...]).astype(BF16)
        o_ref[...] = x

    sub = range(wg_ref.shape[0])
    _ffn_step(h_ref, [wg_ref[t] for t in sub], [wu_ref[t] for t in sub], wd_ref[...], o_ref)


def _ffn(x, gain, wg, wu, wd, layer):
    m, d = x.shape
    f = wg.shape[-1]
    n_head = f // FF_HEAD_TILE
    w16 = lambda shape: jax.ShapeDtypeStruct(shape, BF16)
    x, wg16, wu16, wd16 = pl.pallas_call(
        _ffn_head_kernel,
        grid=(n_head,),
        in_specs=[
            pl.BlockSpec((FFN_ROW_TILE, d), lambda j: (0, 0)),
            pl.BlockSpec((None, 1, d), lambda j: (layer, 0, 0)),
            pl.BlockSpec((None, d, FF_HEAD_TILE), lambda j: (layer, 0, j)),
            pl.BlockSpec((None, d, FF_HEAD_TILE), lambda j: (layer, 0, j)),
            pl.BlockSpec((None, FF_HEAD_TILE, d), lambda j: (layer, j, 0)),
        ],
        out_specs=(
            pl.BlockSpec((FFN_ROW_TILE, d), lambda j: (0, 0)),
            pl.BlockSpec((None, d, FF_HEAD_TILE), lambda j: (j, 0, 0)),
            pl.BlockSpec((None, d, FF_HEAD_TILE), lambda j: (j, 0, 0)),
            pl.BlockSpec((FF_HEAD_TILE, d), lambda j: (j, 0)),
        ),
        out_shape=(jax.ShapeDtypeStruct((m, d), F32), w16((n_head, d, FF_HEAD_TILE)),
                   w16((n_head, d, FF_HEAD_TILE)), w16((f, d))),
        scratch_shapes=[pltpu.VMEM((FFN_ROW_TILE, d), BF16)],
        input_output_aliases={0: 0},
        compiler_params=_params(1),
        name="ffn_head",
    )(x, gain, wg, wu, wd)
    return pl.pallas_call(
        _ffn_rest_kernel,
        grid=(m // FFN_ROW_TILE - 1, f // FF_TILE),
        in_specs=[
            pl.BlockSpec((FFN_ROW_TILE, d), lambda i, j: (i + 1, 0)),
            pl.BlockSpec((None, 1, d), lambda i, j: (layer, 0, 0)),
            pl.BlockSpec((FF_TILE // FF_HEAD_TILE, d, FF_HEAD_TILE), lambda i, j: (j, 0, 0)),
            pl.BlockSpec((FF_TILE // FF_HEAD_TILE, d, FF_HEAD_TILE), lambda i, j: (j, 0, 0)),
            pl.BlockSpec((FF_TILE, d), lambda i, j: (j, 0)),
        ],
        out_specs=pl.BlockSpec((FFN_ROW_TILE, d), lambda i, j: (i + 1, 0)),
        out_shape=jax.ShapeDtypeStruct((m, d), F32),
        scratch_shapes=[pltpu.VMEM((FFN_ROW_TILE, d), BF16)],
        input_output_aliases={0: 0},
        compiler_params=_params(2),
        name="ffn",
    )(x, gain, wg16, wu16, wd16)


def _proj_kernel(x_ref, g_ref, w_ref, o_ref, h_ref):
    @pl.when(pl.program_id(1) == 0)
    def _():
        h_ref[...] = _rms(x_ref[...], g_ref[...]).astype(BF16)

    o_ref[...] = _dot(h_ref[...], w_ref[...]).astype(o_ref.dtype)


def _proj_gates_kernel(x_ref, g_ref, w_ref, wgt_ref, o_ref, gt_ref, h_ref):
    @pl.when(pl.program_id(1) == 0)
    def _():
        h = _rms(x_ref[...], g_ref[...]).astype(BF16)
        h_ref[...] = h
        gt_ref[...] = _dot_nt(wgt_ref[...], h)

    o_ref[...] = _dot(h_ref[...], w_ref[...]).astype(o_ref.dtype)


def _proj(x, gain, gain_layer, w, layer, col_tile, wgt=None):
    m, d = x.shape
    n = w.shape[-1]
    in_specs = [
        pl.BlockSpec((ROW_TILE, d), lambda i, j: (i, 0)),
        pl.BlockSpec((None, 1, d), lambda i, j: (gain_layer, 0, 0)),
        pl.BlockSpec((None, d, col_tile), lambda i, j: (layer, 0, j)),
    ]
    out_spec = pl.BlockSpec((ROW_TILE, col_tile), lambda i, j: (i, j))
    out_shape = jax.ShapeDtypeStruct((m, n), F32)
    common = dict(
        grid=(m // ROW_TILE, n // col_tile),
        scratch_shapes=[pltpu.VMEM((ROW_TILE, d), BF16)],
        compiler_params=_params(2),
    )
    if wgt is None:
        return pl.pallas_call(
            _proj_kernel, in_specs=in_specs, out_specs=out_spec, out_shape=out_shape,
            name="proj", **common)(x, gain, w)
    return pl.pallas_call(
        _proj_gates_kernel,
        in_specs=in_specs + [pl.BlockSpec((None, GATE_ROWS, d), lambda i, j: (layer, 0, 0))],
        out_specs=(out_spec, pl.BlockSpec((GATE_ROWS, ROW_TILE), lambda i, j: (0, i))),
        out_shape=(out_shape, jax.ShapeDtypeStruct((GATE_ROWS, m), F32)),
        name="proj_gates", **common)(x, gain, w, wgt)


def _kv_kernel(x_ref, g_ref, w_ref, kp_ref, vp_ref, ks_ref, vs_ref, o16_ref,
               *, n_pad, n_prompt, n_seq, rows):
    i = pl.program_id(0)

    @pl.when(i < n_pad)
    def _():
        o16_ref[...] = jnp.zeros(o16_ref.shape, BF16)

    @pl.when(i >= n_pad)
    def _():
        h = _rms(x_ref[...], g_ref[...]).astype(BF16)
        heads = [slice(hd * SB_HD, (hd + 1) * SB_HD) for hd in range(SB_HEADS)]
        for j, (p_ref, s_ref) in enumerate(((kp_ref, ks_ref), (vp_ref, vs_ref))):
            y = _dot(h, w_ref[j])
            o16_ref[j] = y.astype(BF16)

            @pl.when(i < n_pad + n_prompt)
            def _():
                for hd, sl in enumerate(heads):
                    p_ref[hd] = y[:, sl]

            @pl.when(i >= n_pad + n_prompt)
            def _():
                for b in range(n_seq):
                    for hd, sl in enumerate(heads):
                        s_ref[b, hd] = y[b * rows:(b + 1) * rows, sl]


def _kv_proj(x, gain, w, seq, n_seq, rows):
    m, d = x.shape
    n = w.shape[-1]
    tile = n_seq * rows
    n_prompt = seq // tile
    n_pad = SB_PAD // tile
    assert n_prompt * tile == seq and seq + tile == m and n_pad * tile == SB_PAD
    row_tile = lambda i: jnp.clip(i - n_pad, 0, n_prompt - 1)
    prompt_spec = pl.BlockSpec((SB_HEADS, tile, SB_HD), lambda i: (0, row_tile(i), 0))
    sample_spec = pl.BlockSpec((n_seq, SB_HEADS, rows, SB_HD), lambda i: (0, 0, 0, 0))
    prompt_shape = jax.ShapeDtypeStruct((SB_HEADS, seq, SB_HD), F32)
    sample_shape = jax.ShapeDtypeStruct((n_seq, SB_HEADS, rows, SB_HD), F32)
    return pl.pallas_call(
        functools.partial(_kv_kernel, n_pad=n_pad, n_prompt=n_prompt, n_seq=n_seq, rows=rows),
        grid=(n_pad + n_prompt + 1,),
        in_specs=[
            pl.BlockSpec((tile, d), lambda i: (jnp.maximum(i - n_pad, 0), 0)),
            pl.BlockSpec((1, d), lambda i: (0, 0)),
            pl.BlockSpec((2, d, n), lambda i: (0, 0, 0)),
        ],
        out_specs=(prompt_spec, prompt_spec, sample_spec, sample_spec,
                   pl.BlockSpec((2, tile, n), lambda i: (0, i, 0))),
        out_shape=(prompt_shape, prompt_shape, sample_shape, sample_shape,
                   jax.ShapeDtypeStruct((2, SB_PAD + m, n), BF16)),
        compiler_params=_params(1),
        name="kv_proj",
    )(x, gain, w)


def _out_proj_kernel(x_ref, tokp_ref, toks_ref, mop_ref, mos_ref, wt_ref, wm_ref, o_ref, w16_ref,
                     *, n_prompt):
    i = pl.program_id(0)
    kt = wt_ref.shape[0]

    @pl.when(i == 0)
    def _():
        w16_ref[:kt] = wt_ref[...].astype(BF16)
        w16_ref[kt:] = wm_ref[...].astype(BF16)

    @pl.when(i < n_prompt)
    def _():
        o_ref[...] = (x_ref[...] + _dot(tokp_ref[...], w16_ref[:kt])
                      + _dot(mop_ref[...], w16_ref[kt:]))

    @pl.when(i >= n_prompt)
    def _():
        o_ref[...] = (x_ref[...] + _dot(toks_ref[...], w16_ref[:kt])
                      + _dot(mos_ref[...], w16_ref[kt:]))


def _out_proj(x, tok_p, tok_s, mo_p, mo_s, w, layer):
    m, d = x.shape
    kt = tok_p.shape[1]
    km = mo_p.shape[1]
    tile = tok_s.shape[0]
    n_prompt = tok_p.shape[0] // tile
    assert n_prompt * tile == tok_p.shape[0] and (n_prompt + 1) * tile == m
    prompt = lambda i: (jnp.minimum(i, n_prompt - 1), 0)
    return pl.pallas_call(
        functools.partial(_out_proj_kernel, n_prompt=n_prompt),
        grid=(n_prompt + 1,),
        in_specs=[
            pl.BlockSpec((tile, d), lambda i: (i, 0)),
            pl.BlockSpec((tile, kt), prompt),
            pl.BlockSpec((tile, kt), lambda i: (0, 0)),
            pl.BlockSpec((tile, km), prompt),
            pl.BlockSpec((tile, km), lambda i: (0, 0)),
            pl.BlockSpec((None, kt, d), lambda i: (layer, 0, 0)),
            pl.BlockSpec((None, km, d), lambda i: (layer, kt // km, 0)),
        ],
        out_specs=pl.BlockSpec((tile, d), lambda i: (i, 0)),
        out_shape=jax.ShapeDtypeStruct((m, d), F32),
        scratch_shapes=[pltpu.VMEM((kt + km, d), BF16)],
        compiler_params=_params(1),
        name="out_proj",
    )(x, tok_p, tok_s, mo_p, mo_s, w, w)


def _mem_kv_kernel(mem_ref, g_ref, wk_ref, wv_ref, kg_ref, mk_ref, mv_ref):
    h = _rms(mem_ref[...], g_ref[...]).astype(BF16)
    k = _dot(h, wk_ref[...].astype(BF16))
    v = _dot(h, wv_ref[...].astype(BF16))
    kg = kg_ref[...]
    for hd in range(MEM_HEADS):
        sl = slice(hd * MEM_HD, (hd + 1) * MEM_HD)
        mk_ref[hd] = _rms(k[:, sl], kg)
        mv_ref[hd] = v[:, sl]


def _mem_kv(mem, gain, wk, wv, k_gain):
    depth, d, w = wk.shape
    n = mem.shape[0]
    out_spec = pl.BlockSpec((None, MEM_HEADS, n, MEM_HD), lambda l: (l, 0, 0, 0))
    out_shape = jax.ShapeDtypeStruct((depth, MEM_HEADS, n, MEM_HD), F32)
    return pl.pallas_call(
        _mem_kv_kernel,
        grid=(depth,),
        in_specs=[
            pl.BlockSpec((n, d), lambda l: (0, 0)),
            pl.BlockSpec((None, 1, d), lambda l: (l, 0, 0)),
            pl.BlockSpec((None, d, w), lambda l: (l, 0, 0)),
            pl.BlockSpec((None, d, w), lambda l: (l, 0, 0)),
            pl.BlockSpec((None, 1, MEM_HD), lambda l: (l, 0, 0)),
        ],
        out_specs=(out_spec, out_spec),
        out_shape=(out_shape, out_shape),
        compiler_params=_params(1),
        name="mem_kv",
    )(mem, gain, wk, wv, k_gain)


def _mem_attn_chains(mq_ref, qg_ref, chains, o_ref):
    qg = qg_ref[...]
    qn = [_rms(mq_ref[r, c], qg).astype(BF16) for r, c, _, _, _ in chains]
    s = [_dot_nt(q, k) * (MEM_HD ** -0.5) for q, (_, _, k, _, _) in zip(qn, chains)]
    s = [x if m is None else jnp.where(m, x, -jnp.inf) for x, (_, _, _, _, m) in zip(s, chains)]
    e = [jnp.exp(x - jnp.max(x, axis=-1, keepdims=True)) for x in s]
    p = [(x / jnp.sum(x, axis=-1, keepdims=True)).astype(BF16) for x in e]
    out = [_dot(x, v) for x, (_, _, _, v, _) in zip(p, chains)]
    for x, (r, c, _, _, _) in zip(out, chains):
        o_ref[r, c] = x.astype(BF16)


def _mem_attn_prompt_kernel(mq_ref, qg_ref, mk_ref, mv_ref, o_ref):
    _mem_attn_chains(mq_ref, qg_ref, [
        (slice(None), slice(hd * MEM_HD, (hd + 1) * MEM_HD), mk_ref[hd].astype(BF16),
         mv_ref[hd].astype(BF16), None) for hd in range(MEM_HEADS)], o_ref)


def _mem_attn_sample_kernel(mq_ref, qg_ref, mk_ref, mv_ref, o_ref):
    group, n_rows, _ = mk_ref.shape
    rows = mq_ref.shape[0] // group
    row_head = lax.broadcasted_iota(jnp.int32, (rows, n_rows), 1) % MEM_HEADS
    masks = [row_head == hd for hd in range(MEM_HEADS)]
    chains = []
    for g in range(group):
        k16 = mk_ref[g].astype(BF16)
        v16 = mv_ref[g].astype(BF16)
        chains += [(slice(g * rows, (g + 1) * rows), slice(hd * MEM_HD, (hd + 1) * MEM_HD),
                    k16, v16, masks[hd]) for hd in range(MEM_HEADS)]
    _mem_attn_chains(mq_ref, qg_ref, chains, o_ref)


def _mem_attn(proj, col_block, row_block0, rows, n_tiles, q_gain, layer, mk, mv, kv_index0, group):
    w = MEM_HEADS * MEM_HD
    if group:
        body = _mem_attn_sample_kernel
        kv_spec = pl.BlockSpec((group,) + mk.shape[1:], lambda b: (kv_index0 + b, 0, 0))
    else:
        body = _mem_attn_prompt_kernel
        kv_spec = pl.BlockSpec((None,) + mk.shape[1:], lambda b: (kv_index0, 0, 0, 0))
    return pl.pallas_call(
        body,
        grid=(n_tiles,),
        in_specs=[
            pl.BlockSpec((rows, w), lambda b: (row_block0 + b, col_block)),
            pl.BlockSpec((None, 1, MEM_HD), lambda b: (layer, 0, 0)),
            kv_spec,
            kv_spec,
        ],
        out_specs=pl.BlockSpec((rows, w), lambda b: (b, 0)),
        out_shape=jax.ShapeDtypeStruct((n_tiles * rows, w), BF16),
        compiler_params=_params(1),
        name="mem_attn",
    )(proj, q_gain, mk, mv)


def _mlstm_kernel(q_ref, k_ref, v_ref, og_ref, gc_ref, gr_ref, bc_ref, br_ref, hn_ref,
                  c0_ref, n0_ref, m0_ref, tok_ref, c_ref, n_ref, m_ref, *, chunk):
    L = chunk

    @pl.when(pl.program_id(1) == 0)
    def _():
        c_ref[...] = c0_ref[...]
        n_ref[...] = n0_ref[...]
        m_ref[...] = m0_ref[...]

    row = lax.broadcasted_iota(jnp.int32, (L, L), 0)
    col = lax.broadcasted_iota(jnp.int32, (L, L), 1)
    causal = col <= row
    tri = jnp.where(causal, 1.0, 0.0).astype(BF16)

    pre_c = gc_ref[...] + bc_ref[...]
    pre_r = gr_ref[...] + br_ref[...]
    lf_c = _log_sigmoid(pre_c)
    lf_r = _log_sigmoid(pre_r)
    cum_c = sum(_dot(tri, part) for part in _split3(lf_c))
    cum_r = sum(_dot_nt(part, tri) for part in _split3(lf_r))

    scale = A_DQK ** -0.5
    heads = range(A_HEADS)
    qk = [slice(h * A_DQK, (h + 1) * A_DQK) for h in heads]
    vv = [slice(h * A_DV, (h + 1) * A_DV) for h in heads]
    q = [q_ref[:, qk[h]] * scale for h in heads]
    q16 = [q[h].astype(BF16) for h in heads]
    k16 = [k_ref[:, qk[h]].astype(BF16) for h in heads]
    c_old = [c_ref[h] for h in heads]
    n_old = [n_ref[h:h + 1, :] for h in heads]
    m_old = [m_ref[h:h + 1, 0:1] for h in heads]
    i_c = [pre_c[:, h:h + 1] for h in heads]
    b_c = [cum_c[:, A_HEADS + h:A_HEADS + h + 1] for h in heads]
    i_r = [pre_r[h:h + 1, :] for h in heads]
    b_r = [cum_r[A_HEADS + h:A_HEADS + h + 1, :] for h in heads]

    qk_t = [_dot_nt(q16[h], k16[h]) for h in heads]
    q_c = [_dot_nt(q16[h], c_old[h].astype(BF16)) for h in heads]

    s, w_st, m_t = [], [], []
    for h in heads:
        d = jnp.where(causal, b_c[h] - b_r[h] + i_r[h], -jnp.inf)
        inter = b_c[h] + m_old[h]
        m_t.append(jnp.maximum(inter, jnp.max(d, axis=-1, keepdims=True)))
        w_st.append(jnp.exp(inter - m_t[h]))
        s.append(qk_t[h] * jnp.exp(d - m_t[h]))
    s_v = [_dot(s[h].astype(BF16), v_ref[:, vv[h]].astype(BF16)) for h in heads]

    for h in heads:
        num = s_v[h] + w_st[h] * q_c[h]
        den = (jnp.sum(s[h], axis=-1, keepdims=True)
               + w_st[h] * jnp.sum(q[h] * n_old[h], axis=-1, keepdims=True))
        hh = num * (1.0 / jnp.maximum(jnp.abs(den), jnp.exp(-m_t[h])))
        out = _rms(hh, hn_ref[:, vv[h]]) * jax.nn.sigmoid(og_ref[:, vv[h]])
        tok_ref[:, vv[h]] = out.astype(BF16)

    decay, w_k = [], []
    for h in heads:
        b_end = b_c[h][L - 1:L, :]
        g = b_end - b_c[h] + i_c[h]
        m_new = jnp.maximum(b_end + m_old[h], jnp.max(g, axis=0, keepdims=True))
        w_k.append(jnp.exp(g - m_new))
        decay.append(jnp.exp(b_end + m_old[h] - m_new))
        m_ref[h:h + 1, :] = jnp.broadcast_to(m_new, (1, LANES))
    vw_k = [lax.dot_general((v_ref[:, vv[h]] * w_k[h]).astype(BF16), k16[h], TN_DIMS,
                            preferred_element_type=F32) for h in heads]
    for h in heads:
        c_ref[h] = decay[h] * c_old[h] + vw_k[h]
        n_ref[h:h + 1, :] = (decay[h] * n_old[h]
                             + jnp.sum(w_k[h] * k_ref[:, qk[h]], axis=0, keepdims=True))


def _mlstm(proj, gates_r, bias_c, bias_r, head_norm, c0, n0, m0, row_block0, chunk, n_chunks):
    n_seq = c0.shape[0]
    aq = A_HEADS * A_DQK
    av = A_HEADS * A_DV
    gate_block = (2 * aq + 2 * av + MEM_HEADS * MEM_HD) // LANES

    def rows(b, c):
        return row_block0 + b * n_chunks + c

    state = lambda b, c: (b, 0, 0)
    return pl.pallas_call(
        functools.partial(_mlstm_kernel, chunk=chunk),
        grid=(n_seq, n_chunks),
        in_specs=[
            pl.BlockSpec((chunk, aq), lambda b, c: (rows(b, c), 0)),
            pl.BlockSpec((chunk, aq), lambda b, c: (rows(b, c), 1)),
            pl.BlockSpec((chunk, av), lambda b, c: (rows(b, c), 1)),
            pl.BlockSpec((chunk, av), lambda b, c: (rows(b, c), 2)),
            pl.BlockSpec((chunk, LANES), lambda b, c: (rows(b, c), gate_block)),
            pl.BlockSpec((None, GATE_ROWS, chunk), lambda b, c: (b, 0, c)),
            pl.BlockSpec((1, LANES), lambda b, c: (0, 0)),
            pl.BlockSpec((GATE_ROWS, 1), lambda b, c: (0, 0)),
            pl.BlockSpec((1, av), lambda b, c: (0, 0)),
            pl.BlockSpec((None, A_HEADS, A_DV, A_DQK), lambda b, c: (b, 0, 0, 0)),
            pl.BlockSpec((None, SUBLANES, A_DQK), state),
            pl.BlockSpec((None, SUBLANES, LANES), state),
        ],
        out_specs=(
            pl.BlockSpec((chunk, av), lambda b, c: (b * n_chunks + c, 0)),
            pl.BlockSpec((None, A_HEADS, A_DV, A_DQK), lambda b, c: (b, 0, 0, 0)),
            pl.BlockSpec((None, SUBLANES, A_DQK), state),
            pl.BlockSpec((None, SUBLANES, LANES), state),
        ),
        out_shape=(
            jax.ShapeDtypeStruct((n_seq * n_chunks * chunk, av), BF16),
            jax.ShapeDtypeStruct((n_seq, A_HEADS, A_DV, A_DQK), F32),
            jax.ShapeDtypeStruct((n_seq, SUBLANES, A_DQK), F32),
            jax.ShapeDtypeStruct((n_seq, SUBLANES, LANES), F32),
        ),
        compiler_params=_params(2),
        name="mlstm",
    )(proj, proj, proj, proj, proj, gates_r, bias_c, bias_r, head_norm, c0, n0, m0)


def _sb_scores(q16, k16, valid):
    z = _dot_nt(q16, k16) * (SB_HD ** -0.5)
    sp = jnp.maximum(z, 0.0) + jnp.log(1.0 + jnp.exp(-jnp.abs(z)))
    if valid is not None:
        sp = jnp.where(valid, sp, 0.0)
    return z, sp


def _sb_newer(sp, upper):
    s1 = sp.astype(BF16)
    s2 = (sp - s1.astype(F32)).astype(BF16)
    return _dot(s1, upper) + _dot(s2, upper)


def _sb_weights(z, sp, newer, valid, r_prev):
    a = jnp.exp(z - sp - newer + r_prev)
    if valid is not None:
        a = jnp.where(valid, a, 0.0)
    return a.astype(BF16)


def _sb_tiles(qs, ks, vs, upper, valid, r_prevs):
    scores = [_sb_scores(q, k, valid) for q, k in zip(qs, ks)]
    newer = [_sb_newer(sp, upper) for _, sp in scores]
    outs = [_dot(_sb_weights(z, sp, nw, valid, r), v)
            for (z, sp), nw, r, v in zip(scores, newer, r_prevs, vs)]
    sums = [r - jnp.sum(sp, axis=-1, keepdims=True) for (_, sp), r in zip(scores, r_prevs)]
    return outs, sums


def _upper(n):
    row = lax.broadcasted_iota(jnp.int32, (n, n), 0)
    col = lax.broadcasted_iota(jnp.int32, (n, n), 1)
    return jnp.where(row > col, 1.0, 0.0).astype(BF16)


def _sb_prompt_kernel(q_ref, k_ref, v_ref, o_ref, q16_ref, acc_ref, r_ref):
    t = SB_TILE
    base = pl.program_id(1) * SB_CHAINS + SB_PAD // SB_TILE
    upper = _upper(t)
    row = lax.broadcasted_iota(jnp.int32, (t, t), 0)
    col = lax.broadcasted_iota(jnp.int32, (t, t), 1)
    q16_ref[...] = q_ref[...].astype(BF16)

    def walk(j, diagonal):
        rows = [slice(c * t, (c + 1) * t) for c in range(SB_CHAINS)]
        starts = [pl.multiple_of((base + c - j) * t, t) for c in range(SB_CHAINS)]
        outs, sums = _sb_tiles(
            [q16_ref[r, :] for r in rows],
            [k_ref[pl.ds(s, t), :] for s in starts],
            [v_ref[pl.ds(s, t), :] for s in starts],
            upper, col < row if diagonal else None,
            [jnp.zeros((t, 1), F32) if diagonal else r_ref[r, :] for r in rows])
        r_max = None
        for r, out, total in zip(rows, outs, sums):
            if diagonal:
                acc_ref[r, :] = out
            else:
                acc_ref[r, :] += out
            r_ref[r, :] = total
            r_c = jnp.max(total)
            r_max = r_c if r_max is None else jnp.maximum(r_max, r_c)
        return r_max

    def cond(carry):
        j, r_max = carry
        return jnp.logical_and(j <= base, r_max > EXP_ZERO_BELOW)

    def body(carry):
        j, _ = carry
        return j + 1, walk(j, False)

    lax.while_loop(cond, body, (jnp.int32(1), walk(0, True)))
    o_ref[...] = acc_ref[...].astype(BF16)


def _sb_prompt(proj, kv16, seq):
    step = SB_CHAINS * SB_TILE
    return pl.pallas_call(
        _sb_prompt_kernel,
        grid=(SB_HEADS, seq // step),
        in_specs=[
            pl.BlockSpec((step, SB_HD), lambda h, i: (i, h)),
            pl.BlockSpec((None, SB_PAD + seq, SB_HD), lambda h, i: (0, 0, h)),
            pl.BlockSpec((None, SB_PAD + seq, SB_HD), lambda h, i: (1, 0, h)),
        ],
        out_specs=pl.BlockSpec((step, SB_HD), lambda h, i: (i, h)),
        out_shape=jax.ShapeDtypeStruct((seq, SB_HEADS * SB_HD), BF16),
        scratch_shapes=[pltpu.VMEM((step, SB_HD), BF16), pltpu.VMEM((step, SB_HD), F32),
                        pltpu.VMEM((step, 1), F32)],
        compiler_params=_params(2),
        name="sb_prompt",
    )(proj, kv16, kv16)


def _sb_sample_kernel(q_ref, kn_ref, vn_ref, pk_hbm, pv_hbm, o_ref, kbuf, vbuf, sem, acc_ref, r_ref,
                      *, rows, n_past):
    b = pl.program_id(0)
    slot = b % 2
    t = SB_PAST_TILE

    def copies(seq, tile_index, slot):
        start = pl.multiple_of(tile_index * t, t)
        return (
            pltpu.make_async_copy(pk_hbm.at[seq, :, pl.ds(start, t), :], kbuf.at[slot], sem.at[0, slot]),
            pltpu.make_async_copy(pv_hbm.at[seq, :, pl.ds(start, t), :], vbuf.at[slot], sem.at[1, slot]),
        )

    def fetch(seq, tile_index, slot):
        for cp in copies(seq, tile_index, slot):
            cp.start()

    def wait(seq, tile_index, slot):
        for cp in copies(seq, tile_index, slot):
            cp.wait()

    heads = [slice(h * SB_HD, (h + 1) * SB_HD) for h in range(SB_HEADS)]

    def past_tile():
        outs, sums = _sb_tiles(
            [q_ref[:, sl].astype(BF16) for sl in heads],
            [kbuf[slot, h].astype(BF16) for h in range(SB_HEADS)],
            [vbuf[slot, h].astype(BF16) for h in range(SB_HEADS)],
            _upper(t), None, [r_ref[h] for h in range(SB_HEADS)])
        r_max = None
        for h, (out, total) in enumerate(zip(outs, sums)):
            acc_ref[:, heads[h]] += out
            r_ref[h] = total
            r_h = jnp.max(total)
            r_max = r_h if r_max is None else jnp.maximum(r_max, r_h)
        return r_max

    @pl.when(b == 0)
    def _():
        fetch(b, n_past - 1, slot)

    @pl.when(b + 1 < pl.num_programs(0))
    def _():
        fetch(b + 1, n_past - 1, 1 - slot)

    row = lax.broadcasted_iota(jnp.int32, (rows, rows), 0)
    col = lax.broadcasted_iota(jnp.int32, (rows, rows), 1)
    outs, sums = _sb_tiles(
        [q_ref[:, sl].astype(BF16) for sl in heads], [kn_ref[:, sl] for sl in heads],
        [vn_ref[:, sl] for sl in heads], _upper(rows), col < row,
        [jnp.zeros((rows, 1), F32)] * SB_HEADS)
    for h, (out, total) in enumerate(zip(outs, sums)):
        acc_ref[:, heads[h]] = out
        r_ref[h] = total
    wait(b, n_past - 1, slot)
    r_max = past_tile()

    def cond(carry):
        tile_index, r_max = carry
        return jnp.logical_and(tile_index >= 0, r_max > EXP_ZERO_BELOW)

    def body(carry):
        tile_index, _ = carry
        fetch(b, tile_index, slot)
        wait(b, tile_index, slot)
        return tile_index - 1, past_tile()

    lax.while_loop(cond, body, (jnp.int32(n_past - 2), r_max))
    o_ref[...] = acc_ref[...].astype(BF16)


def _sb_sample(proj, kv16, past_k, past_v, row_block0, rows):
    n_seq, heads, past, hd = past_k.shape
    w = heads * hd
    new_block0 = row_block0 + SB_PAD // rows
    return pl.pallas_call(
        functools.partial(_sb_sample_kernel, rows=rows, n_past=past // SB_PAST_TILE),
        grid=(n_seq,),
        in_specs=[
            pl.BlockSpec((rows, w), lambda b: (row_block0 + b, 0)),
            pl.BlockSpec((None, rows, w), lambda b: (0, new_block0 + b, 0)),
            pl.BlockSpec((None, rows, w), lambda b: (1, new_block0 + b, 0)),
            pl.BlockSpec(memory_space=pl.ANY),
            pl.BlockSpec(memory_space=pl.ANY),
        ],
        out_specs=pl.BlockSpec((rows, w), lambda b: (b, 0)),
        out_shape=jax.ShapeDtypeStruct((n_seq * rows, w), BF16),
        scratch_shapes=[
            pltpu.VMEM((2, heads, SB_PAST_TILE, hd), F32),
            pltpu.VMEM((2, heads, SB_PAST_TILE, hd), F32),
            pltpu.SemaphoreType.DMA((2, 2)),
            pltpu.VMEM((rows, w), F32),
            pltpu.VMEM((heads, rows, 1), F32),
        ],
        compiler_params=_params(1),
        name="sb_sample",
    )(proj, kv16, kv16, past_k, past_v)


def _pad_rows(a, n):
    return jnp.pad(a, [(0, 0)] * (a.ndim - 2) + [(0, n - a.shape[-2]), (0, 0)])


def kernel(x_prompt, x_sample, state_mlstm_C, state_mlstm_n, state_mlstm_m, cache_sb_k, cache_sb_v,
           cache_mem_k, cache_mem_v, mem_prompt, ffn1_norm, ffn1_w_gate, ffn1_w_up, ffn1_w_down,
           ffn2_norm, ffn2_w_gate, ffn2_w_up, ffn2_w_down, mix_norm, a_w_in, a_b_i, a_b_f, a_head_norm,
           b_w_in, w_out, mem_norm, mem_w_k, mem_w_v, mem_q_norm, mem_k_norm, kv_norm, sb_w_k, sb_w_v):
    n_pb, seq, d = x_prompt.shape
    n_sb, dec_seq, _ = x_sample.shape
    assert n_pb == 1
    depth = ffn1_norm.shape[0]
    n_a = a_w_in.shape[0]
    n_slots = mem_prompt.shape[1]
    aq = A_HEADS * A_DQK
    av = A_HEADS * A_DV
    mem_w = MEM_HEADS * MEM_HD
    sb_w = SB_HEADS * SB_HD
    n_sample = n_sb * dec_seq
    sample_block0 = seq // dec_seq

    gains = lambda g: g.reshape(g.shape[0], 1, g.shape[-1])
    ffn1 = (gains(ffn1_norm), ffn1_w_gate, ffn1_w_up, ffn1_w_down)
    ffn2 = (gains(ffn2_norm), ffn2_w_gate, ffn2_w_up, ffn2_w_down)
    mix_gain = gains(mix_norm)
    n_main = 2 * aq + 2 * av
    w_gates = a_w_in[:, :, n_main:n_main + 2 * A_HEADS]
    a_cols = n_main + mem_w + LANES
    a_tile = 1792
    a_pad = -a_cols % a_tile
    a_w = jnp.concatenate(
        [a_w_in[:, :, :n_main], a_w_in[:, :, n_main + 2 * A_HEADS:], w_gates,
         jnp.zeros((n_a, d, LANES - 2 * A_HEADS + a_pad), F32)], axis=-1).astype(BF16)
    a_wgt = _pad_rows(jnp.swapaxes(w_gates, 1, 2), GATE_ROWS).astype(BF16)
    gate_bias = jnp.concatenate([a_b_i, a_b_f], axis=-1)
    b_w = b_w_in.astype(BF16)
    w_kv = jnp.stack([sb_w_k, sb_w_v]).astype(BF16)
    q_gain = mem_q_norm.reshape(depth, 1, MEM_HD)

    mk_p, mv_p = _mem_kv(mem_prompt[0], gains(mem_norm), mem_w_k, mem_w_v,
                         mem_k_norm.reshape(depth, 1, MEM_HD))
    mk_s = cache_mem_k.reshape(depth * n_sb, n_slots * MEM_HEADS, MEM_HD)
    mv_s = cache_mem_v.reshape(depth * n_sb, n_slots * MEM_HEADS, MEM_HD)

    past_k = jnp.transpose(cache_sb_k, (0, 2, 1, 3))
    past_v = jnp.transpose(cache_sb_v, (0, 2, 1, 3))

    x = jnp.concatenate([x_prompt[0], x_sample.reshape(n_sample, d)], axis=0)
    c_p, n_p, m_p, c_s, n_s, m_s = [], [], [], [], [], []
    k_p = v_p = k_s = v_s = kv16 = None
    mem_tile = 1024
    for l in range(depth):
        x = _ffn(x, *ffn1, l)
        if l < n_a:
            proj, gates_r = _proj(x, mix_gain, l, a_w, l, a_tile, wgt=a_wgt)
            bias_c = jnp.pad(gate_bias[l], (0, LANES - 2 * A_HEADS)).reshape(1, LANES)
            bias_r = jnp.pad(gate_bias[l], (0, GATE_ROWS - 2 * A_HEADS)).reshape(GATE_ROWS, 1)
            head_norm = a_head_norm[l].reshape(1, av)
            chunk_p = 256
            tok_p, c, n, m = _mlstm(
                proj, gates_r[:, :seq].reshape(1, GATE_ROWS, seq), bias_c, bias_r, head_norm,
                jnp.zeros((1, A_HEADS, A_DV, A_DQK), F32), jnp.zeros((1, SUBLANES, A_DQK), F32),
                jnp.zeros((1, SUBLANES, LANES), F32), 0, chunk_p, seq // chunk_p)
            c_p.append(c); n_p.append(n[:, :A_HEADS]); m_p.append(m[:, :A_HEADS, 0])
            gates_s = gates_r[:, seq:].reshape(GATE_ROWS, n_sb, dec_seq).transpose(1, 0, 2)
            m0 = jnp.broadcast_to(state_mlstm_m[l][:, :, None], (n_sb, A_HEADS, LANES))
            tok_s, c, n, m = _mlstm(
                proj, gates_s, bias_c, bias_r, head_norm, state_mlstm_C[l],
                _pad_rows(state_mlstm_n[l], SUBLANES), _pad_rows(m0, SUBLANES), sample_block0, dec_seq, 1)
            c_s.append(c); n_s.append(n[:, :A_HEADS]); m_s.append(m[:, :A_HEADS, 0])
            mq_block = n_main // mem_w
        else:
            proj = _proj(x, mix_gain, l, b_w, l - n_a, b_w.shape[-1])
            tok_p = _sb_prompt(proj, kv16, seq)
            tok_s = _sb_sample(proj, kv16, past_k, past_v, sample_block0, dec_seq)
            mq_block = sb_w // mem_w
        mo_p = _mem_attn(proj, mq_block, 0, mem_tile, seq // mem_tile, q_gain, l, mk_p, mv_p, l, 0)
        mo_s = _mem_attn(proj, mq_block, sample_block0 // MEM_GROUP, dec_seq * MEM_GROUP, n_sb // MEM_GROUP,
                         q_gain, l, mk_s, mv_s, l * n_sb // MEM_GROUP, MEM_GROUP)
        x = _out_proj(x, tok_p, tok_s, mo_p, mo_s, w_out, l)
        x = _ffn(x, *ffn2, l)
        if l == n_a - 1:
            k_p, v_p, k_s, v_s, kv16 = _kv_proj(x, kv_norm.reshape(1, d), w_kv, seq, n_sb, dec_seq)

    y_prompt = x[:seq].reshape(1, seq, d)
    y_sample = x[seq:].reshape(n_sb, dec_seq, d)
    k_p, v_p = (jnp.transpose(a, (1, 0, 2))[None] for a in (k_p, v_p))
    k_s, v_s = (jnp.transpose(a, (0, 2, 1, 3)) for a in (k_s, v_s))
    mem_out = lambda a: jnp.transpose(a, (0, 2, 1, 3))[:, None]
    return (y_prompt, y_sample, jnp.stack(c_p), jnp.stack(n_p), jnp.stack(m_p), k_p, v_p,
            mem_out(mk_p), mem_out(mv_p),
            jnp.stack(c_s), jnp.stack(n_s), jnp.stack(m_s), k_s, v_s)
```

```python
import functools

import jax
import jax.numpy as jnp
from jax import lax
from jax.experimental import pallas as pl
from jax.experimental.pallas import tpu as pltpu

F32 = jnp.float32
BF16 = jnp.bfloat16

RMS_EPS = 1e-6
A_HEADS = 6
A_DQK = 128
A_DV = 256
SB_HEADS = 12
SB_HD = 128
MEM_HEADS = 4
MEM_HD = 128
LANES = 128
SUBLANES = 8
GATE_ROWS = 16

ROW_TILE = 768
FFN_ROW_TILE = 1056
FF_TILE = 512
FF_HEAD_TILE = 256
VMEM_LIMIT = 58 * 1024 * 1024

MEM_GROUP = 4

SB_TILE = 256
SB_CHAINS = 8
SB_PAST_TILE = 256
SB_PAD = 1792

NT_DIMS = (((1,), (1,)), ((), ()))
TN_DIMS = (((0,), (0,)), ((), ()))
EXP_ZERO_BELOW = -104.0


def _params(n_axes):
    return pltpu.CompilerParams(
        dimension_semantics=("arbitrary",) * n_axes, vmem_limit_bytes=VMEM_LIMIT)


def _rms(x, g):
    ms = jnp.mean(x * x, axis=-1, keepdims=True)
    return x * lax.rsqrt(ms + RMS_EPS) * g


def _log_sigmoid(x):
    return jnp.minimum(x, 0.0) - jnp.log1p(jnp.exp(-jnp.abs(x)))


def _dot(a, b):
    return jnp.dot(a, b, preferred_element_type=F32)


def _dot_nt(a, b):
    return lax.dot_general(a, b, NT_DIMS, preferred_element_type=F32)


def _split3(x):
    x1 = x.astype(BF16)
    r = x - x1.astype(F32)
    x2 = r.astype(BF16)
    x3 = (r - x2.astype(F32)).astype(BF16)
    return x1, x2, x3


def _ffn_step(h_ref, wg, wu, wd, o_ref):
    h = h_ref[...]
    g = jnp.concatenate([_dot(h, w) for w in wg], axis=1)
    u = jnp.concatenate([_dot(h, w) for w in wu], axis=1)
    a = (g * jax.nn.sigmoid(g) * u * 0.5).astype(BF16)
    o_ref[...] += _dot(a, wd)


def _ffn_head_kernel(x_ref, g_ref, wg_ref, wu_ref, wd_ref, o_ref, wg16_ref, wu16_ref, wd16_ref, h_ref):
    @pl.when(pl.program_id(0) == 0)
    def _():
        x = x_ref[...]
        h_ref[...] = _rms(x, g_ref[...]).astype(BF16)
        o_ref[...] = x

    wg16_ref[...] = wg_ref[...].astype(BF16)
    wu16_ref[...] = wu_ref[...].astype(BF16)
    wd16_ref[...] = wd_ref[...].astype(BF16)
    _ffn_step(h_ref, [wg16_ref[...]], [wu16_ref[...]], wd16_ref[...], o_ref)


def _ffn_rest_kernel(x_ref, g_ref, wg_ref, wu_ref, wd_ref, o_ref, h_ref):
    @pl.when(pl.program_id(1) == 0)
    def _():
        x = x_ref[...]
        h_ref[...] = _rms(x, g_ref[...]).astype(BF16)
        o_ref[...] = x

    sub = range(wg_ref.shape[0])
    _ffn_step(h_ref, [wg_ref[t] for t in sub], [wu_ref[t] for t in sub], wd_ref[...], o_ref)


def _ffn(x, gain, wg, wu, wd, layer):
    m, d = x.shape
    f = wg.shape[-1]
    n_head = f // FF_HEAD_TILE
    w16 = lambda shape: jax.ShapeDtypeStruct(shape, BF16)
    x, wg16, wu16, wd16 = pl.pallas_call(
        _ffn_head_kernel,
        grid=(n_head,),
        in_specs=[
            pl.BlockSpec((FFN_ROW_TILE, d), lambda j: (0, 0)),
            pl.BlockSpec((None, 1, d), lambda j: (layer, 0, 0)),
            pl.BlockSpec((None, d, FF_HEAD_TILE), lambda j: (layer, 0, j)),
            pl.BlockSpec((None, d, FF_HEAD_TILE), lambda j: (layer, 0, j)),
            pl.BlockSpec((None, FF_HEAD_TILE, d), lambda j: (layer, j, 0)),
        ],
        out_specs=(
            pl.BlockSpec((FFN_ROW_TILE, d), lambda j: (0, 0)),
            pl.BlockSpec((None, d, FF_HEAD_TILE), lambda j: (j, 0, 0)),
            pl.BlockSpec((None, d, FF_HEAD_TILE), lambda j: (j, 0, 0)),
            pl.BlockSpec((FF_HEAD_TILE, d), lambda j: (j, 0)),
        ),
        out_shape=(jax.ShapeDtypeStruct((m, d), F32), w16((n_head, d, FF_HEAD_TILE)),
                   w16((n_head, d, FF_HEAD_TILE)), w16((f, d))),
        scratch_shapes=[pltpu.VMEM((FFN_ROW_TILE, d), BF16)],
        input_output_aliases={0: 0},
        compiler_params=_params(1),
        name="ffn_head",
    )(x, gain, wg, wu, wd)
    return pl.pallas_call(
        _ffn_rest_kernel,
        grid=(m // FFN_ROW_TILE - 1, f // FF_TILE),
        in_specs=[
            pl.BlockSpec((FFN_ROW_TILE, d), lambda i, j: (i + 1, 0)),
            pl.BlockSpec((None, 1, d), lambda i, j: (layer, 0, 0)),
            pl.BlockSpec((FF_TILE // FF_HEAD_TILE, d, FF_HEAD_TILE), lambda i, j: (j, 0, 0)),
            pl.BlockSpec((FF_TILE // FF_HEAD_TILE, d, FF_HEAD_TILE), lambda i, j: (j, 0, 0)),
            pl.BlockSpec((FF_TILE, d), lambda i, j: (j, 0)),
        ],
        out_specs=pl.BlockSpec((FFN_ROW_TILE, d), lambda i, j: (i + 1, 0)),
        out_shape=jax.ShapeDtypeStruct((m, d), F32),
        scratch_shapes=[pltpu.VMEM((FFN_ROW_TILE, d), BF16)],
        input_output_aliases={0: 0},
        compiler_params=_params(2),
        name="ffn",
    )(x, gain, wg16, wu16, wd16)


def _proj_kernel(x_ref, g_ref, w_ref, o_ref, h_ref):
    @pl.when(pl.program_id(1) == 0)
    def _():
        h_ref[...] = _rms(x_ref[...], g_ref[...]).astype(BF16)

    o_ref[...] = _dot(h_ref[...], w_ref[...]).astype(o_ref.dtype)


def _proj_gates_kernel(x_ref, g_ref, w_ref, wgt_ref, o_ref, gt_ref, h_ref):
    @pl.when(pl.program_id(1) == 0)
    def _():
        h = _rms(x_ref[...], g_ref[...]).astype(BF16)
        h_ref[...] = h
        gt_ref[...] = _dot_nt(wgt_ref[...], h)

    o_ref[...] = _dot(h_ref[...], w_ref[...]).astype(o_ref.dtype)


def _proj(x, gain, gain_layer, w, layer, col_tile, wgt=None):
    m, d = x.shape
    n = w.shape[-1]
    in_specs = [
        pl.BlockSpec((ROW_TILE, d), lambda i, j: (i, 0)),
        pl.BlockSpec((None, 1, d), lambda i, j: (gain_layer, 0, 0)),
        pl.BlockSpec((None, d, col_tile), lambda i, j: (layer, 0, j)),
    ]
    out_spec = pl.BlockSpec((ROW_TILE, col_tile), lambda i, j: (i, j))
    out_shape = jax.ShapeDtypeStruct((m, n), F32)
    common = dict(
        grid=(m // ROW_TILE, n // col_tile),
        scratch_shapes=[pltpu.VMEM((ROW_TILE, d), BF16)],
        compiler_params=_params(2),
    )
    if wgt is None:
        return pl.pallas_call(
            _proj_kernel, in_specs=in_specs, out_specs=out_spec, out_shape=out_shape,
            name="proj", **common)(x, gain, w)
    return pl.pallas_call(
        _proj_gates_kernel,
        in_specs=in_specs + [pl.BlockSpec((None, GATE_ROWS, d), lambda i, j: (layer, 0, 0))],
        out_specs=(out_spec, pl.BlockSpec((GATE_ROWS, ROW_TILE), lambda i, j: (0, i))),
        out_shape=(out_shape, jax.ShapeDtypeStruct((GATE_ROWS, m), F32)),
        name="proj_gates", **common)(x, gain, w, wgt)


def _kv_kernel(x_ref, g_ref, w_ref, kp_ref, vp_ref, ks_ref, vs_ref, o16_ref,
               *, n_pad, n_prompt, n_seq, rows):
    i = pl.program_id(0)

    @pl.when(i < n_pad)
    def _():
        o16_ref[...] = jnp.zeros(o16_ref.shape, BF16)

    @pl.when(i >= n_pad)
    def _():
        h = _rms(x_ref[...], g_ref[...]).astype(BF16)
        heads = [slice(hd * SB_HD, (hd + 1) * SB_HD) for hd in range(SB_HEADS)]
        for j, (p_ref, s_ref) in enumerate(((kp_ref, ks_ref), (vp_ref, vs_ref))):
            y = _dot(h, w_ref[j])
            o16_ref[j] = y.astype(BF16)

            @pl.when(i < n_pad + n_prompt)
            def _():
                for hd, sl in enumerate(heads):
                    p_ref[hd] = y[:, sl]

            @pl.when(i >= n_pad + n_prompt)
            def _():
                for b in range(n_seq):
                    for hd, sl in enumerate(heads):
                        s_ref[b, hd] = y[b * rows:(b + 1) * rows, sl]


def _kv_proj(x, gain, w, seq, n_seq, rows):
    m, d = x.shape
    n = w.shape[-1]
    tile = n_seq * rows
    n_prompt = seq // tile
    n_pad = SB_PAD // tile
    assert n_prompt * tile == seq and seq + tile == m and n_pad * tile == SB_PAD
    row_tile = lambda i: jnp.clip(i - n_pad, 0, n_prompt - 1)
    prompt_spec = pl.BlockSpec((SB_HEADS, tile, SB_HD), lambda i: (0, row_tile(i), 0))
    sample_spec = pl.BlockSpec((n_seq, SB_HEADS, rows, SB_HD), lambda i: (0, 0, 0, 0))
    prompt_shape = jax.ShapeDtypeStruct((SB_HEADS, seq, SB_HD), F32)
    sample_shape = jax.ShapeDtypeStruct((n_seq, SB_HEADS, rows, SB_HD), F32)
    return pl.pallas_call(
        functools.partial(_kv_kernel, n_pad=n_pad, n_prompt=n_prompt, n_seq=n_seq, rows=rows),
        grid=(n_pad + n_prompt + 1,),
        in_specs=[
            pl.BlockSpec((tile, d), lambda i: (jnp.maximum(i - n_pad, 0), 0)),
            pl.BlockSpec((1, d), lambda i: (0, 0)),
            pl.BlockSpec((2, d, n), lambda i: (0, 0, 0)),
        ],
        out_specs=(prompt_spec, prompt_spec, sample_spec, sample_spec,
                   pl.BlockSpec((2, tile, n), lambda i: (0, i, 0))),
        out_shape=(prompt_shape, prompt_shape, sample_shape, sample_shape,
                   jax.ShapeDtypeStruct((2, SB_PAD + m, n), BF16)),
        compiler_params=_params(1),
        name="kv_proj",
    )(x, gain, w)


def _out_proj_kernel(x_ref, tokp_ref, toks_ref, mop_ref, mos_ref, wt_ref, wm_ref, o_ref, w16_ref,
                     *, n_prompt):
    i = pl.program_id(0)
    kt = wt_ref.shape[0]

    @pl.when(i == 0)
    def _():
        w16_ref[:kt] = wt_ref[...].astype(BF16)
        w16_ref[kt:] = wm_ref[...].astype(BF16)

    @pl.when(i < n_prompt)
    def _():
        o_ref[...] = (x_ref[...] + _dot(tokp_ref[...], w16_ref[:kt])
                      + _dot(mop_ref[...], w16_ref[kt:]))

    @pl.when(i >= n_prompt)
    def _():
        o_ref[...] = (x_ref[...] + _dot(toks_ref[...], w16_ref[:kt])
                      + _dot(mos_ref[...], w16_ref[kt:]))


def _out_proj(x, tok_p, tok_s, mo_p, mo_s, w, layer):
    m, d = x.shape
    kt = tok_p.shape[1]
    km = mo_p.shape[1]
    tile = tok_s.shape[0]
    n_prompt = tok_p.shape[0] // tile
    assert n_prompt * tile == tok_p.shape[0] and (n_prompt + 1) * tile == m
    prompt = lambda i: (jnp.minimum(i, n_prompt - 1), 0)
    return pl.pallas_call(
        functools.partial(_out_proj_kernel, n_prompt=n_prompt),
        grid=(n_prompt + 1,),
        in_specs=[
            pl.BlockSpec((tile, d), lambda i: (i, 0)),
            pl.BlockSpec((tile, kt), prompt),
            pl.BlockSpec((tile, kt), lambda i: (0, 0)),
            pl.BlockSpec((tile, km), prompt),
            pl.BlockSpec((tile, km), lambda i: (0, 0)),
            pl.BlockSpec((None, kt, d), lambda i: (layer, 0, 0)),
            pl.BlockSpec((None, km, d), lambda i: (layer, kt // km, 0)),
        ],
        out_specs=pl.BlockSpec((tile, d), lambda i: (i, 0)),
        out_shape=jax.ShapeDtypeStruct((m, d), F32),
        scratch_shapes=[pltpu.VMEM((kt + km, d), BF16)],
        compiler_params=_params(1),
        name="out_proj",
    )(x, tok_p, tok_s, mo_p, mo_s, w, w)


def _mem_kv_kernel(mem_ref, g_ref, wk_ref, wv_ref, kg_ref, mk_ref, mv_ref):
    h = _rms(mem_ref[...], g_ref[...]).astype(BF16)
    k = _dot(h, wk_ref[...].astype(BF16))
    v = _dot(h, wv_ref[...].astype(BF16))
    kg = kg_ref[...]
    for hd in range(MEM_HEADS):
        sl = slice(hd * MEM_HD, (hd + 1) * MEM_HD)
        mk_ref[hd] = _rms(k[:, sl], kg)
        mv_ref[hd] = v[:, sl]


def _mem_kv(mem, gain, wk, wv, k_gain):
    depth, d, w = wk.shape
    n = mem.shape[0]
    out_spec = pl.BlockSpec((None, MEM_HEADS, n, MEM_HD), lambda l: (l, 0, 0, 0))
    out_shape = jax.ShapeDtypeStruct((depth, MEM_HEADS, n, MEM_HD), F32)
    return pl.pallas_call(
        _mem_kv_kernel,
        grid=(depth,),
        in_specs=[
            pl.BlockSpec((n, d), lambda l: (0, 0)),
            pl.BlockSpec((None, 1, d), lambda l: (l, 0, 0)),
            pl.BlockSpec((None, d, w), lambda l: (l, 0, 0)),
            pl.BlockSpec((None, d, w), lambda l: (l, 0, 0)),
            pl.BlockSpec((None, 1, MEM_HD), lambda l: (l, 0, 0)),
        ],
        out_specs=(out_spec, out_spec),
        out_shape=(out_shape, out_shape),
        compiler_params=_params(1),
        name="mem_kv",
    )(mem, gain, wk, wv, k_gain)


def _mem_attn_chains(mq_ref, qg_ref, chains, o_ref):
    qg = qg_ref[...]
    qn = [_rms(mq_ref[r, c], qg).astype(BF16) for r, c, _, _, _ in chains]
    s = [_dot_nt(q, k) * (MEM_HD ** -0.5) for q, (_, _, k, _, _) in zip(qn, chains)]
    s = [x if m is None else jnp.where(m, x, -jnp.inf) for x, (_, _, _, _, m) in zip(s, chains)]
    e = [jnp.exp(x - jnp.max(x, axis=-1, keepdims=True)) for x in s]
    p = [(x / jnp.sum(x, axis=-1, keepdims=True)).astype(BF16) for x in e]
    out = [_dot(x, v) for x, (_, _, _, v, _) in zip(p, chains)]
    for x, (r, c, _, _, _) in zip(out, chains):
        o_ref[r, c] = x.astype(BF16)


def _mem_attn_prompt_kernel(mq_ref, qg_ref, mk_ref, mv_ref, o_ref):
    _mem_attn_chains(mq_ref, qg_ref, [
        (slice(None), slice(hd * MEM_HD, (hd + 1) * MEM_HD), mk_ref[hd].astype(BF16),
         mv_ref[hd].astype(BF16), None) for hd in range(MEM_HEADS)], o_ref)


def _mem_attn_sample_kernel(mq_ref, qg_ref, mk_ref, mv_ref, o_ref):
    group, n_rows, _ = mk_ref.shape
    rows = mq_ref.shape[0] // group
    row_head = lax.broadcasted_iota(jnp.int32, (rows, n_rows), 1) % MEM_HEADS
    masks = [row_head == hd for hd in range(MEM_HEADS)]
    chains = []
    for g in range(group):
        k16 = mk_ref[g].astype(BF16)
        v16 = mv_ref[g].astype(BF16)
        chains += [(slice(g * rows, (g + 1) * rows), slice(hd * MEM_HD, (hd + 1) * MEM_HD),
                    k16, v16, masks[hd]) for hd in range(MEM_HEADS)]
    _mem_attn_chains(mq_ref, qg_ref, chains, o_ref)


def _mem_attn(proj, col_block, row_block0, rows, n_tiles, q_gain, layer, mk, mv, kv_index0, group):
    w = MEM_HEADS * MEM_HD
    if group:
        body = _mem_attn_sample_kernel
        kv_spec = pl.BlockSpec((group,) + mk.shape[1:], lambda b: (kv_index0 + b, 0, 0))
    else:
        body = _mem_attn_prompt_kernel
        kv_spec = pl.BlockSpec((None,) + mk.shape[1:], lambda b: (kv_index0, 0, 0, 0))
    return pl.pallas_call(
        body,
        grid=(n_tiles,),
        in_specs=[
            pl.BlockSpec((rows, w), lambda b: (row_block0 + b, col_block)),
            pl.BlockSpec((None, 1, MEM_HD), lambda b: (layer, 0, 0)),
            kv_spec,
            kv_spec,
        ],
        out_specs=pl.BlockSpec((rows, w), lambda b: (b, 0)),
        out_shape=jax.ShapeDtypeStruct((n_tiles * rows, w), BF16),
        compiler_params=_params(1),
        name="mem_attn",
    )(proj, q_gain, mk, mv)


def _mlstm_kernel(q_ref, k_ref, v_ref, og_ref, gc_ref, gr_ref, bc_ref, br_ref, hn_ref,
                  c0_ref, n0_ref, m0_ref, tok_ref, c_ref, n_ref, m_ref, *, chunk):
    L = chunk

    @pl.when(pl.program_id(1) == 0)
    def _():
        c_ref[...] = c0_ref[...]
        n_ref[...] = n0_ref[...]
        m_ref[...] = m0_ref[...]

    row = lax.broadcasted_iota(jnp.int32, (L, L), 0)
    col = lax.broadcasted_iota(jnp.int32, (L, L), 1)
    causal = col <= row
    tri = jnp.where(causal, 1.0, 0.0).astype(BF16)

    pre_c = gc_ref[...] + bc_ref[...]
    pre_r = gr_ref[...] + br_ref[...]
    lf_c = _log_sigmoid(pre_c)
    lf_r = _log_sigmoid(pre_r)
    cum_c = sum(_dot(tri, part) for part in _split3(lf_c))
    cum_r = sum(_dot_nt(part, tri) for part in _split3(lf_r))

    scale = A_DQK ** -0.5
    heads = range(A_HEADS)
    qk = [slice(h * A_DQK, (h + 1) * A_DQK) for h in heads]
    vv = [slice(h * A_DV, (h + 1) * A_DV) for h in heads]
    q = [q_ref[:, qk[h]] * scale for h in heads]
    q16 = [q[h].astype(BF16) for h in heads]
    k16 = [k_ref[:, qk[h]].astype(BF16) for h in heads]
    c_old = [c_ref[h] for h in heads]
    n_old = [n_ref[h:h + 1, :] for h in heads]
    m_old = [m_ref[h:h + 1, 0:1] for h in heads]
    i_c = [pre_c[:, h:h + 1] for h in heads]
    b_c = [cum_c[:, A_HEADS + h:A_HEADS + h + 1] for h in heads]
    i_r = [pre_r[h:h + 1, :] for h in heads]
    b_r = [cum_r[A_HEADS + h:A_HEADS + h + 1, :] for h in heads]

    qk_t = [_dot_nt(q16[h], k16[h]) for h in heads]
    q_c = [_dot_nt(q16[h], c_old[h].astype(BF16)) for h in heads]

    s, w_st, m_t = [], [], []
    for h in heads:
        d = jnp.where(causal, b_c[h] - b_r[h] + i_r[h], -jnp.inf)
        inter = b_c[h] + m_old[h]
        m_t.append(jnp.maximum(inter, jnp.max(d, axis=-1, keepdims=True)))
        w_st.append(jnp.exp(inter - m_t[h]))
        s.append(qk_t[h] * jnp.exp(d - m_t[h]))
    s_v = [_dot(s[h].astype(BF16), v_ref[:, vv[h]].astype(BF16)) for h in heads]

    for h in heads:
        num = s_v[h] + w_st[h] * q_c[h]
        den = (jnp.sum(s[h], axis=-1, keepdims=True)
               + w_st[h] * jnp.sum(q[h] * n_old[h], axis=-1, keepdims=True))
        hh = num * (1.0 / jnp.maximum(jnp.abs(den), jnp.exp(-m_t[h])))
        out = _rms(hh, hn_ref[:, vv[h]]) * jax.nn.sigmoid(og_ref[:, vv[h]])
        tok_ref[:, vv[h]] = out.astype(BF16)

    decay, w_k = [], []
    for h in heads:
        b_end = b_c[h][L - 1:L, :]
        g = b_end - b_c[h] + i_c[h]
        m_new = jnp.maximum(b_end + m_old[h], jnp.max(g, axis=0, keepdims=True))
        w_k.append(jnp.exp(g - m_new))
        decay.append(jnp.exp(b_end + m_old[h] - m_new))
        m_ref[h:h + 1, :] = jnp.broadcast_to(m_new, (1, LANES))
    vw_k = [lax.dot_general((v_ref[:, vv[h]] * w_k[h]).astype(BF16), k16[h], TN_DIMS,
                            preferred_element_type=F32) for h in heads]
    for h in heads:
        c_ref[h] = decay[h] * c_old[h] + vw_k[h]
        n_ref[h:h + 1, :] = (decay[h] * n_old[h]
                             + jnp.sum(w_k[h] * k_ref[:, qk[h]], axis=0, keepdims=True))


def _mlstm(proj, gates_r, bias_c, bias_r, head_norm, c0, n0, m0, row_block0, chunk, n_chunks):
    n_seq = c0.shape[0]
    aq = A_HEADS * A_DQK
    av = A_HEADS * A_DV
    gate_block = (2 * aq + 2 * av + MEM_HEADS * MEM_HD) // LANES

    def rows(b, c):
        return row_block0 + b * n_chunks + c

    state = lambda b, c: (b, 0, 0)
    return pl.pallas_call(
        functools.partial(_mlstm_kernel, chunk=chunk),
        grid=(n_seq, n_chunks),
        in_specs=[
            pl.BlockSpec((chunk, aq), lambda b, c: (rows(b, c), 0)),
            pl.BlockSpec((chunk, aq), lambda b, c: (rows(b, c), 1)),
            pl.BlockSpec((chunk, av), lambda b, c: (rows(b, c), 1)),
            pl.BlockSpec((chunk, av), lambda b, c: (rows(b, c), 2)),
            pl.BlockSpec((chunk, LANES), lambda b, c: (rows(b, c), gate_block)),
            pl.BlockSpec((None, GATE_ROWS, chunk), lambda b, c: (b, 0, c)),
            pl.BlockSpec((1, LANES), lambda b, c: (0, 0)),
            pl.BlockSpec((GATE_ROWS, 1), lambda b, c: (0, 0)),
            pl.BlockSpec((1, av), lambda b, c: (0, 0)),
            pl.BlockSpec((None, A_HEADS, A_DV, A_DQK), lambda b, c: (b, 0, 0, 0)),
            pl.BlockSpec((None, SUBLANES, A_DQK), state),
            pl.BlockSpec((None, SUBLANES, LANES), state),
        ],
        out_specs=(
            pl.BlockSpec((chunk, av), lambda b, c: (b * n_chunks + c, 0)),
            pl.BlockSpec((None, A_HEADS, A_DV, A_DQK), lambda b, c: (b, 0, 0, 0)),
            pl.BlockSpec((None, SUBLANES, A_DQK), state),
            pl.BlockSpec((None, SUBLANES, LANES), state),
        ),
        out_shape=(
            jax.ShapeDtypeStruct((n_seq * n_chunks * chunk, av), BF16),
            jax.ShapeDtypeStruct((n_seq, A_HEADS, A_DV, A_DQK), F32),
            jax.ShapeDtypeStruct((n_seq, SUBLANES, A_DQK), F32),
            jax.ShapeDtypeStruct((n_seq, SUBLANES, LANES), F32),
        ),
        compiler_params=_params(2),
        name="mlstm",
    )(proj, proj, proj, proj, proj, gates_r, bias_c, bias_r, head_norm, c0, n0, m0)


def _sb_scores(q16, k16, valid):
    z = _dot_nt(q16, k16) * (SB_HD ** -0.5)
    sp = jnp.maximum(z, 0.0) + jnp.log(1.0 + jnp.exp(-jnp.abs(z)))
    if valid is not None:
        sp = jnp.where(valid, sp, 0.0)
    return z, sp


def _sb_newer(sp, upper):
    s1 = sp.astype(BF16)
    s2 = (sp - s1.astype(F32)).astype(BF16)
    return _dot(s1, upper) + _dot(s2, upper)


def _sb_weights(z, sp, newer, valid, r_prev):
    a = jnp.exp(z - sp - newer + r_prev)
    if valid is not None:
        a = jnp.where(valid, a, 0.0)
    return a.astype(BF16)


def _sb_tiles(qs, ks, vs, upper, valid, r_prevs):
    scores = [_sb_scores(q, k, valid) for q, k in zip(qs, ks)]
    newer = [_sb_newer(sp, upper) for _, sp in scores]
    outs = [_dot(_sb_weights(z, sp, nw, valid, r), v)
            for (z, sp), nw, r, v in zip(scores, newer, r_prevs, vs)]
    sums = [r - jnp.sum(sp, axis=-1, keepdims=True) for (_, sp), r in zip(scores, r_prevs)]
    return outs, sums


def _upper(n):
    row = lax.broadcasted_iota(jnp.int32, (n, n), 0)
    col = lax.broadcasted_iota(jnp.int32, (n, n), 1)
    return jnp.where(row > col, 1.0, 0.0).astype(BF16)


def _sb_prompt_kernel(q_ref, k_ref, v_ref, o_ref, q16_ref, acc_ref, r_ref):
    t = SB_TILE
    base = pl.program_id(1) * SB_CHAINS + SB_PAD // SB_TILE
    upper = _upper(t)
    row = lax.broadcasted_iota(jnp.int32, (t, t), 0)
    col = lax.broadcasted_iota(jnp.int32, (t, t), 1)
    q16_ref[...] = q_ref[...].astype(BF16)

    def walk(j, diagonal):
        rows = [slice(c * t, (c + 1) * t) for c in range(SB_CHAINS)]
        starts = [pl.multiple_of((base + c - j) * t, t) for c in range(SB_CHAINS)]
        outs, sums = _sb_tiles(
            [q16_ref[r, :] for r in rows],
            [k_ref[pl.ds(s, t), :] for s in starts],
            [v_ref[pl.ds(s, t), :] for s in starts],
            upper, col < row if diagonal else None,
            [jnp.zeros((t, 1), F32) if diagonal else r_ref[r, :] for r in rows])
        r_max = None
        for r, out, total in zip(rows, outs, sums):
            if diagonal:
                acc_ref[r, :] = out
            else:
                acc_ref[r, :] += out
            r_ref[r, :] = total
            r_c = jnp.max(total)
            r_max = r_c if r_max is None else jnp.maximum(r_max, r_c)
        return r_max

    def cond(carry):
        j, r_max = carry
        return jnp.logical_and(j <= base, r_max > EXP_ZERO_BELOW)

    def body(carry):
        j, _ = carry
        return j + 1, walk(j, False)

    lax.while_loop(cond, body, (jnp.int32(1), walk(0, True)))
    o_ref[...] = acc_ref[...].astype(BF16)


def _sb_prompt(proj, kv16, seq):
    step = SB_CHAINS * SB_TILE
    return pl.pallas_call(
        _sb_prompt_kernel,
        grid=(SB_HEADS, seq // step),
        in_specs=[
            pl.BlockSpec((step, SB_HD), lambda h, i: (i, h)),
            pl.BlockSpec((None, SB_PAD + seq, SB_HD), lambda h, i: (0, 0, h)),
            pl.BlockSpec((None, SB_PAD + seq, SB_HD), lambda h, i: (1, 0, h)),
        ],
        out_specs=pl.BlockSpec((step, SB_HD), lambda h, i: (i, h)),
        out_shape=jax.ShapeDtypeStruct((seq, SB_HEADS * SB_HD), BF16),
        scratch_shapes=[pltpu.VMEM((step, SB_HD), BF16), pltpu.VMEM((step, SB_HD), F32),
                        pltpu.VMEM((step, 1), F32)],
        compiler_params=_params(2),
        name="sb_prompt",
    )(proj, kv16, kv16)


def _sb_sample_kernel(q_ref, kn_ref, vn_ref, pk_hbm, pv_hbm, o_ref, kbuf, vbuf, sem, acc_ref, r_ref,
                      *, rows, n_past):
    b = pl.program_id(0)
    slot = b % 2
    t = SB_PAST_TILE

    def copies(seq, tile_index, slot):
        start = pl.multiple_of(tile_index * t, t)
        return (
            pltpu.make_async_copy(pk_hbm.at[seq, :, pl.ds(start, t), :], kbuf.at[slot], sem.at[0, slot]),
            pltpu.make_async_copy(pv_hbm.at[seq, :, pl.ds(start, t), :], vbuf.at[slot], sem.at[1, slot]),
        )

    def fetch(seq, tile_index, slot):
        for cp in copies(seq, tile_index, slot):
            cp.start()

    def wait(seq, tile_index, slot):
        for cp in copies(seq, tile_index, slot):
            cp.wait()

    heads = [slice(h * SB_HD, (h + 1) * SB_HD) for h in range(SB_HEADS)]

    def past_tile():
        outs, sums = _sb_tiles(
            [q_ref[:, sl].astype(BF16) for sl in heads],
            [kbuf[slot, h].astype(BF16) for h in range(SB_HEADS)],
            [vbuf[slot, h].astype(BF16) for h in range(SB_HEADS)],
            _upper(t), None, [r_ref[h] for h in range(SB_HEADS)])
        r_max = None
        for h, (out, total) in enumerate(zip(outs, sums)):
            acc_ref[:, heads[h]] += out
            r_ref[h] = total
            r_h = jnp.max(total)
            r_max = r_h if r_max is None else jnp.maximum(r_max, r_h)
        return r_max

    @pl.when(b == 0)
    def _():
        fetch(b, n_past - 1, slot)

    @pl.when(b + 1 < pl.num_programs(0))
    def _():
        fetch(b + 1, n_past - 1, 1 - slot)

    row = lax.broadcasted_iota(jnp.int32, (rows, rows), 0)
    col = lax.broadcasted_iota(jnp.int32, (rows, rows), 1)
    outs, sums = _sb_tiles(
        [q_ref[:, sl].astype(BF16) for sl in heads], [kn_ref[:, sl] for sl in heads],
        [vn_ref[:, sl] for sl in heads], _upper(rows), col < row,
        [jnp.zeros((rows, 1), F32)] * SB_HEADS)
    for h, (out, total) in enumerate(zip(outs, sums)):
        acc_ref[:, heads[h]] = out
        r_ref[h] = total
    wait(b, n_past - 1, slot)
    r_max = past_tile()

    def cond(carry):
        tile_index, r_max = carry
        return jnp.logical_and(tile_index >= 0, r_max > EXP_ZERO_BELOW)

    def body(carry):
        tile_index, _ = carry
        fetch(b, tile_index, slot)
        wait(b, tile_index, slot)
        return tile_index - 1, past_tile()

    lax.while_loop(cond, body, (jnp.int32(n_past - 2), r_max))
    o_ref[...] = acc_ref[...].astype(BF16)


def _sb_sample(proj, kv16, past_k, past_v, row_block0, rows):
    n_seq, heads, past, hd = past_k.shape
    w = heads * hd
    new_block0 = row_block0 + SB_PAD // rows
    return pl.pallas_call(
        functools.partial(_sb_sample_kernel, rows=rows, n_past=past // SB_PAST_TILE),
        grid=(n_seq,),
        in_specs=[
            pl.BlockSpec((rows, w), lambda b: (row_block0 + b, 0)),
            pl.BlockSpec((None, rows, w), lambda b: (0, new_block0 + b, 0)),
            pl.BlockSpec((None, rows, w), lambda b: (1, new_block0 + b, 0)),
            pl.BlockSpec(memory_space=pl.ANY),
            pl.BlockSpec(memory_space=pl.ANY),
        ],
        out_specs=pl.BlockSpec((rows, w), lambda b: (b, 0)),
        out_shape=jax.ShapeDtypeStruct((n_seq * rows, w), BF16),
        scratch_shapes=[
            pltpu.VMEM((2, heads, SB_PAST_TILE, hd), F32),
            pltpu.VMEM((2, heads, SB_PAST_TILE, hd), F32),
            pltpu.SemaphoreType.DMA((2, 2)),
            pltpu.VMEM((rows, w), F32),
            pltpu.VMEM((heads, rows, 1), F32),
        ],
        compiler_params=_params(1),
        name="sb_sample",
    )(proj, kv16, kv16, past_k, past_v)


def _pad_rows(a, n):
    return jnp.pad(a, [(0, 0)] * (a.ndim - 2) + [(0, n - a.shape[-2]), (0, 0)])


def kernel(x_prompt, x_sample, state_mlstm_C, state_mlstm_n, state_mlstm_m, cache_sb_k, cache_sb_v,
           cache_mem_k, cache_mem_v, mem_prompt, ffn1_norm, ffn1_w_gate, ffn1_w_up, ffn1_w_down,
           ffn2_norm, ffn2_w_gate, ffn2_w_up, ffn2_w_down, mix_norm, a_w_in, a_b_i, a_b_f, a_head_norm,
           b_w_in, w_out, mem_norm, mem_w_k, mem_w_v, mem_q_norm, mem_k_norm, kv_norm, sb_w_k, sb_w_v):
    n_pb, seq, d = x_prompt.shape
    n_sb, dec_seq, _ = x_sample.shape
    assert n_pb == 1
    depth = ffn1_norm.shape[0]
    n_a = a_w_in.shape[0]
    n_slots = mem_prompt.shape[1]
    aq = A_HEADS * A_DQK
    av = A_HEADS * A_DV
    mem_w = MEM_HEADS * MEM_HD
    sb_w = SB_HEADS * SB_HD
    n_sample = n_sb * dec_seq
    sample_block0 = seq // dec_seq

    gains = lambda g: g.reshape(g.shape[0], 1, g.shape[-1])
    ffn1 = (gains(ffn1_norm), ffn1_w_gate, ffn1_w_up, ffn1_w_down)
    ffn2 = (gains(ffn2_norm), ffn2_w_gate, ffn2_w_up, ffn2_w_down)
    mix_gain = gains(mix_norm)
    n_main = 2 * aq + 2 * av
    w_gates = a_w_in[:, :, n_main:n_main + 2 * A_HEADS]
    a_cols = n_main + mem_w + LANES
    a_tile = 1792
    a_pad = -a_cols % a_tile
    a_w = jnp.concatenate(
        [a_w_in[:, :, :n_main], a_w_in[:, :, n_main + 2 * A_HEADS:], w_gates,
         jnp.zeros((n_a, d, LANES - 2 * A_HEADS + a_pad), F32)], axis=-1).astype(BF16)
    a_wgt = _pad_rows(jnp.swapaxes(w_gates, 1, 2), GATE_ROWS).astype(BF16)
    gate_bias = jnp.concatenate([a_b_i, a_b_f], axis=-1)
    b_w = b_w_in.astype(BF16)
    w_kv = jnp.stack([sb_w_k, sb_w_v]).astype(BF16)
    q_gain = mem_q_norm.reshape(depth, 1, MEM_HD)

    mk_p, mv_p = _mem_kv(mem_prompt[0], gains(mem_norm), mem_w_k, mem_w_v,
                         mem_k_norm.reshape(depth, 1, MEM_HD))
    mk_s = cache_mem_k.reshape(depth * n_sb, n_slots * MEM_HEADS, MEM_HD)
    mv_s = cache_mem_v.reshape(depth * n_sb, n_slots * MEM_HEADS, MEM_HD)

    past_k = jnp.transpose(cache_sb_k, (0, 2, 1, 3))
    past_v = jnp.transpose(cache_sb_v, (0, 2, 1, 3))

    x = jnp.concatenate([x_prompt[0], x_sample.reshape(n_sample, d)], axis=0)
    c_p, n_p, m_p, c_s, n_s, m_s = [], [], [], [], [], []
    k_p = v_p = k_s = v_s = kv16 = None
    mem_tile = 1024
    for l in range(depth):
        x = _ffn(x, *ffn1, l)
        if l < n_a:
            proj, gates_r = _proj(x, mix_gain, l, a_w, l, a_tile, wgt=a_wgt)
            bias_c = jnp.pad(gate_bias[l], (0, LANES - 2 * A_HEADS)).reshape(1, LANES)
            bias_r = jnp.pad(gate_bias[l], (0, GATE_ROWS - 2 * A_HEADS)).reshape(GATE_ROWS, 1)
            head_norm = a_head_norm[l].reshape(1, av)
            chunk_p = 256
            tok_p, c, n, m = _mlstm(
                proj, gates_r[:, :seq].reshape(1, GATE_ROWS, seq), bias_c, bias_r, head_norm,
                jnp.zeros((1, A_HEADS, A_DV, A_DQK), F32), jnp.zeros((1, SUBLANES, A_DQK), F32),
                jnp.zeros((1, SUBLANES, LANES), F32), 0, chunk_p, seq // chunk_p)
            c_p.append(c); n_p.append(n[:, :A_HEADS]); m_p.append(m[:, :A_HEADS, 0])
            gates_s = gates_r[:, seq:].reshape(GATE_ROWS, n_sb, dec_seq).transpose(1, 0, 2)
            m0 = jnp.broadcast_to(state_mlstm_m[l][:, :, None], (n_sb, A_HEADS, LANES))
            tok_s, c, n, m = _mlstm(
                proj, gates_s, bias_c, bias_r, head_norm, state_mlstm_C[l],
                _pad_rows(state_mlstm_n[l], SUBLANES), _pad_rows(m0, SUBLANES), sample_block0, dec_seq, 1)
            c_s.append(c); n_s.append(n[:, :A_HEADS]); m_s.append(m[:, :A_HEADS, 0])
            mq_block = n_main // mem_w
        else:
            proj = _proj(x, mix_gain, l, b_w, l - n_a, b_w.shape[-1])
            tok_p = _sb_prompt(proj, kv16, seq)
            tok_s = _sb_sample(proj, kv16, past_k, past_v, sample_block0, dec_seq)
            mq_block = sb_w // mem_w
        mo_p = _mem_attn(proj, mq_block, 0, mem_tile, seq // mem_tile, q_gain, l, mk_p, mv_p, l, 0)
        mo_s = _mem_attn(proj, mq_block, sample_block0 // MEM_GROUP, dec_seq * MEM_GROUP, n_sb // MEM_GROUP,
                         q_gain, l, mk_s, mv_s, l * n_sb // MEM_GROUP, MEM_GROUP)
        x = _out_proj(x, tok_p, tok_s, mo_p, mo_s, w_out, l)
        x = _ffn(x, *ffn2, l)
        if l == n_a - 1:
            k_p, v_p, k_s, v_s, kv16 = _kv_proj(x, kv_norm.reshape(1, d), w_kv, seq, n_sb, dec_seq)

    y_prompt = x[:seq].reshape(1, seq, d)
    y_sample = x[seq:].reshape(n_sb, dec_seq, d)
    k_p, v_p = (jnp.transpose(a, (1, 0, 2))[None] for a in (k_p, v_p))
    k_s, v_s = (jnp.transpose(a, (0, 2, 1, 3)) for a in (k_s, v_s))
    mem_out = lambda a: jnp.transpose(a, (0, 2, 1, 3))[:, None]
    return (y_prompt, y_sample, jnp.stack(c_p), jnp.stack(n_p), jnp.stack(m_p), k_p, v_p,
            mem_out(mk_p), mem_out(mv_p),
            jnp.stack(c_s), jnp.stack(n_s), jnp.stack(m_s), k_s, v_s)
```
